```python
import math
import jax
import jax.numpy as jnp
from jax import lax
import numpy as np

D_MODEL = 2048
BATCH = 1
SEQ = 8192
DEPTH = 1

CHUNK = 64
SB_HEAD_DIM = 64
SB_WIDTH = D_MODEL // 2
SB_HEADS = SB_WIDTH // SB_HEAD_DIM
SB_BLOCK = 128
SSM_HEAD_DIM = 64
SSM_INNER = D_MODEL - SB_WIDTH
SSM_HEADS = SSM_INNER // SSM_HEAD_DIM
SSM_GROUPS = 2
SSM_STATE = 128
SSM_CONV = 4
SSM_CONV_DIM = SSM_INNER + 2 * SSM_GROUPS * SSM_STATE
MIX_WIDTH = SB_WIDTH + SSM_INNER
IN_PROJ = 3 * SB_WIDTH + SSM_INNER + SSM_CONV_DIM + SSM_HEADS
IN_SPLITS = (SB_WIDTH, 2 * SB_WIDTH, 3 * SB_WIDTH, 3 * SB_WIDTH + SSM_INNER,
             3 * SB_WIDTH + SSM_INNER + SSM_CONV_DIM)
N_EXPERTS = 32
TOP_K = 4
D_FF = D_MODEL
SWIGLU_LIMIT = 7.0
SWIGLU_ALPHA = 1.702
MOE_BLOCK = 128
RMS_EPS = 1e-5

kernel_name = "hymba_stickbreak_ssd_moe_block"


def rmsnorm(x, g):
    xf = x.astype(jnp.float32)
    y = xf * lax.rsqrt(jnp.mean(xf * xf, axis=-1, keepdims=True) + RMS_EPS)
    return (y * g.astype(jnp.float32)).astype(x.dtype)


def stick_breaking_attention(q, k, v):
    b, s, h, dh = q.shape
    nb = s // SB_BLOCK
    f32 = jnp.float32
    kf = k.astype(f32)
    vf = v.astype(f32)
    q_blocks = jnp.moveaxis(q.astype(f32).reshape(b, nb, SB_BLOCK, h, dh), 1, 0)
    key_pos = jnp.arange(s)
    scale = dh ** -0.5

    def one_block(args):
        q_blk, blk = args
        q_pos = blk * SB_BLOCK + jnp.arange(SB_BLOCK)
        past = key_pos[None, :] < q_pos[:, None]
        logits = jnp.einsum('bqhd,bkhd->bhqk', q_blk, kf) * scale
        log_keep = jnp.where(past, jax.nn.log_sigmoid(-logits), 0.0)
        between = lax.cumsum(log_keep, axis=3, reverse=True) - log_keep
        weights = jnp.where(past, jnp.exp(jax.nn.log_sigmoid(logits) + between), 0.0)
        return jnp.einsum('bhqk,bkhd->bqhd', weights, vf)

    out = lax.map(one_block, (q_blocks, jnp.arange(nb)))
    return jnp.moveaxis(out, 0, 1).reshape(b, s, h, dh)


def causal_depthwise_conv(x, w, bias):
    kw = w.shape[0]
    y = lax.conv_general_dilated(
        x, w[:, None, :].astype(x.dtype), window_strides=(1,), padding=[(kw - 1, 0)],
        dimension_numbers=('NWC', 'WIO', 'NWC'), feature_group_count=x.shape[-1])
    return y + bias.astype(x.dtype)


def segsum(a):
    t = a.shape[-1]
    a_rep = jnp.broadcast_to(a[..., None], a.shape + (t,))
    a_rep = jnp.where(jnp.tril(jnp.ones((t, t), bool), -1), a_rep, 0.0)
    seg = jnp.cumsum(a_rep, axis=-2)
    return jnp.where(jnp.tril(jnp.ones((t, t), bool), 0), seg, -jnp.inf)


def ssd_chunked(xs, a_dt, bmat, cmat):
    b, s, h, p = xs.shape
    g, n = bmat.shape[2], bmat.shape[3]
    r = h // g
    c = s // CHUNK
    x_c = xs.reshape(b, c, CHUNK, g, r, p)
    a_c = a_dt.reshape(b, c, CHUNK, g, r).transpose(0, 3, 4, 1, 2)
    b_c = bmat.reshape(b, c, CHUNK, g, n)
    c_c = cmat.reshape(b, c, CHUNK, g, n)
    a_cum = jnp.cumsum(a_c, axis=-1)
    decay_in = jnp.exp(segsum(a_c))
    cb = jnp.einsum('bclgn,bcsgn->bgcls', c_c, b_c)
    y_diag = jnp.einsum('bgrcls,bcsgrp->bclgrp', cb[:, :, None] * decay_in, x_c)
    decay_states = jnp.exp(a_cum[..., -1:] - a_cum)
    states = jnp.einsum('bclgn,bgrcl,bclgrp->cbgrpn', b_c, decay_states, x_c)
    chunk_decay = jnp.exp(a_cum[..., -1]).transpose(3, 0, 1, 2)

    def step(carry, inp):
        st, dec = inp
        return carry * dec[..., None, None] + st, carry

    init = jnp.zeros(states.shape[1:], states.dtype)
    _, prev_states = lax.scan(step, init, (states, chunk_decay))
    y_off = jnp.einsum('bclgn,cbgrpn,bgrcl->bclgrp', c_c, prev_states, jnp.exp(a_cum))
    return (y_diag + y_off).reshape(b, s, h, p)


def mamba2_mixer(z, xbc, dt, conv_w, conv_b, dt_bias, a_log, d_skip, g_ssm):
    b, s, _ = xbc.shape
    f32 = jnp.float32
    xbc = jax.nn.silu(causal_depthwise_conv(xbc, conv_w, conv_b)).astype(f32)
    xs, bmat, cmat = jnp.split(xbc, (SSM_INNER, SSM_INNER + SSM_GROUPS * SSM_STATE), axis=-1)
    xs = xs.reshape(b, s, SSM_HEADS, SSM_HEAD_DIM)
    bmat = bmat.reshape(b, s, SSM_GROUPS, SSM_STATE)
    cmat = cmat.reshape(b, s, SSM_GROUPS, SSM_STATE)
    dt = jax.nn.softplus(dt.astype(f32) + dt_bias.astype(f32))
    a = -jnp.exp(a_log.astype(f32))
    y = ssd_chunked(xs * dt[..., None], dt * a, bmat, cmat)
    y = y + d_skip.astype(f32)[:, None] * xs
    y = y.reshape(b, s, SSM_INNER) * jax.nn.silu(z.astype(f32))
    y = y.reshape(b, s, SSM_GROUPS, SSM_INNER // SSM_GROUPS)
    y = y * lax.rsqrt(jnp.mean(y * y, axis=-1, keepdims=True) + RMS_EPS)
    return y.reshape(b, s, SSM_INNER) * g_ssm.astype(f32)


def moe_ffn(h, w_router, b_router, w_gate_up, b_gate_up, w_down, b_down):
    b, s, d = h.shape
    t = b * s
    f32 = jnp.float32
    hf = h.reshape(t, d)
    logits = (hf @ w_router).astype(f32) + b_router.astype(f32)
    top_vals, top_idx = lax.top_k(logits, TOP_K)
    gates = jax.nn.softmax(top_vals, axis=-1)
    flat_e = top_idx.reshape(-1)
    order = jnp.argsort(flat_e, stable=True)
    sorted_e = flat_e[order]
    sorted_tok = (order // TOP_K).astype(jnp.int32)
    sorted_gate = gates.reshape(-1)[order]
    counts = jnp.bincount(flat_e, length=N_EXPERTS)
    padded = (counts + MOE_BLOCK - 1) // MOE_BLOCK * MOE_BLOCK
    start = jnp.cumsum(counts) - counts
    pend = jnp.cumsum(padded)
    pstart = pend - padded
    dest = pstart[sorted_e] + (jnp.arange(t * TOP_K) - start[sorted_e])
    n_blocks = -(-(t * TOP_K) // MOE_BLOCK) + N_EXPERTS
    n_slots = n_blocks * MOE_BLOCK
    slot_tok = jnp.zeros((n_slots,), jnp.int32).at[dest].set(sorted_tok)
    slot_gate = jnp.zeros((n_slots,), f32).at[dest].set(sorted_gate)
    block_e = jnp.minimum(
        jnp.searchsorted(pend, jnp.arange(n_blocks) * MOE_BLOCK, side='right'), N_EXPERTS - 1)
    xs = hf[slot_tok].reshape(n_blocks, MOE_BLOCK, d)

    def expert_block(args):
        xb, e = args
        gu = (xb @ w_gate_up[e] + b_gate_up[e]).astype(f32)
        gate = jnp.minimum(gu[:, 0::2], SWIGLU_LIMIT)
        up = jnp.clip(gu[:, 1::2], -SWIGLU_LIMIT, SWIGLU_LIMIT)
        act = gate * jax.nn.sigmoid(SWIGLU_ALPHA * gate) * (up + 1.0)
        return (act.astype(xb.dtype) @ w_down[e] + b_down[e]).astype(f32)

    ys = lax.map(expert_block, (xs, block_e)).reshape(n_slots, d)
    out = jnp.zeros((t, d), f32).at[slot_tok].add(ys * slot_gate[:, None])
    return out.reshape(b, s, d).astype(h.dtype)


def setup_inputs(seed: int = 0) -> dict:
    key = jax.random.key(seed)
    ks = jax.random.split(key, 18)
    f32 = jnp.float32

    def nrm(k, shape, scale):
        return jax.random.normal(k, shape, f32) * scale

    x = nrm(ks[0], (BATCH, SEQ, D_MODEL), 1.0)
    g_mix = 1.0 + nrm(ks[1], (DEPTH, D_MODEL), 0.02)
    w_in = nrm(ks[2], (DEPTH, D_MODEL, IN_PROJ), D_MODEL ** -0.5)
    conv_w = nrm(ks[3], (DEPTH, SSM_CONV, SSM_CONV_DIM), SSM_CONV ** -0.5)
    conv_b = nrm(ks[4], (DEPTH, SSM_CONV_DIM), 0.01)
    dt0 = jnp.exp(jax.random.uniform(ks[5], (DEPTH, SSM_HEADS), f32,
                                     math.log(1e-3), math.log(1e-1)))
    dt_bias = dt0 + jnp.log(-jnp.expm1(-dt0))
    a_log = jnp.log(jax.random.uniform(ks[6], (DEPTH, SSM_HEADS), f32, 1.0, 16.0))
    d_skip = 1.0 + nrm(ks[7], (DEPTH, SSM_HEADS), 0.1)
    g_ssm = 1.0 + nrm(ks[8], (DEPTH, SSM_INNER), 0.02)
    w_out = nrm(ks[9], (DEPTH, MIX_WIDTH, D_MODEL), MIX_WIDTH ** -0.5)
    g_ffn = 1.0 + nrm(ks[10], (DEPTH, D_MODEL), 0.02)
    w_router = nrm(ks[11], (DEPTH, D_MODEL, N_EXPERTS), D_MODEL ** -0.5)
    b_router = nrm(ks[12], (DEPTH, N_EXPERTS), 0.01)
    w_gate_up = nrm(ks[13], (DEPTH, N_EXPERTS, D_MODEL, 2 * D_FF), D_MODEL ** -0.5)
    b_gate_up = nrm(ks[14], (DEPTH, N_EXPERTS, 2 * D_FF), 0.01)
    w_down = nrm(ks[15], (DEPTH, N_EXPERTS, D_FF, D_MODEL), D_FF ** -0.5)
    b_down = nrm(ks[16], (DEPTH, N_EXPERTS, D_MODEL), 0.01)
    g_final = 1.0 + nrm(ks[17], (D_MODEL,), 0.02)
    return {"x": x, "g_mix": g_mix, "w_in": w_in, "conv_w": conv_w, "conv_b": conv_b,
            "dt_bias": dt_bias, "a_log": a_log, "d_skip": d_skip, "g_ssm": g_ssm,
            "w_out": w_out, "g_ffn": g_ffn, "w_router": w_router, "b_router": b_router,
            "w_gate_up": w_gate_up, "b_gate_up": b_gate_up, "w_down": w_down,
            "b_down": b_down, "g_final": g_final}


def reference(x, g_mix, w_in, conv_w, conv_b, dt_bias, a_log, d_skip, g_ssm, w_out,
              g_ffn, w_router, b_router, w_gate_up, b_gate_up, w_down, b_down, g_final):
    b, s, _ = x.shape
    for l in range(DEPTH):
        h = rmsnorm(x, g_mix[l])
        proj = h @ w_in[l]
        q, k, v, z, xbc, dt = jnp.split(proj, IN_SPLITS, axis=-1)
        heads = (b, s, SB_HEADS, SB_HEAD_DIM)
        attn = stick_breaking_attention(q.reshape(heads), k.reshape(heads), v.reshape(heads))
        attn = attn.reshape(b, s, SB_WIDTH).astype(x.dtype)
        ssm = mamba2_mixer(z, xbc, dt, conv_w[l], conv_b[l], dt_bias[l], a_log[l],
                           d_skip[l], g_ssm[l]).astype(x.dtype)
        x = x + jnp.concatenate([attn, ssm], axis=-1) @ w_out[l]
        x = x + moe_ffn(rmsnorm(x, g_ffn[l]), w_router[l], b_router[l], w_gate_up[l],
                        b_gate_up[l], w_down[l], b_down[l])
    return rmsnorm(x, g_final)
```

```python
import functools

import jax
import jax.numpy as jnp
from jax import lax
from jax.experimental import pallas as pl
from jax.experimental.pallas import tpu as pltpu

F32 = jnp.float32
BF16 = jnp.bfloat16
HIGHEST = lax.Precision.HIGHEST

LANES = 128
HEAD_DIM = 64
SB_WIDTH = 1024
SSM_INNER = 1024
SSM_GROUPS = 2
SSM_STATE = 128
SSM_CONV = 4
SSM_CONV_DIM = SSM_INNER + 2 * SSM_GROUPS * SSM_STATE
SSM_HEADS = SSM_INNER // HEAD_DIM
N_EXPERTS = 32
TOP_K = 4
SWIGLU_LIMIT = 7.0
SWIGLU_ALPHA = 1.702
RMS_EPS = 1e-5

VMEM_LIMIT = 56 * 1024 * 1024

ATTN_DEAD = -110.0


def _cparams(*sem):
    return pltpu.CompilerParams(dimension_semantics=sem, vmem_limit_bytes=VMEM_LIMIT)


def _rms(x, g):
    return x * lax.rsqrt(jnp.mean(x * x, axis=-1, keepdims=True) + RMS_EPS) * g


def _sigmoid(x):
    return 1.0 / (1.0 + jnp.exp(-x))


def _softplus(x):
    return jnp.maximum(x, 0.0) + jnp.log1p(jnp.exp(-jnp.abs(x)))


def _cast_kernel(w_ref, o_ref):
    o_ref[...] = w_ref[...].astype(o_ref.dtype)


def _to_bf16(w, rows=256):
    k, n = w.shape
    return pl.pallas_call(
        _cast_kernel,
        grid=(k // rows,),
        in_specs=[pl.BlockSpec((rows, n), lambda i: (i, 0))],
        out_specs=pl.BlockSpec((rows, n), lambda i: (i, 0)),
        out_shape=jax.ShapeDtypeStruct((k, n), BF16),
        compiler_params=_cparams("arbitrary"),
        name="cast_bf16",
    )(w)


def _in_proj_kernel(x_ref, g_ref, w_ref, wdt_ref, q_ref, k_ref, v_ref, z_ref, xbc_ref, dt_ref):
    hb = _rms(x_ref[...], g_ref[...]).astype(BF16)

    def mm(a, b):
        return jnp.dot(hb, w_ref[:, a:b], preferred_element_type=F32)

    o = 0
    q_ref[...] = (mm(o, o + SB_WIDTH) * (HEAD_DIM ** -0.5)).astype(BF16)
    o += SB_WIDTH
    k_ref[...] = mm(o, o + SB_WIDTH).astype(BF16)
    o += SB_WIDTH
    v_ref[...] = mm(o, o + SB_WIDTH).astype(BF16)
    o += SB_WIDTH
    z_ref[...] = mm(o, o + SSM_INNER)
    o += SSM_INNER
    xbc_ref[...] = mm(o, o + SSM_CONV_DIM)
    dt_ref[...] = jnp.dot(hb, wdt_ref[...].astype(BF16), preferred_element_type=F32)


def _in_proj(x, g, w_bf, w_dt, tm=256):
    t, d = x.shape
    n = w_bf.shape[1]
    row = lambda w: pl.BlockSpec((tm, w), lambda i: (i, 0))
    outs = [(SB_WIDTH, BF16)] * 3 + [(SSM_INNER, F32), (SSM_CONV_DIM, F32), (LANES, F32)]
    return pl.pallas_call(
        _in_proj_kernel,
        grid=(t // tm,),
        in_specs=[row(d),
                  pl.BlockSpec((1, d), lambda i: (0, 0)),
                  pl.BlockSpec((d, n), lambda i: (0, 0), pipeline_mode=pl.Buffered(1)),
                  pl.BlockSpec((d, LANES), lambda i: (0, 0))],
        out_specs=[row(w) for w, _ in outs],
        out_shape=[jax.ShapeDtypeStruct((t, w), dt) for w, dt in outs],
        compiler_params=_cparams("arbitrary"),
        name="in_proj",
    )(x, g, w_bf, w_dt)


def _attn_kernel(q_ref, k_ref, v_ref, o_ref, acc_ref, *, tq):
    i = pl.program_id(1)
    nsub = tq // LANES
    row = lax.broadcasted_iota(jnp.int32, (LANES, LANES), 0)
    col = lax.broadcasted_iota(jnp.int32, (LANES, LANES), 1)
    past = col < row
    lo = col < HEAD_DIM
    jj = lax.broadcasted_iota(jnp.int32, (LANES, 2 * LANES), 0)
    ss = lax.broadcasted_iota(jnp.int32, (LANES, 2 * LANES), 1)
    neg_later = jnp.where((ss >= LANES) | (jj > ss), -1.0, 0.0).astype(BF16)
    nt = (((1,), (1,)), ((), ()))

    def sub_body(sub, c):
        qb = i * nsub + sub
        r0 = pl.multiple_of(sub * LANES, LANES)
        q = q_ref[pl.ds(r0, LANES), :]
        for hd in range(2):
            qh = jnp.where(lo if hd == 0 else jnp.logical_not(lo), q, jnp.zeros_like(q))

            def tile(kb, carry, diag):
                k0 = pl.multiple_of(kb * LANES, LANES)
                kt = k_ref[pl.ds(k0, LANES), :]
                vt = v_ref[pl.ds(k0, LANES), :]
                l = lax.dot_general(qh, kt, nt, preferred_element_type=F32)
                sp = _softplus(l)
                lk = jnp.where(past, sp, 0.0) if diag else sp
                r = jnp.dot(lk.astype(BF16), neg_later, preferred_element_type=F32)
                w = jnp.exp((l - sp) + r[:, :LANES] + carry)
                if diag:
                    w = jnp.where(past, w, 0.0)
                pv = jnp.dot(w.astype(BF16), vt, preferred_element_type=F32)
                if diag:
                    acc_ref[hd] = pv
                else:
                    acc_ref[hd] += pv
                return carry + r[:, LANES:]

            carry0 = tile(qb, jnp.zeros((LANES, LANES), F32), True)

            def cond(st):
                kb, carry = st
                return jnp.logical_and(kb >= 0, jnp.max(carry) > ATTN_DEAD)

            def body(st):
                kb, carry = st
                return kb - 1, tile(kb, carry, False)

            lax.while_loop(cond, body, (qb - 1, carry0))
        o_ref[pl.ds(r0, LANES), :] = jnp.where(lo, acc_ref[0], acc_ref[1]).astype(o_ref.dtype)
        return c

    lax.fori_loop(0, nsub, sub_body, 0)


def _attention(q, k, v, tq=512):
    t, w = q.shape
    tq = min(tq, t)
    return pl.pallas_call(
        functools.partial(_attn_kernel, tq=tq),
        grid=(w // LANES, t // tq),
        in_specs=[pl.BlockSpec((tq, LANES), lambda p, i: (i, p)),
                  pl.BlockSpec((t, LANES), lambda p, i: (0, p)),
                  pl.BlockSpec((t, LANES), lambda p, i: (0, p))],
        out_specs=pl.BlockSpec((tq, LANES), lambda p, i: (i, p)),
        out_shape=jax.ShapeDtypeStruct((t, w), BF16),
        scratch_shapes=[pltpu.VMEM((2, LANES, LANES), F32)],
        compiler_params=_cparams("arbitrary", "arbitrary"),
        name="sb_attention",
    )(q, k, v)


def _ssd_kernel(z_ref, xbc_ref, dt_ref, cw_ref, cb_ref, dtb_ref, alog_ref, dsk_ref, g_ref, e_ref,
                o_ref, state_ref, xcat_ref):
    c = pl.program_id(0)
    L = z_ref.shape[0]
    half = SSM_INNER // SSM_GROUPS
    pad = 8

    @pl.when(c == 0)
    def _():
        state_ref[...] = jnp.zeros(state_ref.shape, F32)
        xcat_ref[0:pad, :] = jnp.zeros((pad, SSM_CONV_DIM), F32)

    xcat_ref[pad:pad + L, :] = xbc_ref[...]
    acc = jnp.broadcast_to(cb_ref[...], (L, SSM_CONV_DIM))
    for kk in range(SSM_CONV):
        s = pad - (SSM_CONV - 1) + kk
        acc = acc + cw_ref[kk:kk + 1, :] * xcat_ref[s:s + L, :]
    xcat_ref[0:pad, :] = xcat_ref[L:L + pad, :]
    xc = acc * _sigmoid(acc)
    xs = xc[:, :SSM_INNER]
    bm = xc[:, SSM_INNER:SSM_INNER + SSM_GROUPS * SSM_STATE]
    cm = xc[:, SSM_INNER + SSM_GROUPS * SSM_STATE:]

    dtv = _softplus(dt_ref[...] + dtb_ref[...])
    adt = dtv * (-jnp.exp(alog_ref[...]))
    ri = lax.broadcasted_iota(jnp.int32, (L, L), 0)
    ci = lax.broadcasted_iota(jnp.int32, (L, L), 1)
    causal = ri >= ci
    tri = jnp.where(causal, 1.0, 0.0).astype(F32)
    acum = jnp.dot(tri, adt, precision=HIGHEST, preferred_element_type=F32)
    expand = e_ref[...]
    acx = jnp.dot(acum, expand, precision=HIGHEST, preferred_element_type=F32)
    dtx = jnp.dot(dtv, expand, precision=HIGHEST, preferred_element_type=F32)
    alx = acx[L - 1:L, :]
    xdt = xs * dtx
    acum_t = acum.T
    lo = lax.broadcasted_iota(jnp.int32, (L, LANES), 1) < HEAD_DIM
    nt = (((1,), (1,)), ((), ()))

    y_parts = []
    for g in range(SSM_GROUPS):
        cg = cm[:, g * SSM_STATE:(g + 1) * SSM_STATE]
        bg = bm[:, g * SSM_STATE:(g + 1) * SSM_STATE]
        y_off = jnp.dot(cg, state_ref[:, g * half:(g + 1) * half], precision=HIGHEST,
                        preferred_element_type=F32)
        cb = lax.dot_general(cg, bg, nt, precision=HIGHEST, preferred_element_type=F32)
        for p in range(half // LANES):
            pair = g * (half // LANES) + p
            xp = xdt[:, pair * LANES:(pair + 1) * LANES]
            yh = []
            for hh in range(2):
                h = 2 * pair + hh
                seg = acum[:, h:h + 1] - acum_t[h:h + 1, :]
                dec = jnp.where(causal, jnp.exp(jnp.minimum(seg, 0.0)), 0.0)
                yh.append(jnp.dot(cb * dec, xp, precision=HIGHEST, preferred_element_type=F32))
            y_parts.append(jnp.where(lo, yh[0], yh[1])
                           + y_off[:, p * LANES:(p + 1) * LANES]
                           * jnp.exp(acx[:, pair * LANES:(pair + 1) * LANES]))
        xd = xdt[:, g * half:(g + 1) * half] * jnp.exp(alx[:, g * half:(g + 1) * half]
                                                       - acx[:, g * half:(g + 1) * half])
        upd = jnp.dot(bg.T, xd, precision=HIGHEST, preferred_element_type=F32)
        state_ref[:, g * half:(g + 1) * half] = (
            jnp.exp(alx[:, g * half:(g + 1) * half]) * state_ref[:, g * half:(g + 1) * half] + upd)

    y = jnp.concatenate(y_parts, axis=1) + dsk_ref[...] * xs
    zz = z_ref[...]
    y = y * (zz * _sigmoid(zz))
    outs = []
    for g in range(SSM_GROUPS):
        yg = y[:, g * half:(g + 1) * half]
        outs.append(yg * lax.rsqrt(jnp.mean(yg * yg, axis=-1, keepdims=True) + RMS_EPS))
    o_ref[...] = (jnp.concatenate(outs, axis=1) * g_ref[...]).astype(o_ref.dtype)


def _ssd(z, xbc, dt, conv_w, conv_b, dt_bias, a_log, d_skip, g_ssm, chunk=128):
    t = z.shape[0]
    chunk = min(chunk, t)
    padh = lambda a: jnp.pad(a.reshape(1, -1), ((0, 0), (0, LANES - a.shape[-1])))
    heads = jnp.arange(SSM_INNER, dtype=jnp.int32) // HEAD_DIM
    expand = (jnp.arange(LANES, dtype=jnp.int32)[:, None] == heads[None, :]).astype(F32)
    row = lambda w: pl.BlockSpec((chunk, w), lambda c: (c, 0))
    full = lambda a: pl.BlockSpec(a.shape, lambda c: (0, 0))
    params = [conv_w, conv_b.reshape(1, -1), padh(dt_bias), padh(a_log),
              jnp.repeat(d_skip, HEAD_DIM).reshape(1, -1), g_ssm.reshape(1, -1), expand]
    return pl.pallas_call(
        _ssd_kernel,
        grid=(t // chunk,),
        in_specs=[row(SSM_INNER), row(SSM_CONV_DIM), row(LANES)] + [full(a) for a in params],
        out_specs=row(SSM_INNER),
        out_shape=jax.ShapeDtypeStruct((t, SSM_INNER), BF16),
        scratch_shapes=[pltpu.VMEM((SSM_STATE, SSM_INNER), F32),
                        pltpu.VMEM((chunk + 8, SSM_CONV_DIM), F32)],
        compiler_params=_cparams("arbitrary"),
        name="ssd_mixer",
    )(z, xbc, dt, *params)


def _out_proj_kernel(x_ref, a_ref, s_ref, w_ref, o_ref):
    o_ref[...] = (x_ref[...]
                  + jnp.dot(a_ref[...], w_ref[0:SB_WIDTH, :], preferred_element_type=F32)
                  + jnp.dot(s_ref[...], w_ref[SB_WIDTH:, :], preferred_element_type=F32))


def _out_proj(x, attn, ssm, w_bf, tm=512):
    t, d = x.shape
    tm = min(tm, t)
    return pl.pallas_call(
        _out_proj_kernel,
        grid=(t // tm,),
        in_specs=[pl.BlockSpec((tm, d), lambda i: (i, 0)),
                  pl.BlockSpec((tm, SB_WIDTH), lambda i: (i, 0)),
                  pl.BlockSpec((tm, SSM_INNER), lambda i: (i, 0)),
                  pl.BlockSpec(w_bf.shape, lambda i: (0, 0), pipeline_mode=pl.Buffered(1))],
        out_specs=pl.BlockSpec((tm, d), lambda i: (i, 0)),
        out_shape=jax.ShapeDtypeStruct((t, d), F32),
        compiler_params=_cparams("arbitrary"),
        name="out_proj",
    )(x, attn, ssm, w_bf)


def _router_kernel(x_ref, g_ref, wt_ref, b_ref, idx_ref, gate_ref, pos_ref, cnt_ref, base_ref):
    i = pl.program_id(0)
    tm = x_ref.shape[0]

    @pl.when(i == 0)
    def _():
        base_ref[...] = jnp.zeros(base_ref.shape, F32)

    h = _rms(x_ref[...], g_ref[...])
    logits = lax.dot_general(wt_ref[...], h, (((1,), (1,)), ((), ())), precision=HIGHEST,
                             preferred_element_type=F32) + b_ref[:, 0:1]
    eio = lax.broadcasted_iota(jnp.int32, (N_EXPERTS, tm), 0).astype(F32)
    work = logits
    vals, hots = [], []
    for k in range(TOP_K):
        m = jnp.max(work, axis=0, keepdims=True)
        idx = jnp.min(jnp.where(work == m, eio, float(N_EXPERTS)), axis=0, keepdims=True)
        hot = eio == idx
        work = jnp.where(hot, -jnp.inf, work)
        vals.append(m)
        hots.append(hot)
        idx_ref[k:k + 1, :] = idx.astype(jnp.int32)
    ex = [jnp.exp(v - vals[0]) for v in vals]
    den = ex[0] + ex[1] + ex[2] + ex[3]
    picked = jnp.zeros((N_EXPERTS, tm), F32)
    for k in range(TOP_K):
        gate_ref[k:k + 1, :] = ex[k] / den
        picked = picked + jnp.where(hots[k], 1.0, 0.0)
    ti = lax.broadcasted_iota(jnp.int32, (tm, tm), 0)
    tj = lax.broadcasted_iota(jnp.int32, (tm, tm), 1)
    before = jnp.where(ti < tj, 1.0, 0.0).astype(BF16)
    rank = jnp.dot(picked.astype(BF16), before, preferred_element_type=F32) + base_ref[:, 0:1]
    for k in range(TOP_K):
        pos = jnp.sum(jnp.where(hots[k], rank, 0.0), axis=0, keepdims=True)
        pos_ref[k:k + 1, :] = pos.astype(jnp.int32)
    base_ref[...] = base_ref[...] + jnp.sum(picked, axis=1, keepdims=True)
    cnt_ref[...] = base_ref[...].astype(jnp.int32)


def _router(x1, g, w_router, b_router, tm=512):
    t, d = x1.shape
    tm = min(tm, t)
    wt = w_router.T
    b = jnp.broadcast_to(b_router.reshape(-1, 1), (N_EXPERTS, LANES))
    tok = lambda dt: (pl.BlockSpec((TOP_K, tm), lambda i: (0, i)), jax.ShapeDtypeStruct((TOP_K, t), dt))
    specs = [tok(jnp.int32), tok(F32), tok(jnp.int32),
             (pl.BlockSpec((N_EXPERTS, LANES), lambda i: (0, 0)),
              jax.ShapeDtypeStruct((N_EXPERTS, LANES), jnp.int32))]
    return pl.pallas_call(
        _router_kernel,
        grid=(t // tm,),
        in_specs=[pl.BlockSpec((tm, d), lambda i: (i, 0)),
                  pl.BlockSpec((1, d), lambda i: (0, 0)),
                  pl.BlockSpec((N_EXPERTS, d), lambda i: (0, 0)),
                  pl.BlockSpec((N_EXPERTS, LANES), lambda i: (0, 0))],
        out_specs=[s for s, _ in specs],
        out_shape=[s for _, s in specs],
        scratch_shapes=[pltpu.VMEM((N_EXPERTS, LANES), F32)],
        compiler_params=_cparams("arbitrary"),
        name="router",
    )(x1, g, wt, b)


def _row_copy(src_hbm, row, dst, r, sem):
    return pltpu.make_async_copy(src_hbm.at[pl.ds(row, 1), :], dst.at[pl.ds(r, 1), :], sem)


def _gather_kernel(tok_ref, x_hbm, g_ref, o_ref, buf_ref, sem):
    tm = o_ref.shape[0]
    s0 = pl.program_id(0) * tm

    def start(r, c):
        _row_copy(x_hbm, tok_ref[s0 + r], buf_ref, r, sem.at[0]).start()
        return c

    def wait(r, c):
        _row_copy(x_hbm, tok_ref[s0 + r], buf_ref, r, sem.at[0]).wait()
        return c

    lax.fori_loop(0, tm, start, 0)
    lax.fori_loop(0, tm, wait, 0)
    o_ref[...] = _rms(buf_ref[...], g_ref[...]).astype(o_ref.dtype)


def _gather_norm(x1, g, slot_tok, tm):
    t, d = x1.shape
    n_slots = slot_tok.shape[0]
    return pl.pallas_call(
        _gather_kernel,
        grid_spec=pltpu.PrefetchScalarGridSpec(
            num_scalar_prefetch=1,
            grid=(n_slots // tm,),
            in_specs=[pl.BlockSpec(memory_space=pl.ANY),
                      pl.BlockSpec((1, d), lambda i, tok: (0, 0))],
            out_specs=pl.BlockSpec((tm, d), lambda i, tok: (i, 0)),
            scratch_shapes=[pltpu.VMEM((tm, d), F32), pltpu.SemaphoreType.DMA((1,))]),
        out_shape=jax.ShapeDtypeStruct((n_slots, d), BF16),
        compiler_params=_cparams("arbitrary"),
        name="moe_gather",
    )(slot_tok, x1, g)


def _gmm_kernel(be_ref, nu_ref, x_ref, w_ref, b_ref, *rest, swiglu):
    if swiglu:
        pick_ref, o_ref, wbf_ref = rest
    else:
        o_ref, wbf_ref = rest
    m = pl.program_id(1)
    used = m < nu_ref[0]
    me = jnp.minimum(m, nu_ref[0] - 1)
    new_expert = jnp.logical_or(m == 0, be_ref[me] != be_ref[jnp.maximum(me - 1, 0)])

    @pl.when(jnp.logical_and(used, new_expert))
    def _():
        wbf_ref[...] = w_ref[...].astype(BF16)

    @pl.when(jnp.logical_not(used))
    def _():
        o_ref[...] = jnp.zeros(o_ref.shape, o_ref.dtype)

    @pl.when(used)
    def _():
        y = jnp.dot(x_ref[...], wbf_ref[...], preferred_element_type=F32) + b_ref[...]
        if not swiglu:
            o_ref[...] = y.astype(o_ref.dtype)
            return
        tm, n2 = y.shape
        even = lax.broadcasted_iota(jnp.int32, (tm, LANES), 1) % 2 == 0
        for cblk in range(n2 // (2 * LANES)):
            prods = []
            for s in range(2):
                c0 = cblk * 2 * LANES + s * LANES
                yc = y[:, c0:c0 + LANES]
                gate = jnp.minimum(yc, SWIGLU_LIMIT)
                gate = gate * _sigmoid(SWIGLU_ALPHA * gate)
                up = jnp.clip(yc, -SWIGLU_LIMIT, SWIGLU_LIMIT) + 1.0
                a = jnp.where(even, gate, up)
                prods.append((a * pltpu.roll(a, LANES - 1, 1)).astype(BF16))
            pr = jnp.concatenate(prods, axis=1)
            o_ref[:, cblk * LANES:(cblk + 1) * LANES] = jnp.dot(
                pr, pick_ref[...], preferred_element_type=F32).astype(o_ref.dtype)


def _grouped_matmul(x, w, b, block_e, n_used, *, tm, tn, swiglu, out_dtype):
    n_slots, kdim = x.shape
    n_exp, _, n = w.shape
    nb = n_slots // tm
    n_out = n // 2 if swiglu else n
    tn_out = tn // 2 if swiglu else tn
    blk = lambda m, nu: jnp.minimum(m, nu[0] - 1)
    in_specs = [pl.BlockSpec((tm, kdim), lambda j, m, be, nu: (blk(m, nu), 0)),
                pl.BlockSpec((None, kdim, tn), lambda j, m, be, nu: (be[blk(m, nu)], 0, j)),
                pl.BlockSpec((None, 1, tn), lambda j, m, be, nu: (be[blk(m, nu)], 0, j))]
    args = [x, w, b.reshape(n_exp, 1, n)]
    if swiglu:
        src = jnp.arange(2 * LANES, dtype=jnp.int32)[:, None]
        dst = jnp.arange(LANES, dtype=jnp.int32)[None, :]
        args.append((src == 2 * dst).astype(BF16))
        in_specs.append(pl.BlockSpec((2 * LANES, LANES), lambda j, m, be, nu: (0, 0)))
    return pl.pallas_call(
        functools.partial(_gmm_kernel, swiglu=swiglu),
        grid_spec=pltpu.PrefetchScalarGridSpec(
            num_scalar_prefetch=2,
            grid=(n // tn, nb),
            in_specs=in_specs,
            out_specs=pl.BlockSpec((tm, tn_out), lambda j, m, be, nu: (m, j)),
            scratch_shapes=[pltpu.VMEM((kdim, tn), BF16)]),
        out_shape=jax.ShapeDtypeStruct((n_slots, n_out), out_dtype),
        compiler_params=_cparams("arbitrary", "arbitrary"),
        name="moe_gate_up" if swiglu else "moe_down",
    )(block_e, n_used, *args)


def _combine_kernel(dest_ref, x_ref, gate_ref, g_ref, ys_hbm, o_ref, buf_ref, sem, *, norm):
    tc = x_ref.shape[0]
    t = dest_ref.shape[0] // TOP_K
    t0 = pl.program_id(0) * tc

    def start(r, c):
        for k in range(TOP_K):
            _row_copy(ys_hbm, dest_ref[k * t + t0 + r], buf_ref.at[k], r, sem.at[0]).start()
        return c

    def wait(r, c):
        for k in range(TOP_K):
            _row_copy(ys_hbm, dest_ref[k * t + t0 + r], buf_ref.at[k], r, sem.at[0]).wait()
        return c

    lax.fori_loop(0, tc, start, 0)
    lax.fori_loop(0, tc, wait, 0)
    y = x_ref[...]
    for k in range(TOP_K):
        y = y + gate_ref[:, k:k + 1] * buf_ref[k]
    o_ref[...] = _rms(y, g_ref[...]) if norm else y


def _combine(x1, gates_t, dest_flat, ys, g_final, norm, tc=128):
    t, d = x1.shape
    tc = min(tc, t)
    return pl.pallas_call(
        functools.partial(_combine_kernel, norm=norm),
        grid_spec=pltpu.PrefetchScalarGridSpec(
            num_scalar_prefetch=1,
            grid=(t // tc,),
            in_specs=[pl.BlockSpec((tc, d), lambda i, dest: (i, 0)),
                      pl.BlockSpec((tc, TOP_K), lambda i, dest: (i, 0)),
                      pl.BlockSpec((1, d), lambda i, dest: (0, 0)),
                      pl.BlockSpec(memory_space=pl.ANY)],
            out_specs=pl.BlockSpec((tc, d), lambda i, dest: (i, 0)),
            scratch_shapes=[pltpu.VMEM((TOP_K, tc, d), F32), pltpu.SemaphoreType.DMA((1,))]),
        out_shape=jax.ShapeDtypeStruct((t, d), F32),
        compiler_params=_cparams("arbitrary"),
        name="moe_combine",
    )(dest_flat, x1, gates_t, g_final, ys)


def _mixer(x, g_mix, w_in, conv_w, conv_b, dt_bias, a_log, d_skip, g_ssm, w_out):
    n_main = 3 * SB_WIDTH + SSM_INNER + SSM_CONV_DIM
    w_dt = jnp.pad(w_in[:, n_main:], ((0, 0), (0, LANES - SSM_HEADS)))
    q, k, v, z, xbc, dt = _in_proj(x, g_mix.reshape(1, -1), _to_bf16(w_in), w_dt)
    attn = _attention(q, k, v)
    ssm = _ssd(z, xbc, dt, conv_w, conv_b, dt_bias, a_log, d_skip, g_ssm)
    return _out_proj(x, attn, ssm, _to_bf16(w_out))


def _moe(x1, g_ffn, w_router, b_router, w_gate_up, b_gate_up, w_down, b_down, g_out, norm, tm=256):
    t, d = x1.shape
    g_ffn = g_ffn.reshape(1, -1)
    idx, gates, pos, cnt = _router(x1, g_ffn, w_router, b_router)
    counts = cnt[:, 0]
    padded = (counts + tm - 1) // tm * tm
    pend = jnp.cumsum(padded)
    pstart = pend - padded
    dest = pstart[idx] + pos
    n_blocks = (t * TOP_K) // tm + N_EXPERTS
    n_slots = n_blocks * tm
    n_used = (pend[-1] // tm).astype(jnp.int32).reshape(1)
    block_e = jnp.minimum(
        jnp.searchsorted(pend, jnp.arange(n_blocks, dtype=jnp.int32) * tm, side="right"),
        N_EXPERTS - 1).astype(jnp.int32)
    tok = jnp.broadcast_to(jnp.arange(t, dtype=jnp.int32)[None, :], (TOP_K, t))
    slot_tok = jnp.zeros((n_slots,), jnp.int32).at[dest.reshape(-1)].set(tok.reshape(-1))

    xs = _gather_norm(x1, g_ffn, slot_tok, tm)
    act = _grouped_matmul(xs, w_gate_up, b_gate_up, block_e, n_used,
                          tm=tm, tn=1024, swiglu=True, out_dtype=BF16)
    ys = _grouped_matmul(act, w_down, b_down, block_e, n_used,
                         tm=tm, tn=1024, swiglu=False, out_dtype=F32)
    return _combine(x1, gates.T, dest.reshape(-1), ys, g_out.reshape(1, -1), norm)


def kernel(x, g_mix, w_in, conv_w, conv_b, dt_bias, a_log, d_skip, g_ssm, w_out, g_ffn, w_router,
           b_router, w_gate_up, b_gate_up, w_down, b_down, g_final):
    b, s, d = x.shape
    depth = g_mix.shape[0]
    outs = []
    for bi in range(b):
        xb = x[bi]
        for l in range(depth):
            x1 = _mixer(xb, g_mix[l], w_in[l], conv_w[l], conv_b[l], dt_bias[l], a_log[l],
                        d_skip[l], g_ssm[l], w_out[l])
            xb = _moe(x1, g_ffn[l], w_router[l], b_router[l], w_gate_up[l], b_gate_up[l],
                      w_down[l], b_down[l], g_final, norm=(l == depth - 1))
        outs.append(xb)
    return outs[0][None] if b == 1 else jnp.stack(outs)
```

```python
import functools

import jax
import jax.numpy as jnp
from jax import lax
from jax.experimental import pallas as pl
from jax.experimental.pallas import tpu as pltpu

F32 = jnp.float32
BF16 = jnp.bfloat16
HIGHEST = lax.Precision.HIGHEST

LANES = 128
HEAD_DIM = 64
SB_WIDTH = 1024
SSM_INNER = 1024
SSM_GROUPS = 2
SSM_STATE = 128
SSM_CONV = 4
SSM_CONV_DIM = SSM_INNER + 2 * SSM_GROUPS * SSM_STATE
SSM_HEADS = SSM_INNER // HEAD_DIM
N_EXPERTS = 32
TOP_K = 4
SWIGLU_LIMIT = 7.0
SWIGLU_ALPHA = 1.702
RMS_EPS = 1e-5

VMEM_LIMIT = 56 * 1024 * 1024

ATTN_DEAD = -110.0


def _cparams(*sem):
    return pltpu.CompilerParams(dimension_semantics=sem, vmem_limit_bytes=VMEM_LIMIT)


def _rms(x, g):
    return x * lax.rsqrt(jnp.mean(x * x, axis=-1, keepdims=True) + RMS_EPS) * g


def _sigmoid(x):
    return 1.0 / (1.0 + jnp.exp(-x))


def _softplus(x):
    return jnp.maximum(x, 0.0) + jnp.log(1.0 + jnp.exp(-jnp.abs(x)))


def _cast_kernel(w_ref, o_ref):
    o_ref[...] = w_ref[...].astype(o_ref.dtype)


def _to_bf16(w, rows=256):
    k, n = w.shape
    return pl.pallas_call(
        _cast_kernel,
        grid=(k // rows,),
        in_specs=[pl.BlockSpec((rows, n), lambda i: (i, 0))],
        out_specs=pl.BlockSpec((rows, n), lambda i: (i, 0)),
        out_shape=jax.ShapeDtypeStruct((k, n), BF16),
        compiler_params=_cparams("arbitrary"),
        name="cast_bf16",
    )(w)


def _in_proj_kernel(x_ref, g_ref, w_ref, wdt_ref, q_ref, k_ref, v_ref, z_ref, xbc_ref, dt_ref):
    hb = _rms(x_ref[...], g_ref[...]).astype(BF16)

    def mm(a, b):
        return jnp.dot(hb, w_ref[:, a:b], preferred_element_type=F32)

    o = 0
    q_ref[...] = (mm(o, o + SB_WIDTH) * (HEAD_DIM ** -0.5)).astype(BF16)
    o += SB_WIDTH
    k_ref[...] = mm(o, o + SB_WIDTH).astype(BF16)
    o += SB_WIDTH
    v_ref[...] = mm(o, o + SB_WIDTH).astype(BF16)
    o += SB_WIDTH
    z_ref[...] = mm(o, o + SSM_INNER)
    o += SSM_INNER
    xbc_ref[...] = mm(o, o + SSM_CONV_DIM)
    dt_ref[...] = jnp.dot(hb, wdt_ref[...].astype(BF16), preferred_element_type=F32)


def _in_proj(x, g, w_bf, w_dt, tm=256):
    t, d = x.shape
    n = w_bf.shape[1]
    row = lambda w: pl.BlockSpec((tm, w), lambda i: (i, 0))
    outs = [(SB_WIDTH, BF16)] * 3 + [(SSM_INNER, F32), (SSM_CONV_DIM, F32), (LANES, F32)]
    return pl.pallas_call(
        _in_proj_kernel,
        grid=(t // tm,),
        in_specs=[row(d),
                  pl.BlockSpec((1, d), lambda i: (0, 0)),
                  pl.BlockSpec((d, n), lambda i: (0, 0), pipeline_mode=pl.Buffered(1)),
                  pl.BlockSpec((d, LANES), lambda i: (0, 0))],
        out_specs=[row(w) for w, _ in outs],
        out_shape=[jax.ShapeDtypeStruct((t, w), dt) for w, dt in outs],
        compiler_params=_cparams("arbitrary"),
        name="in_proj",
    )(x, g, w_bf, w_dt)


def _attn_kernel(q_ref, k_ref, v_ref, o_ref, acc_ref, carry_ref, *, tq):
    i = pl.program_id(1)
    nsub = tq // LANES
    qb0 = i * nsub
    row = lax.broadcasted_iota(jnp.int32, (LANES, LANES), 0)
    col = lax.broadcasted_iota(jnp.int32, (LANES, LANES), 1)
    past = col < row
    lo = col < HEAD_DIM
    jj = lax.broadcasted_iota(jnp.int32, (LANES, 2 * LANES), 0)
    ss = lax.broadcasted_iota(jnp.int32, (LANES, 2 * LANES), 1)
    neg_later = jnp.where((ss >= LANES) | (jj > ss), -1.0, 0.0).astype(BF16)
    nt = (((1,), (1,)), ((), ()))

    def step(n, diag):
        k0s = [pl.multiple_of(jnp.maximum(qb0 + s - n, 0) * LANES, LANES) for s in range(nsub)]
        chains = [(s, hd) for s in range(nsub) for hd in range(2)]
        logits = []
        for s, hd in chains:
            q = q_ref[s * LANES:(s + 1) * LANES, :]
            qh = jnp.where(lo if hd == 0 else jnp.logical_not(lo), q, jnp.zeros(q.shape, q.dtype))
            kt = k_ref[pl.ds(k0s[s], LANES), :]
            logits.append(lax.dot_general(qh, kt, nt, preferred_element_type=F32))
        logsig, sums = [], []
        for l in logits:
            sp = _softplus(l)
            lk = jnp.where(past, sp, 0.0) if diag else sp
            sums.append(jnp.dot(lk.astype(BF16), neg_later, preferred_element_type=F32))
            logsig.append(l - sp)
        top = None
        for c, (s, hd) in enumerate(chains):
            r = sums[c]
            if diag:
                w = jnp.where(past, jnp.exp(logsig[c] + r[:, :LANES]), 0.0)
                carry = r[:, LANES:]
            else:
                before = jnp.where(qb0 + s - n >= 0, carry_ref[c], -1e30)
                w = jnp.exp(logsig[c] + r[:, :LANES] + before)
                carry = before + r[:, LANES:]
            vt = v_ref[pl.ds(k0s[s], LANES), :]
            pv = jnp.dot(w.astype(BF16), vt, preferred_element_type=F32)
            if diag:
                acc_ref[c] = pv
            else:
                acc_ref[c] += pv
            carry_ref[c] = carry
            top = carry if top is None else jnp.maximum(top, carry)
        return jnp.max(top)

    top0 = step(0, True)

    def cond(st):
        n, top = st
        return jnp.logical_and(n <= qb0 + nsub - 1, top > ATTN_DEAD)

    def body(st):
        n, _ = st
        return n + 1, step(n, False)

    lax.while_loop(cond, body, (jnp.int32(1), top0))
    for s in range(nsub):
        o_ref[s * LANES:(s + 1) * LANES, :] = jnp.where(
            lo, acc_ref[2 * s], acc_ref[2 * s + 1]).astype(o_ref.dtype)


def _attention(q, k, v, tq=512):
    t, w = q.shape
    tq = min(tq, t)
    chains = 2 * (tq // LANES)
    return pl.pallas_call(
        functools.partial(_attn_kernel, tq=tq),
        grid=(w // LANES, t // tq),
        in_specs=[pl.BlockSpec((tq, LANES), lambda p, i: (i, p)),
                  pl.BlockSpec((t, LANES), lambda p, i: (0, p)),
                  pl.BlockSpec((t, LANES), lambda p, i: (0, p))],
        out_specs=pl.BlockSpec((tq, LANES), lambda p, i: (i, p)),
        out_shape=jax.ShapeDtypeStruct((t, w), BF16),
        scratch_shapes=[pltpu.VMEM((chains, LANES, LANES), F32),
                        pltpu.VMEM((chains, LANES, LANES), F32)],
        compiler_params=_cparams("arbitrary", "arbitrary"),
        name="sb_attention",
    )(q, k, v)


def _ssd_kernel(z_ref, xbc_ref, dt_ref, cw_ref, cb_ref, dtb_ref, alog_ref, dsk_ref, g_ref, e_ref,
                o_ref, state_ref, xcat_ref):
    c = pl.program_id(0)
    L = z_ref.shape[0]
    half = SSM_INNER // SSM_GROUPS
    pad = 8

    @pl.when(c == 0)
    def _():
        state_ref[...] = jnp.zeros(state_ref.shape, F32)
        xcat_ref[0:pad, :] = jnp.zeros((pad, SSM_CONV_DIM), F32)

    xcat_ref[pad:pad + L, :] = xbc_ref[...]
    acc = jnp.broadcast_to(cb_ref[...], (L, SSM_CONV_DIM))
    for kk in range(SSM_CONV):
        s = pad - (SSM_CONV - 1) + kk
        acc = acc + cw_ref[kk:kk + 1, :] * xcat_ref[s:s + L, :]
    xcat_ref[0:pad, :] = xcat_ref[L:L + pad, :]
    xc = acc * _sigmoid(acc)
    xs = xc[:, :SSM_INNER]
    bm = xc[:, SSM_INNER:SSM_INNER + SSM_GROUPS * SSM_STATE]
    cm = xc[:, SSM_INNER + SSM_GROUPS * SSM_STATE:]

    dtv = _softplus(dt_ref[...] + dtb_ref[...])
    adt = dtv * (-jnp.exp(alog_ref[...]))
    ri = lax.broadcasted_iota(jnp.int32, (L, L), 0)
    ci = lax.broadcasted_iota(jnp.int32, (L, L), 1)
    causal = ri >= ci
    tri = jnp.where(causal, 1.0, 0.0).astype(F32)
    acum = jnp.dot(tri, adt, precision=HIGHEST, preferred_element_type=F32)
    expand = e_ref[...]
    acx = jnp.dot(acum, expand, precision=HIGHEST, preferred_element_type=F32)
    dtx = jnp.dot(dtv, expand, precision=HIGHEST, preferred_element_type=F32)
    alx = acx[L - 1:L, :]
    xdt = xs * dtx
    acum_t = acum.T
    lo = lax.broadcasted_iota(jnp.int32, (L, LANES), 1) < HEAD_DIM
    nt = (((1,), (1,)), ((), ()))

    y_parts = []
    for g in range(SSM_GROUPS):
        cg = cm[:, g * SSM_STATE:(g + 1) * SSM_STATE]
        bg = bm[:, g * SSM_STATE:(g + 1) * SSM_STATE]
        y_off = jnp.dot(cg, state_ref[:, g * half:(g + 1) * half], precision=HIGHEST,
                        preferred_element_type=F32)
        cb = lax.dot_general(cg, bg, nt, precision=HIGHEST, preferred_element_type=F32)
        for p in range(half // LANES):
            pair = g * (half // LANES) + p
            xp = xdt[:, pair * LANES:(pair + 1) * LANES]
            yh = []
            for hh in range(2):
                h = 2 * pair + hh
                seg = acum[:, h:h + 1] - acum_t[h:h + 1, :]
                dec = jnp.where(causal, jnp.exp(jnp.minimum(seg, 0.0)), 0.0)
                yh.append(jnp.dot(cb * dec, xp, precision=HIGHEST, preferred_element_type=F32))
            y_parts.append(jnp.where(lo, yh[0], yh[1])
                           + y_off[:, p * LANES:(p + 1) * LANES]
                           * jnp.exp(acx[:, pair * LANES:(pair + 1) * LANES]))
        xd = xdt[:, g * half:(g + 1) * half] * jnp.exp(alx[:, g * half:(g + 1) * half]
                                                       - acx[:, g * half:(g + 1) * half])
        upd = jnp.dot(bg.T, xd, precision=HIGHEST, preferred_element_type=F32)
        state_ref[:, g * half:(g + 1) * half] = (
            jnp.exp(alx[:, g * half:(g + 1) * half]) * state_ref[:, g * half:(g + 1) * half] + upd)

    y = jnp.concatenate(y_parts, axis=1) + dsk_ref[...] * xs
    zz = z_ref[...]
    y = y * (zz * _sigmoid(zz))
    outs = []
    for g in range(SSM_GROUPS):
        yg = y[:, g * half:(g + 1) * half]
        outs.append(yg * lax.rsqrt(jnp.mean(yg * yg, axis=-1, keepdims=True) + RMS_EPS))
    o_ref[...] = (jnp.concatenate(outs, axis=1) * g_ref[...]).astype(o_ref.dtype)


def _ssd(z, xbc, dt, conv_w, conv_b, dt_bias, a_log, d_skip, g_ssm, chunk=128):
    t = z.shape[0]
    chunk = min(chunk, t)
    padh = lambda a: jnp.pad(a.reshape(1, -1), ((0, 0), (0, LANES - a.shape[-1])))
    heads = jnp.arange(SSM_INNER, dtype=jnp.int32) // HEAD_DIM
    expand = (jnp.arange(LANES, dtype=jnp.int32)[:, None] == heads[None, :]).astype(F32)
    row = lambda w: pl.BlockSpec((chunk, w), lambda c: (c, 0))
    full = lambda a: pl.BlockSpec(a.shape, lambda c: (0, 0))
    params = [conv_w, conv_b.reshape(1, -1), padh(dt_bias), padh(a_log),
              jnp.repeat(d_skip, HEAD_DIM).reshape(1, -1), g_ssm.reshape(1, -1), expand]
    return pl.pallas_call(
        _ssd_kernel,
        grid=(t // chunk,),
        in_specs=[row(SSM_INNER), row(SSM_CONV_DIM), row(LANES)] + [full(a) for a in params],
        out_specs=row(SSM_INNER),
        out_shape=jax.ShapeDtypeStruct((t, SSM_INNER), BF16),
        scratch_shapes=[pltpu.VMEM((SSM_STATE, SSM_INNER), F32),
                        pltpu.VMEM((chunk + 8, SSM_CONV_DIM), F32)],
        compiler_params=_cparams("arbitrary"),
        name="ssd_mixer",
    )(z, xbc, dt, *params)


def _out_proj_kernel(x_ref, a_ref, s_ref, w_ref, o_ref):
    o_ref[...] = (x_ref[...]
                  + jnp.dot(a_ref[...], w_ref[0:SB_WIDTH, :], preferred_element_type=F32)
                  + jnp.dot(s_ref[...], w_ref[SB_WIDTH:, :], preferred_element_type=F32))


def _out_proj(x, attn, ssm, w_bf, tm=512):
    t, d = x.shape
    tm = min(tm, t)
    return pl.pallas_call(
        _out_proj_kernel,
        grid=(t // tm,),
        in_specs=[pl.BlockSpec((tm, d), lambda i: (i, 0)),
                  pl.BlockSpec((tm, SB_WIDTH), lambda i: (i, 0)),
                  pl.BlockSpec((tm, SSM_INNER), lambda i: (i, 0)),
                  pl.BlockSpec(w_bf.shape, lambda i: (0, 0), pipeline_mode=pl.Buffered(1))],
        out_specs=pl.BlockSpec((tm, d), lambda i: (i, 0)),
        out_shape=jax.ShapeDtypeStruct((t, d), F32),
        compiler_params=_cparams("arbitrary"),
        name="out_proj",
    )(x, attn, ssm, w_bf)


def _router_kernel(x_ref, g_ref, wt_ref, b_ref, idx_ref, gate_ref, pos_ref, cnt_ref, base_ref):
    i = pl.program_id(0)
    tm = x_ref.shape[0]

    @pl.when(i == 0)
    def _():
        base_ref[...] = jnp.zeros(base_ref.shape, F32)

    h = _rms(x_ref[...], g_ref[...])
    logits = lax.dot_general(wt_ref[...], h, (((1,), (1,)), ((), ())), precision=HIGHEST,
                             preferred_element_type=F32) + b_ref[:, 0:1]
    eio = lax.broadcasted_iota(jnp.int32, (N_EXPERTS, tm), 0).astype(F32)
    work = logits
    vals, hots = [], []
    for k in range(TOP_K):
        m = jnp.max(work, axis=0, keepdims=True)
        idx = jnp.min(jnp.where(work == m, eio, float(N_EXPERTS)), axis=0, keepdims=True)
        hot = eio == idx
        work = jnp.where(hot, -jnp.inf, work)
        vals.append(m)
        hots.append(hot)
        idx_ref[k:k + 1, :] = idx.astype(jnp.int32)
    ex = [jnp.exp(v - vals[0]) for v in vals]
    den = ex[0] + ex[1] + ex[2] + ex[3]
    picked = jnp.zeros((N_EXPERTS, tm), F32)
    for k in range(TOP_K):
        gate_ref[k:k + 1, :] = ex[k] / den
        picked = picked + jnp.where(hots[k], 1.0, 0.0)
    ti = lax.broadcasted_iota(jnp.int32, (tm, tm), 0)
    tj = lax.broadcasted_iota(jnp.int32, (tm, tm), 1)
    before = jnp.where(ti < tj, 1.0, 0.0).astype(BF16)
    rank = jnp.dot(picked.astype(BF16), before, preferred_element_type=F32) + base_ref[:, 0:1]
    for k in range(TOP_K):
        pos = jnp.sum(jnp.where(hots[k], rank, 0.0), axis=0, keepdims=True)
        pos_ref[k:k + 1, :] = pos.astype(jnp.int32)
    base_ref[...] = base_ref[...] + jnp.sum(picked, axis=1, keepdims=True)
    cnt_ref[...] = base_ref[...].astype(jnp.int32)


def _router(x1, g, w_router, b_router, tm=512):
    t, d = x1.shape
    tm = min(tm, t)
    wt = w_router.T
    b = jnp.broadcast_to(b_router.reshape(-1, 1), (N_EXPERTS, LANES))
    tok = lambda dt: (pl.BlockSpec((TOP_K, tm), lambda i: (0, i)), jax.ShapeDtypeStruct((TOP_K, t), dt))
    specs = [tok(jnp.int32), tok(F32), tok(jnp.int32),
             (pl.BlockSpec((N_EXPERTS, LANES), lambda i: (0, 0)),
              jax.ShapeDtypeStruct((N_EXPERTS, LANES), jnp.int32))]
    return pl.pallas_call(
        _router_kernel,
        grid=(t // tm,),
        in_specs=[pl.BlockSpec((tm, d), lambda i: (i, 0)),
                  pl.BlockSpec((1, d), lambda i: (0, 0)),
                  pl.BlockSpec((N_EXPERTS, d), lambda i: (0, 0)),
                  pl.BlockSpec((N_EXPERTS, LANES), lambda i: (0, 0))],
        out_specs=[s for s, _ in specs],
        out_shape=[s for _, s in specs],
        scratch_shapes=[pltpu.VMEM((N_EXPERTS, LANES), F32)],
        compiler_params=_cparams("arbitrary"),
        name="router",
    )(x1, g, wt, b)


def _row_copy(src_hbm, row, dst, r, sem):
    return pltpu.make_async_copy(src_hbm.at[pl.ds(row, 1), :], dst.at[pl.ds(r, 1), :], sem)


def _gather_kernel(tok_ref, x_hbm, g_ref, o_ref, buf_ref, sem):
    tm = o_ref.shape[0]
    s0 = pl.program_id(0) * tm

    def start(r, c):
        _row_copy(x_hbm, tok_ref[s0 + r], buf_ref, r, sem.at[0]).start()
        return c

    def wait(r, c):
        _row_copy(x_hbm, tok_ref[s0 + r], buf_ref, r, sem.at[0]).wait()
        return c

    lax.fori_loop(0, tm, start, 0)
    lax.fori_loop(0, tm, wait, 0)
    o_ref[...] = _rms(buf_ref[...], g_ref[...]).astype(o_ref.dtype)


def _gather_norm(x1, g, slot_tok, tm):
    t, d = x1.shape
    n_slots = slot_tok.shape[0]
    return pl.pallas_call(
        _gather_kernel,
        grid_spec=pltpu.PrefetchScalarGridSpec(
            num_scalar_prefetch=1,
            grid=(n_slots // tm,),
            in_specs=[pl.BlockSpec(memory_space=pl.ANY),
                      pl.BlockSpec((1, d), lambda i, tok: (0, 0))],
            out_specs=pl.BlockSpec((tm, d), lambda i, tok: (i, 0)),
            scratch_shapes=[pltpu.VMEM((tm, d), F32), pltpu.SemaphoreType.DMA((1,))]),
        out_shape=jax.ShapeDtypeStruct((n_slots, d), BF16),
        compiler_params=_cparams("arbitrary"),
        name="moe_gather",
    )(slot_tok, x1, g)


def _gmm_kernel(be_ref, nu_ref, x_ref, w_ref, b_ref, *rest, swiglu):
    if swiglu:
        pick_ref, o_ref, wbf_ref = rest
    else:
        o_ref, wbf_ref = rest
    m = pl.program_id(1)
    used = m < nu_ref[0]
    me = jnp.minimum(m, nu_ref[0] - 1)
    new_expert = jnp.logical_or(m == 0, be_ref[me] != be_ref[jnp.maximum(me - 1, 0)])

    @pl.when(jnp.logical_and(used, new_expert))
    def _():
        wbf_ref[...] = w_ref[...].astype(BF16)

    @pl.when(jnp.logical_not(used))
    def _():
        o_ref[...] = jnp.zeros(o_ref.shape, o_ref.dtype)

    @pl.when(used)
    def _():
        y = jnp.dot(x_ref[...], wbf_ref[...], preferred_element_type=F32) + b_ref[...]
        if not swiglu:
            o_ref[...] = y.astype(o_ref.dtype)
            return
        tm, n2 = y.shape
        even = lax.broadcasted_iota(jnp.int32, (tm, LANES), 1) % 2 == 0
        for cblk in range(n2 // (2 * LANES)):
            prods = []
            for s in range(2):
                c0 = cblk * 2 * LANES + s * LANES
                yc = y[:, c0:c0 + LANES]
                gate = jnp.minimum(yc, SWIGLU_LIMIT)
                gate = gate * _sigmoid(SWIGLU_ALPHA * gate)
                up = jnp.clip(yc, -SWIGLU_LIMIT, SWIGLU_LIMIT) + 1.0
                a = jnp.where(even, gate, up)
                prods.append((a * pltpu.roll(a, LANES - 1, 1)).astype(BF16))
            pr = jnp.concatenate(prods, axis=1)
            o_ref[:, cblk * LANES:(cblk + 1) * LANES] = jnp.dot(
                pr, pick_ref[...], preferred_element_type=F32).astype(o_ref.dtype)


def _grouped_matmul(x, w, b, block_e, n_used, *, tm, tn, swiglu, out_dtype):
    n_slots, kdim = x.shape
    n_exp, _, n = w.shape
    nb = n_slots // tm
    n_out = n // 2 if swiglu else n
    tn_out = tn // 2 if swiglu else tn
    blk = lambda m, nu: jnp.minimum(m, nu[0] - 1)
    in_specs = [pl.BlockSpec((tm, kdim), lambda j, m, be, nu: (blk(m, nu), 0)),
                pl.BlockSpec((None, kdim, tn), lambda j, m, be, nu: (be[blk(m, nu)], 0, j)),
                pl.BlockSpec((None, 1, tn), lambda j, m, be, nu: (be[blk(m, nu)], 0, j))]
    args = [x, w, b.reshape(n_exp, 1, n)]
    if swiglu:
        src = jnp.arange(2 * LANES, dtype=jnp.int32)[:, None]
        dst = jnp.arange(LANES, dtype=jnp.int32)[None, :]
        args.append((src == 2 * dst).astype(BF16))
        in_specs.append(pl.BlockSpec((2 * LANES, LANES), lambda j, m, be, nu: (0, 0)))
    return pl.pallas_call(
        functools.partial(_gmm_kernel, swiglu=swiglu),
        grid_spec=pltpu.PrefetchScalarGridSpec(
            num_scalar_prefetch=2,
            grid=(n // tn, nb),
            in_specs=in_specs,
            out_specs=pl.BlockSpec((tm, tn_out), lambda j, m, be, nu: (m, j)),
            scratch_shapes=[pltpu.VMEM((kdim, tn), BF16)]),
        out_shape=jax.ShapeDtypeStruct((n_slots, n_out), out_dtype),
        compiler_params=_cparams("arbitrary", "arbitrary"),
        name="moe_gate_up" if swiglu else "moe_down",
    )(block_e, n_used, *args)


def _combine_kernel(dest_ref, x_ref, gate_ref, g_ref, ys_hbm, o_ref, buf_ref, sem, *, norm):
    tc = x_ref.shape[0]
    t = dest_ref.shape[0] // TOP_K
    t0 = pl.program_id(0) * tc

    def start(r, c):
        for k in range(TOP_K):
            _row_copy(ys_hbm, dest_ref[k * t + t0 + r], buf_ref.at[k], r, sem.at[0]).start()
        return c

    def wait(r, c):
        for k in range(TOP_K):
            _row_copy(ys_hbm, dest_ref[k * t + t0 + r], buf_ref.at[k], r, sem.at[0]).wait()
        return c

    lax.fori_loop(0, tc, start, 0)
    lax.fori_loop(0, tc, wait, 0)
    y = x_ref[...]
    for k in range(TOP_K):
        y = y + gate_ref[:, k:k + 1] * buf_ref[k]
    o_ref[...] = _rms(y, g_ref[...]) if norm else y


def _combine(x1, gates_t, dest_flat, ys, g_final, norm, tc=128):
    t, d = x1.shape
    tc = min(tc, t)
    return pl.pallas_call(
        functools.partial(_combine_kernel, norm=norm),
        grid_spec=pltpu.PrefetchScalarGridSpec(
            num_scalar_prefetch=1,
            grid=(t // tc,),
            in_specs=[pl.BlockSpec((tc, d), lambda i, dest: (i, 0)),
                      pl.BlockSpec((tc, TOP_K), lambda i, dest: (i, 0)),
                      pl.BlockSpec((1, d), lambda i, dest: (0, 0)),
                      pl.BlockSpec(memory_space=pl.ANY)],
            out_specs=pl.BlockSpec((tc, d), lambda i, dest: (i, 0)),
            scratch_shapes=[pltpu.VMEM((TOP_K, tc, d), F32), pltpu.SemaphoreType.DMA((1,))]),
        out_shape=jax.ShapeDtypeStruct((t, d), F32),
        compiler_params=_cparams("arbitrary"),
        name="moe_combine",
    )(dest_flat, x1, gates_t, g_final, ys)


def _mixer(x, g_mix, w_in, conv_w, conv_b, dt_bias, a_log, d_skip, g_ssm, w_out):
    n_main = 3 * SB_WIDTH + SSM_INNER + SSM_CONV_DIM
    w_dt = jnp.pad(w_in[:, n_main:], ((0, 0), (0, LANES - SSM_HEADS)))
    q, k, v, z, xbc, dt = _in_proj(x, g_mix.reshape(1, -1), _to_bf16(w_in), w_dt)
    attn = _attention(q, k, v)
    ssm = _ssd(z, xbc, dt, conv_w, conv_b, dt_bias, a_log, d_skip, g_ssm)
    return _out_proj(x, attn, ssm, _to_bf16(w_out))


def _moe(x1, g_ffn, w_router, b_router, w_gate_up, b_gate_up, w_down, b_down, g_out, norm, tm=256):
    t, d = x1.shape
    g_ffn = g_ffn.reshape(1, -1)
    idx, gates, pos, cnt = _router(x1, g_ffn, w_router, b_router)
    counts = cnt[:, 0]
    padded = (counts + tm - 1) // tm * tm
    pend = jnp.cumsum(padded)
    pstart = pend - padded
    experts = jnp.arange(N_EXPERTS, dtype=jnp.int32)
    dest = pos + jnp.sum(jnp.where(idx[None] == experts[:, None, None],
                                   pstart[:, None, None], 0), axis=0)
    n_blocks = (t * TOP_K) // tm + N_EXPERTS
    n_slots = n_blocks * tm
    n_used = (pend[-1] // tm).astype(jnp.int32).reshape(1)
    block_row = jnp.arange(n_blocks, dtype=jnp.int32) * tm
    block_e = jnp.minimum(jnp.sum((pend[None, :] <= block_row[:, None]).astype(jnp.int32), axis=1),
                          N_EXPERTS - 1)
    tok = jnp.broadcast_to(jnp.arange(t, dtype=jnp.int32)[None, :], (TOP_K, t))
    slot_tok = jnp.zeros((n_slots,), jnp.int32).at[dest.reshape(-1)].set(tok.reshape(-1))

    xs = _gather_norm(x1, g_ffn, slot_tok, tm)
    act = _grouped_matmul(xs, w_gate_up, b_gate_up, block_e, n_used,
                          tm=tm, tn=1024, swiglu=True, out_dtype=BF16)
    ys = _grouped_matmul(act, w_down, b_down, block_e, n_used,
                         tm=tm, tn=1024, swiglu=False, out_dtype=F32)
    return _combine(x1, gates.T, dest.reshape(-1), ys, g_out.reshape(1, -1), norm)


def kernel(x, g_mix, w_in, conv_w, conv_b, dt_bias, a_log, d_skip, g_ssm, w_out, g_ffn, w_router,
           b_router, w_gate_up, b_gate_up, w_down, b_down, g_final):
    b, s, d = x.shape
    depth = g_mix.shape[0]
    outs = []
    for bi in range(b):
        xb = x[bi]
        for l in range(depth):
            x1 = _mixer(xb, g_mix[l], w_in[l], conv_w[l], conv_b[l], dt_bias[l], a_log[l],
                        d_skip[l], g_ssm[l], w_out[l])
            xb = _moe(x1, g_ffn[l], w_router[l], b_router[l], w_gate_up[l], b_gate_up[l],
                      w_down[l], b_down[l], g_final, norm=(l == depth - 1))
        outs.append(xb)
    return outs[0][None] if b == 1 else jnp.stack(outs)
```

```python
import functools

import jax
import jax.numpy as jnp
from jax import lax
from jax.experimental import pallas as pl
from jax.experimental.pallas import tpu as pltpu

F32 = jnp.float32
BF16 = jnp.bfloat16
HIGHEST = lax.Precision.HIGHEST

LANES = 128
HEAD_DIM = 64
SB_WIDTH = 1024
SSM_INNER = 1024
SSM_GROUPS = 2
SSM_STATE = 128
SSM_CONV = 4
SSM_CONV_DIM = SSM_INNER + 2 * SSM_GROUPS * SSM_STATE
SSM_HEADS = SSM_INNER // HEAD_DIM
N_EXPERTS = 32
TOP_K = 4
SWIGLU_LIMIT = 7.0
SWIGLU_ALPHA = 1.702
RMS_EPS = 1e-5

VMEM_LIMIT = 56 * 1024 * 1024

ATTN_DEAD = -110.0


def _cparams(*sem):
    return pltpu.CompilerParams(dimension_semantics=sem, vmem_limit_bytes=VMEM_LIMIT)


def _rms(x, g):
    return x * lax.rsqrt(jnp.mean(x * x, axis=-1, keepdims=True) + RMS_EPS) * g


def _sigmoid(x):
    return 1.0 / (1.0 + jnp.exp(-x))


def _softplus(x):
    return jnp.maximum(x, 0.0) + jnp.log(1.0 + jnp.exp(-jnp.abs(x)))


def _cast_kernel(w_ref, o_ref):
    o_ref[...] = w_ref[...].astype(o_ref.dtype)


def _to_bf16(w, rows=256):
    k, n = w.shape
    return pl.pallas_call(
        _cast_kernel,
        grid=(k // rows,),
        in_specs=[pl.BlockSpec((rows, n), lambda i: (i, 0))],
        out_specs=pl.BlockSpec((rows, n), lambda i: (i, 0)),
        out_shape=jax.ShapeDtypeStruct((k, n), BF16),
        compiler_params=_cparams("arbitrary"),
        name="cast_bf16",
    )(w)


def _in_proj_kernel(x_ref, g_ref, w_ref, wdt_ref, q_ref, k_ref, v_ref, z_ref, xbc_ref, dt_ref):
    hb = _rms(x_ref[...], g_ref[...]).astype(BF16)

    def mm(a, b):
        return jnp.dot(hb, w_ref[:, a:b], preferred_element_type=F32)

    o = 0
    q_ref[...] = (mm(o, o + SB_WIDTH) * (HEAD_DIM ** -0.5)).astype(BF16)
    o += SB_WIDTH
    k_ref[...] = mm(o, o + SB_WIDTH).astype(BF16)
    o += SB_WIDTH
    v_ref[...] = mm(o, o + SB_WIDTH).astype(BF16)
    o += SB_WIDTH
    z_ref[...] = mm(o, o + SSM_INNER)
    o += SSM_INNER
    xbc_ref[...] = mm(o, o + SSM_CONV_DIM)
    dt_ref[...] = jnp.dot(hb, wdt_ref[...].astype(BF16), preferred_element_type=F32)


def _in_proj(x, g, w_bf, w_dt, tm=256):
    t, d = x.shape
    n = w_bf.shape[1]
    row = lambda w: pl.BlockSpec((tm, w), lambda i: (i, 0))
    outs = [(SB_WIDTH, BF16)] * 3 + [(SSM_INNER, F32), (SSM_CONV_DIM, F32), (LANES, F32)]
    return pl.pallas_call(
        _in_proj_kernel,
        grid=(t // tm,),
        in_specs=[row(d),
                  pl.BlockSpec((1, d), lambda i: (0, 0)),
                  pl.BlockSpec((d, n), lambda i: (0, 0), pipeline_mode=pl.Buffered(1)),
                  pl.BlockSpec((d, LANES), lambda i: (0, 0))],
        out_specs=[row(w) for w, _ in outs],
        out_shape=[jax.ShapeDtypeStruct((t, w), dt) for w, dt in outs],
        compiler_params=_cparams("arbitrary"),
        name="in_proj",
    )(x, g, w_bf, w_dt)


def _attn_kernel(q_ref, k_ref, v_ref, o_ref, acc_ref, carry_ref, *, tq):
    i = pl.program_id(1)
    nsub = tq // LANES
    qb0 = i * nsub
    row = lax.broadcasted_iota(jnp.int32, (LANES, LANES), 0)
    col = lax.broadcasted_iota(jnp.int32, (LANES, LANES), 1)
    past = col < row
    lo = col < HEAD_DIM
    jj = lax.broadcasted_iota(jnp.int32, (LANES, 2 * LANES), 0)
    ss = lax.broadcasted_iota(jnp.int32, (LANES, 2 * LANES), 1)
    neg_later = jnp.where((ss >= LANES) | (jj > ss), -1.0, 0.0).astype(BF16)
    nt = (((1,), (1,)), ((), ()))

    def step(n, diag):
        k0s = [pl.multiple_of(jnp.maximum(qb0 + s - n, 0) * LANES, LANES) for s in range(nsub)]
        chains = [(s, hd) for s in range(nsub) for hd in range(2)]
        logits = []
        for s, hd in chains:
            q = q_ref[s * LANES:(s + 1) * LANES, :]
            qh = jnp.where(lo if hd == 0 else jnp.logical_not(lo), q, jnp.zeros(q.shape, q.dtype))
            kt = k_ref[pl.ds(k0s[s], LANES), :]
            logits.append(lax.dot_general(qh, kt, nt, preferred_element_type=F32))
        logsig, sums = [], []
        for l in logits:
            sp = _softplus(l)
            lk = jnp.where(past, sp, 0.0) if diag else sp
            sums.append(jnp.dot(lk.astype(BF16), neg_later, preferred_element_type=F32))
            logsig.append(l - sp)
        top = None
        for c, (s, hd) in enumerate(chains):
            r = sums[c]
            if diag:
                w = jnp.where(past, jnp.exp(logsig[c] + r[:, :LANES]), 0.0)
                carry = r[:, LANES:]
            else:
                before = jnp.where(qb0 + s - n >= 0, carry_ref[c], -1e30)
                w = jnp.exp(logsig[c] + r[:, :LANES] + before)
                carry = before + r[:, LANES:]
            vt = v_ref[pl.ds(k0s[s], LANES), :]
            pv = jnp.dot(w.astype(BF16), vt, preferred_element_type=F32)
            if diag:
                acc_ref[c] = pv
            else:
                acc_ref[c] += pv
            carry_ref[c] = carry
            top = carry if top is None else jnp.maximum(top, carry)
        return jnp.max(top)

    top0 = step(0, True)

    def cond(st):
        n, top = st
        return jnp.logical_and(n <= qb0 + nsub - 1, top > ATTN_DEAD)

    def body(st):
        n, _ = st
        return n + 1, step(n, False)

    lax.while_loop(cond, body, (jnp.int32(1), top0))
    for s in range(nsub):
        o_ref[s * LANES:(s + 1) * LANES, :] = jnp.where(
            lo, acc_ref[2 * s], acc_ref[2 * s + 1]).astype(o_ref.dtype)


def _attention(q, k, v, tq=512):
    t, w = q.shape
    tq = min(tq, t)
    chains = 2 * (tq // LANES)
    return pl.pallas_call(
        functools.partial(_attn_kernel, tq=tq),
        grid=(w // LANES, t // tq),
        in_specs=[pl.BlockSpec((tq, LANES), lambda p, i: (i, p)),
                  pl.BlockSpec((t, LANES), lambda p, i: (0, p)),
                  pl.BlockSpec((t, LANES), lambda p, i: (0, p))],
        out_specs=pl.BlockSpec((tq, LANES), lambda p, i: (i, p)),
        out_shape=jax.ShapeDtypeStruct((t, w), BF16),
        scratch_shapes=[pltpu.VMEM((chains, LANES, LANES), F32),
                        pltpu.VMEM((chains, LANES, LANES), F32)],
        compiler_params=_cparams("arbitrary", "arbitrary"),
        name="sb_attention",
    )(q, k, v)


def _ssd_kernel(z_ref, xbc_ref, dt_ref, cw_ref, cb_ref, dtb_ref, alog_ref, dsk_ref, g_ref, e_ref,
                o_ref, state_ref, xcat_ref):
    c = pl.program_id(0)
    L = z_ref.shape[0]
    half = SSM_INNER // SSM_GROUPS
    pad = 8

    @pl.when(c == 0)
    def _():
        state_ref[...] = jnp.zeros(state_ref.shape, F32)
        xcat_ref[0:pad, :] = jnp.zeros((pad, SSM_CONV_DIM), F32)

    xcat_ref[pad:pad + L, :] = xbc_ref[...]
    acc = jnp.broadcast_to(cb_ref[...], (L, SSM_CONV_DIM))
    for kk in range(SSM_CONV):
        s = pad - (SSM_CONV - 1) + kk
        acc = acc + cw_ref[kk:kk + 1, :] * xcat_ref[s:s + L, :]
    xcat_ref[0:pad, :] = xcat_ref[L:L + pad, :]
    xc = acc * _sigmoid(acc)
    xs = xc[:, :SSM_INNER]
    bm = xc[:, SSM_INNER:SSM_INNER + SSM_GROUPS * SSM_STATE]
    cm = xc[:, SSM_INNER + SSM_GROUPS * SSM_STATE:]

    dtv = _softplus(dt_ref[...] + dtb_ref[...])
    adt = dtv * (-jnp.exp(alog_ref[...]))
    ri = lax.broadcasted_iota(jnp.int32, (L, L), 0)
    ci = lax.broadcasted_iota(jnp.int32, (L, L), 1)
    causal = ri >= ci
    tri = jnp.where(causal, 1.0, 0.0).astype(F32)
    acum = jnp.dot(tri, adt, precision=HIGHEST, preferred_element_type=F32)
    expand = e_ref[...]
    acx = jnp.dot(acum, expand, precision=HIGHEST, preferred_element_type=F32)
    dtx = jnp.dot(dtv, expand, precision=HIGHEST, preferred_element_type=F32)
    alx = acx[L - 1:L, :]
    xdt = xs * dtx
    acum_t = acum.T
    lo = lax.broadcasted_iota(jnp.int32, (L, LANES), 1) < HEAD_DIM
    nt = (((1,), (1,)), ((), ()))

    y_parts = []
    for g in range(SSM_GROUPS):
        cg = cm[:, g * SSM_STATE:(g + 1) * SSM_STATE]
        bg = bm[:, g * SSM_STATE:(g + 1) * SSM_STATE]
        y_off = jnp.dot(cg, state_ref[:, g * half:(g + 1) * half], precision=HIGHEST,
                        preferred_element_type=F32)
        cb = lax.dot_general(cg, bg, nt, precision=HIGHEST, preferred_element_type=F32)
        for p in range(half // LANES):
            pair = g * (half // LANES) + p
            xp = xdt[:, pair * LANES:(pair + 1) * LANES]
            yh = []
            for hh in range(2):
                h = 2 * pair + hh
                seg = acum[:, h:h + 1] - acum_t[h:h + 1, :]
                dec = jnp.where(causal, jnp.exp(jnp.minimum(seg, 0.0)), 0.0)
                yh.append(jnp.dot(cb * dec, xp, precision=HIGHEST, preferred_element_type=F32))
            y_parts.append(jnp.where(lo, yh[0], yh[1])
                           + y_off[:, p * LANES:(p + 1) * LANES]
                           * jnp.exp(acx[:, pair * LANES:(pair + 1) * LANES]))
        xd = xdt[:, g * half:(g + 1) * half] * jnp.exp(alx[:, g * half:(g + 1) * half]
                                                       - acx[:, g * half:(g + 1) * half])
        upd = jnp.dot(bg.T, xd, precision=HIGHEST, preferred_element_type=F32)
        state_ref[:, g * half:(g + 1) * half] = (
            jnp.exp(alx[:, g * half:(g + 1) * half]) * state_ref[:, g * half:(g + 1) * half] + upd)

    y = jnp.concatenate(y_parts, axis=1) + dsk_ref[...] * xs
    zz = z_ref[...]
    y = y * (zz * _sigmoid(zz))
    outs = []
    for g in range(SSM_GROUPS):
        yg = y[:, g * half:(g + 1) * half]
        outs.append(yg * lax.rsqrt(jnp.mean(yg * yg, axis=-1, keepdims=True) + RMS_EPS))
    o_ref[...] = (jnp.concatenate(outs, axis=1) * g_ref[...]).astype(o_ref.dtype)


def _ssd(z, xbc, dt, conv_w, conv_b, dt_bias, a_log, d_skip, g_ssm, chunk=128):
    t = z.shape[0]
    chunk = min(chunk, t)
    padh = lambda a: jnp.pad(a.reshape(1, -1), ((0, 0), (0, LANES - a.shape[-1])))
    heads = jnp.arange(SSM_INNER, dtype=jnp.int32) // HEAD_DIM
    expand = (jnp.arange(LANES, dtype=jnp.int32)[:, None] == heads[None, :]).astype(F32)
    row = lambda w: pl.BlockSpec((chunk, w), lambda c: (c, 0))
    full = lambda a: pl.BlockSpec(a.shape, lambda c: (0, 0))
    params = [conv_w, conv_b.reshape(1, -1), padh(dt_bias), padh(a_log),
              jnp.repeat(d_skip, HEAD_DIM).reshape(1, -1), g_ssm.reshape(1, -1), expand]
    return pl.pallas_call(
        _ssd_kernel,
        grid=(t // chunk,),
        in_specs=[row(SSM_INNER), row(SSM_CONV_DIM), row(LANES)] + [full(a) for a in params],
        out_specs=row(SSM_INNER),
        out_shape=jax.ShapeDtypeStruct((t, SSM_INNER), BF16),
        scratch_shapes=[pltpu.VMEM((SSM_STATE, SSM_INNER), F32),
                        pltpu.VMEM((chunk + 8, SSM_CONV_DIM), F32)],
        compiler_params=_cparams("arbitrary"),
        name="ssd_mixer",
    )(z, xbc, dt, *params)


def _out_proj_kernel(x_ref, a_ref, s_ref, w_ref, o_ref):
    o_ref[...] = (x_ref[...]
                  + jnp.dot(a_ref[...], w_ref[0:SB_WIDTH, :], preferred_element_type=F32)
                  + jnp.dot(s_ref[...], w_ref[SB_WIDTH:, :], preferred_element_type=F32))


def _out_proj(x, attn, ssm, w_bf, tm=512):
    t, d = x.shape
    tm = min(tm, t)
    return pl.pallas_call(
        _out_proj_kernel,
        grid=(t // tm,),
        in_specs=[pl.BlockSpec((tm, d), lambda i: (i, 0)),
                  pl.BlockSpec((tm, SB_WIDTH), lambda i: (i, 0)),
                  pl.BlockSpec((tm, SSM_INNER), lambda i: (i, 0)),
                  pl.BlockSpec(w_bf.shape, lambda i: (0, 0), pipeline_mode=pl.Buffered(1))],
        out_specs=pl.BlockSpec((tm, d), lambda i: (i, 0)),
        out_shape=jax.ShapeDtypeStruct((t, d), F32),
        compiler_params=_cparams("arbitrary"),
        name="out_proj",
    )(x, attn, ssm, w_bf)


def _router_kernel(x_ref, g_ref, wt_ref, b_ref, idx_ref, gate_ref, pos_ref, cnt_ref, base_ref):
    i = pl.program_id(0)
    tm = x_ref.shape[0]

    @pl.when(i == 0)
    def _():
        base_ref[...] = jnp.zeros(base_ref.shape, F32)

    h = _rms(x_ref[...], g_ref[...])
    logits = lax.dot_general(wt_ref[...], h, (((1,), (1,)), ((), ())), precision=HIGHEST,
                             preferred_element_type=F32) + b_ref[:, 0:1]
    eio = lax.broadcasted_iota(jnp.int32, (N_EXPERTS, tm), 0).astype(F32)
    work = logits
    vals, hots = [], []
    for k in range(TOP_K):
        m = jnp.max(work, axis=0, keepdims=True)
        idx = jnp.min(jnp.where(work == m, eio, float(N_EXPERTS)), axis=0, keepdims=True)
        hot = eio == idx
        work = jnp.where(hot, -jnp.inf, work)
        vals.append(m)
        hots.append(hot)
        idx_ref[k:k + 1, :] = idx.astype(jnp.int32)
    ex = [jnp.exp(v - vals[0]) for v in vals]
    den = ex[0] + ex[1] + ex[2] + ex[3]
    picked = jnp.zeros((N_EXPERTS, tm), F32)
    for k in range(TOP_K):
        gate_ref[k:k + 1, :] = ex[k] / den
        picked = picked + jnp.where(hots[k], 1.0, 0.0)
    ti = lax.broadcasted_iota(jnp.int32, (tm, tm), 0)
    tj = lax.broadcasted_iota(jnp.int32, (tm, tm), 1)
    before = jnp.where(ti < tj, 1.0, 0.0).astype(BF16)
    rank = jnp.dot(picked.astype(BF16), before, preferred_element_type=F32) + base_ref[:, 0:1]
    for k in range(TOP_K):
        pos = jnp.sum(jnp.where(hots[k], rank, 0.0), axis=0, keepdims=True)
        pos_ref[k:k + 1, :] = pos.astype(jnp.int32)
    base_ref[...] = base_ref[...] + jnp.sum(picked, axis=1, keepdims=True)
    cnt_ref[...] = base_ref[...].astype(jnp.int32)


def _router(x1, g, w_router, b_router, tm=512):
    t, d = x1.shape
    tm = min(tm, t)
    wt = w_router.T
    b = jnp.broadcast_to(b_router.reshape(-1, 1), (N_EXPERTS, LANES))
    tok = lambda dt: (pl.BlockSpec((TOP_K, tm), lambda i: (0, i)), jax.ShapeDtypeStruct((TOP_K, t), dt))
    specs = [tok(jnp.int32), tok(F32), tok(jnp.int32),
             (pl.BlockSpec((N_EXPERTS, LANES), lambda i: (0, 0)),
              jax.ShapeDtypeStruct((N_EXPERTS, LANES), jnp.int32))]
    return pl.pallas_call(
        _router_kernel,
        grid=(t // tm,),
        in_specs=[pl.BlockSpec((tm, d), lambda i: (i, 0)),
                  pl.BlockSpec((1, d), lambda i: (0, 0)),
                  pl.BlockSpec((N_EXPERTS, d), lambda i: (0, 0)),
                  pl.BlockSpec((N_EXPERTS, LANES), lambda i: (0, 0))],
        out_specs=[s for s, _ in specs],
        out_shape=[s for _, s in specs],
        scratch_shapes=[pltpu.VMEM((N_EXPERTS, LANES), F32)],
        compiler_params=_cparams("arbitrary"),
        name="router",
    )(x1, g, wt, b)


def _dispatch_kernel(dest_ref, lo_ref, hi_ref, nu_ref, x_ref, g_ref, xs_hbm, hbuf, zbuf, sem, zsem,
                     *, n_blocks):
    i = pl.program_id(0)
    last = pl.num_programs(0) - 1
    tm = x_ref.shape[0]
    slot = i % 2

    def row_copies(step, sl, r):
        return [pltpu.make_async_copy(hbuf.at[sl, pl.ds(r, 1), :],
                                      xs_hbm.at[pl.ds(dest_ref[k, step * tm + r], 1), :], sem.at[sl])
                for k in range(TOP_K)]

    def wait_step(step, sl):
        def body(r, c):
            for cp in row_copies(step, sl, r):
                cp.wait()
            return c
        lax.fori_loop(0, tm, body, 0)

    @pl.when(i >= 2)
    def _():
        wait_step(i - 2, slot)

    hbuf[slot] = _rms(x_ref[...], g_ref[...])

    def start(r, c):
        for cp in row_copies(i, slot, r):
            cp.start()
        return c
    lax.fori_loop(0, tm, start, 0)

    @pl.when(i == 0)
    def _():
        zbuf[...] = jnp.zeros(zbuf.shape, F32)

        def pad_row(r):
            return pltpu.make_async_copy(zbuf.at[pl.ds(0, 1), :], xs_hbm.at[pl.ds(r, 1), :], zsem.at[0])

        def tail_block(b):
            return pltpu.make_async_copy(zbuf, xs_hbm.at[pl.ds(pl.multiple_of(b * tm, tm), tm), :],
                                         zsem.at[0])

        def each(fn):
            def per_expert(e, c):
                return lax.fori_loop(lo_ref[e], hi_ref[e], lambda r, c2: (fn(pad_row(r)), c2)[1], c)
            lax.fori_loop(0, N_EXPERTS, per_expert, 0)
            lax.fori_loop(nu_ref[0], n_blocks, lambda b, c: (fn(tail_block(b)), c)[1], 0)

        each(lambda cp: cp.start())
        each(lambda cp: cp.wait())

    @pl.when(i == last)
    def _():
        @pl.when(i >= 1)
        def _():
            wait_step(i - 1, 1 - slot)
        wait_step(i, slot)


def _dispatch(x1, g, dest, pad_lo, pad_hi, n_used, n_blocks, tm):
    t, d = x1.shape
    return pl.pallas_call(
        functools.partial(_dispatch_kernel, n_blocks=n_blocks),
        grid_spec=pltpu.PrefetchScalarGridSpec(
            num_scalar_prefetch=4,
            grid=(t // tm,),
            in_specs=[pl.BlockSpec((tm, d), lambda i, *_: (i, 0)),
                      pl.BlockSpec((1, d), lambda i, *_: (0, 0))],
            out_specs=pl.BlockSpec(memory_space=pl.ANY),
            scratch_shapes=[pltpu.VMEM((2, tm, d), F32), pltpu.VMEM((tm, d), F32),
                            pltpu.SemaphoreType.DMA((2,)), pltpu.SemaphoreType.DMA((1,))]),
        out_shape=jax.ShapeDtypeStruct((n_blocks * tm, d), F32),
        compiler_params=_cparams("arbitrary"),
        name="moe_dispatch",
    )(dest, pad_lo, pad_hi, n_used, x1, g)


def _swiglu_pick(y, pick):
    tm, n2 = y.shape
    even = lax.broadcasted_iota(jnp.int32, (tm, LANES), 1) % 2 == 0
    outs = []
    for cblk in range(n2 // (2 * LANES)):
        prods = []
        for s in range(2):
            c0 = cblk * 2 * LANES + s * LANES
            yc = y[:, c0:c0 + LANES]
            gate = jnp.minimum(yc, SWIGLU_LIMIT)
            gate = gate * _sigmoid(SWIGLU_ALPHA * gate)
            up = jnp.clip(yc, -SWIGLU_LIMIT, SWIGLU_LIMIT) + 1.0
            a = jnp.where(even, gate, up)
            prods.append((a * pltpu.roll(a, LANES - 1, 1)).astype(BF16))
        outs.append(jnp.dot(jnp.concatenate(prods, axis=1), pick, preferred_element_type=F32))
    return jnp.concatenate(outs, axis=1)


def _gmm_kernel(ps_ref, nb_ref, nu_ref, w_ref, b_ref, *rest, swiglu, tm, n_blocks):
    if swiglu:
        pick_ref, x_hbm, o_hbm, wbf, xbuf, obuf, sin, sout = rest
    else:
        x_hbm, o_hbm, wbf, xbuf, obuf, sin, sout = rest
    e = pl.program_id(0)
    nb = nb_ref[e]
    tn_out = obuf.shape[2]
    col = pl.multiple_of(pl.program_id(1) * tn_out, tn_out)

    def rows(b):
        return pl.ds(pl.multiple_of(ps_ref[e] + b * tm, tm), tm)

    def x_copy(b, sl):
        return pltpu.make_async_copy(x_hbm.at[rows(b), :], xbuf.at[sl], sin.at[sl])

    def o_copy(b, sl):
        return pltpu.make_async_copy(obuf.at[sl], o_hbm.at[rows(b), pl.ds(col, tn_out)], sout.at[sl])

    @pl.when(nb > 0)
    def _():
        x_copy(0, 0).start()
        wbf[...] = w_ref[...].astype(BF16)

    def block(b, c):
        sl = b % 2

        @pl.when(b + 1 < nb)
        def _():
            x_copy(b + 1, 1 - sl).start()

        x_copy(b, sl).wait()

        @pl.when(b >= 2)
        def _():
            o_copy(b - 2, sl).wait()

        y = jnp.dot(xbuf[sl].astype(BF16), wbf[...], preferred_element_type=F32) + b_ref[...]
        if swiglu:
            y = _swiglu_pick(y, pick_ref[...])
        obuf[sl] = y.astype(obuf.dtype)
        o_copy(b, sl).start()
        return c

    lax.fori_loop(0, nb, block, 0)

    @pl.when(nb >= 2)
    def _():
        o_copy(nb - 2, nb % 2).wait()

    @pl.when(nb >= 1)
    def _():
        o_copy(nb - 1, (nb + 1) % 2).wait()

    @pl.when(e == pl.num_programs(0) - 1)
    def _():
        obuf[0] = jnp.zeros(obuf.shape[1:], obuf.dtype)

        def tail(b):
            return pltpu.make_async_copy(
                obuf.at[0], o_hbm.at[pl.ds(pl.multiple_of(b * tm, tm), tm), pl.ds(col, tn_out)],
                sout.at[0])

        lax.fori_loop(nu_ref[0], n_blocks, lambda b, c: (tail(b).start(), c)[1], 0)
        lax.fori_loop(nu_ref[0], n_blocks, lambda b, c: (tail(b).wait(), c)[1], 0)


def _grouped_matmul(x, w, b, pstart, nblk, n_used, *, tm, tn, swiglu, out_dtype):
    n_slots, kdim = x.shape
    n_exp, _, n = w.shape
    n_out = n // 2 if swiglu else n
    tn_out = tn // 2 if swiglu else tn
    in_specs = [pl.BlockSpec((None, kdim, tn), lambda e, j, *_: (e, 0, j)),
                pl.BlockSpec((None, 1, tn), lambda e, j, *_: (e, 0, j))]
    args = [w, b.reshape(n_exp, 1, n)]
    if swiglu:
        src = jnp.arange(2 * LANES, dtype=jnp.int32)[:, None]
        dst = jnp.arange(LANES, dtype=jnp.int32)[None, :]
        args.append((src == 2 * dst).astype(BF16))
        in_specs.append(pl.BlockSpec((2 * LANES, LANES), lambda e, j, *_: (0, 0)))
    in_specs.append(pl.BlockSpec(memory_space=pl.ANY))
    args.append(x)
    return pl.pallas_call(
        functools.partial(_gmm_kernel, swiglu=swiglu, tm=tm, n_blocks=n_slots // tm),
        grid_spec=pltpu.PrefetchScalarGridSpec(
            num_scalar_prefetch=3,
            grid=(n_exp, n // tn),
            in_specs=in_specs,
            out_specs=pl.BlockSpec(memory_space=pl.ANY),
            scratch_shapes=[pltpu.VMEM((kdim, tn), BF16), pltpu.VMEM((2, tm, kdim), x.dtype),
                            pltpu.VMEM((2, tm, tn_out), out_dtype),
                            pltpu.SemaphoreType.DMA((2,)), pltpu.SemaphoreType.DMA((2,))]),
        out_shape=jax.ShapeDtypeStruct((n_slots, n_out), out_dtype),
        compiler_params=_cparams("arbitrary", "arbitrary"),
        name="moe_gate_up" if swiglu else "moe_down",
    )(pstart, nblk, n_used, *args)


def _row_copy(src_hbm, row, dst, r, sem):
    return pltpu.make_async_copy(src_hbm.at[pl.ds(row, 1), :], dst.at[pl.ds(r, 1), :], sem)


def _combine_kernel(dest_ref, x_ref, gate_ref, g_ref, ys_hbm, o_ref, buf_ref, sem, *, norm):
    tc = x_ref.shape[0]
    t0 = pl.program_id(0) * tc

    def start(r, c):
        for k in range(TOP_K):
            _row_copy(ys_hbm, dest_ref[k, t0 + r], buf_ref.at[k], r, sem.at[0]).start()
        return c

    def wait(r, c):
        for k in range(TOP_K):
            _row_copy(ys_hbm, dest_ref[k, t0 + r], buf_ref.at[k], r, sem.at[0]).wait()
        return c

    lax.fori_loop(0, tc, start, 0)
    lax.fori_loop(0, tc, wait, 0)
    y = x_ref[...]
    for k in range(TOP_K):
        y = y + gate_ref[:, k:k + 1] * buf_ref[k]
    o_ref[...] = _rms(y, g_ref[...]) if norm else y


def _combine(x1, gates_t, dest, ys, g_final, norm, tc=128):
    t, d = x1.shape
    tc = min(tc, t)
    return pl.pallas_call(
        functools.partial(_combine_kernel, norm=norm),
        grid_spec=pltpu.PrefetchScalarGridSpec(
            num_scalar_prefetch=1,
            grid=(t // tc,),
            in_specs=[pl.BlockSpec((tc, d), lambda i, dest: (i, 0)),
                      pl.BlockSpec((tc, TOP_K), lambda i, dest: (i, 0)),
                      pl.BlockSpec((1, d), lambda i, dest: (0, 0)),
                      pl.BlockSpec(memory_space=pl.ANY)],
            out_specs=pl.BlockSpec((tc, d), lambda i, dest: (i, 0)),
            scratch_shapes=[pltpu.VMEM((TOP_K, tc, d), F32), pltpu.SemaphoreType.DMA((1,))]),
        out_shape=jax.ShapeDtypeStruct((t, d), F32),
        compiler_params=_cparams("arbitrary"),
        name="moe_combine",
    )(dest, x1, gates_t, g_final, ys)


def _mixer(x, g_mix, w_in, conv_w, conv_b, dt_bias, a_log, d_skip, g_ssm, w_out):
    n_main = 3 * SB_WIDTH + SSM_INNER + SSM_CONV_DIM
    w_dt = jnp.pad(w_in[:, n_main:], ((0, 0), (0, LANES - SSM_HEADS)))
    q, k, v, z, xbc, dt = _in_proj(x, g_mix.reshape(1, -1), _to_bf16(w_in), w_dt)
    attn = _attention(q, k, v)
    ssm = _ssd(z, xbc, dt, conv_w, conv_b, dt_bias, a_log, d_skip, g_ssm)
    return _out_proj(x, attn, ssm, _to_bf16(w_out))


def _moe(x1, g_ffn, w_router, b_router, w_gate_up, b_gate_up, w_down, b_down, g_out, norm, tm=256):
    t, d = x1.shape
    g_ffn = g_ffn.reshape(1, -1)
    idx, gates, pos, cnt = _router(x1, g_ffn, w_router, b_router)
    counts = cnt[:, 0]
    padded = (counts + tm - 1) // tm * tm
    pend = jnp.cumsum(padded)
    pstart = pend - padded
    experts = jnp.arange(N_EXPERTS, dtype=jnp.int32)
    dest = pos + jnp.sum(jnp.where(idx[None] == experts[:, None, None],
                                   pstart[:, None, None], 0), axis=0)
    n_blocks = (t * TOP_K) // tm + N_EXPERTS
    n_used = (pend[-1] // tm).reshape(1)
    nblk = padded // tm

    xs = _dispatch(x1, g_ffn, dest, pstart + counts, pend, n_used, n_blocks, tm)
    act = _grouped_matmul(xs, w_gate_up, b_gate_up, pstart, nblk, n_used,
                          tm=tm, tn=2048, swiglu=True, out_dtype=BF16)
    ys = _grouped_matmul(act, w_down, b_down, pstart, nblk, n_used,
                         tm=tm, tn=2048, swiglu=False, out_dtype=F32)
    return _combine(x1, gates.T, dest, ys, g_out.reshape(1, -1), norm)


def kernel(x, g_mix, w_in, conv_w, conv_b, dt_bias, a_log, d_skip, g_ssm, w_out, g_ffn, w_router,
           b_router, w_gate_up, b_gate_up, w_down, b_down, g_final):
    b, s, d = x.shape
    depth = g_mix.shape[0]
    outs = []
    for bi in range(b):
        xb = x[bi]
        for l in range(depth):
            x1 = _mixer(xb, g_mix[l], w_in[l], conv_w[l], conv_b[l], dt_bias[l], a_log[l],
                        d_skip[l], g_ssm[l], w_out[l])
            xb = _moe(x1, g_ffn[l], w_router[l], b_router[l], w_gate_up[l], b_gate_up[l],
                      w_down[l], b_down[l], g_final, norm=(l == depth - 1))
        outs.append(xb)
    return outs[0][None] if b == 1 else jnp.stack(outs)
```

```python
import functools

import jax
import jax.numpy as jnp
from jax import lax
from jax.experimental import pallas as pl
from jax.experimental.pallas import tpu as pltpu

F32 = jnp.float32
BF16 = jnp.bfloat16
HIGHEST = lax.Precision.HIGHEST

LANES = 128
HEAD_DIM = 64
SB_WIDTH = 1024
SSM_INNER = 1024
SSM_GROUPS = 2
SSM_STATE = 128
SSM_CONV = 4
SSM_CONV_DIM = SSM_INNER + 2 * SSM_GROUPS * SSM_STATE
SSM_HEADS = SSM_INNER // HEAD_DIM
N_EXPERTS = 32
TOP_K = 4
SWIGLU_LIMIT = 7.0
SWIGLU_ALPHA = 1.702
RMS_EPS = 1e-5

VMEM_LIMIT = 56 * 1024 * 1024

ATTN_DEAD = -110.0


def _cparams(*sem):
    return pltpu.CompilerParams(dimension_semantics=sem, vmem_limit_bytes=VMEM_LIMIT)


def _rms(x, g):
    return x * lax.rsqrt(jnp.mean(x * x, axis=-1, keepdims=True) + RMS_EPS) * g


def _sigmoid(x):
    return 1.0 / (1.0 + jnp.exp(-x))


def _softplus(x):
    return jnp.maximum(x, 0.0) + jnp.log(1.0 + jnp.exp(-jnp.abs(x)))


def _cast_kernel(w_ref, o_ref):
    o_ref[...] = w_ref[...].astype(o_ref.dtype)


def _to_bf16(w, layer, rows=256):
    _, k, n = w.shape
    return pl.pallas_call(
        _cast_kernel,
        grid=(k // rows,),
        in_specs=[pl.BlockSpec((None, rows, n), lambda i: (layer, i, 0))],
        out_specs=pl.BlockSpec((rows, n), lambda i: (i, 0)),
        out_shape=jax.ShapeDtypeStruct((k, n), BF16),
        compiler_params=_cparams("arbitrary"),
        name="cast_bf16",
    )(w)


def _in_proj_kernel(x_ref, g_ref, w_ref, wdt_ref, q_ref, k_ref, v_ref, z_ref, xbc_ref, dt_ref):
    hb = _rms(x_ref[...], g_ref[...]).astype(BF16)

    def mm(a, b):
        return jnp.dot(hb, w_ref[:, a:b], preferred_element_type=F32)

    o = 0
    q_ref[...] = (mm(o, o + SB_WIDTH) * (HEAD_DIM ** -0.5)).astype(BF16)
    o += SB_WIDTH
    k_ref[...] = mm(o, o + SB_WIDTH).astype(BF16)
    o += SB_WIDTH
    v_ref[...] = mm(o, o + SB_WIDTH).astype(BF16)
    o += SB_WIDTH
    z_ref[...] = mm(o, o + SSM_INNER)
    o += SSM_INNER
    xbc_ref[...] = mm(o, o + SSM_CONV_DIM)
    dt_ref[...] = jnp.dot(hb, wdt_ref[...].astype(BF16), preferred_element_type=F32)


def _in_proj(x, g, w_bf, w_dt, tm=256):
    t, d = x.shape
    n = w_bf.shape[1]
    row = lambda w: pl.BlockSpec((tm, w), lambda i: (i, 0))
    outs = [(SB_WIDTH, BF16)] * 3 + [(SSM_INNER, F32), (SSM_CONV_DIM, F32), (LANES, F32)]
    return pl.pallas_call(
        _in_proj_kernel,
        grid=(t // tm,),
        in_specs=[row(d),
                  pl.BlockSpec((1, d), lambda i: (0, 0)),
                  pl.BlockSpec((d, n), lambda i: (0, 0), pipeline_mode=pl.Buffered(1)),
                  pl.BlockSpec((d, LANES), lambda i: (0, 0))],
        out_specs=[row(w) for w, _ in outs],
        out_shape=[jax.ShapeDtypeStruct((t, w), dt) for w, dt in outs],
        compiler_params=_cparams("arbitrary"),
        name="in_proj",
    )(x, g, w_bf, w_dt)


def _attn_kernel(q_ref, k_ref, v_ref, o_ref, acc_ref, carry_ref, *, tq):
    i = pl.program_id(1)
    nsub = tq // LANES
    qb0 = i * nsub
    row = lax.broadcasted_iota(jnp.int32, (LANES, LANES), 0)
    col = lax.broadcasted_iota(jnp.int32, (LANES, LANES), 1)
    past = col < row
    lo = col < HEAD_DIM
    jj = lax.broadcasted_iota(jnp.int32, (LANES, 2 * LANES), 0)
    ss = lax.broadcasted_iota(jnp.int32, (LANES, 2 * LANES), 1)
    neg_later = jnp.where((ss >= LANES) | (jj > ss), -1.0, 0.0).astype(BF16)
    nt = (((1,), (1,)), ((), ()))

    def step(n, diag):
        k0s = [pl.multiple_of(jnp.maximum(qb0 + s - n, 0) * LANES, LANES) for s in range(nsub)]
        chains = [(s, hd) for s in range(nsub) for hd in range(2)]
        logits = []
        for s, hd in chains:
            q = q_ref[s * LANES:(s + 1) * LANES, :]
            qh = jnp.where(lo if hd == 0 else jnp.logical_not(lo), q, jnp.zeros(q.shape, q.dtype))
            kt = k_ref[pl.ds(k0s[s], LANES), :]
            logits.append(lax.dot_general(qh, kt, nt, preferred_element_type=F32))
        logsig, sums = [], []
        for l in logits:
            sp = _softplus(l)
            lk = jnp.where(past, sp, 0.0) if diag else sp
            sums.append(jnp.dot(lk.astype(BF16), neg_later, preferred_element_type=F32))
            logsig.append(l - sp)
        top = None
        for c, (s, hd) in enumerate(chains):
            r = sums[c]
            if diag:
                w = jnp.where(past, jnp.exp(logsig[c] + r[:, :LANES]), 0.0)
                carry = r[:, LANES:]
            else:
                before = jnp.where(qb0 + s - n >= 0, carry_ref[c], -1e30)
                w = jnp.exp(logsig[c] + r[:, :LANES] + before)
                carry = before + r[:, LANES:]
            vt = v_ref[pl.ds(k0s[s], LANES), :]
            pv = jnp.dot(w.astype(BF16), vt, preferred_element_type=F32)
            if diag:
                acc_ref[c] = pv
            else:
                acc_ref[c] += pv
            carry_ref[c] = carry
            top = carry if top is None else jnp.maximum(top, carry)
        return jnp.max(top)

    top0 = step(0, True)

    def cond(st):
        n, top = st
        return jnp.logical_and(n <= qb0 + nsub - 1, top > ATTN_DEAD)

    def body(st):
        n, _ = st
        return n + 1, step(n, False)

    lax.while_loop(cond, body, (jnp.int32(1), top0))
    for s in range(nsub):
        o_ref[s * LANES:(s + 1) * LANES, :] = jnp.where(
            lo, acc_ref[2 * s], acc_ref[2 * s + 1]).astype(o_ref.dtype)


def _attention(q, k, v, tq=512):
    t, w = q.shape
    tq = min(tq, t)
    chains = 2 * (tq // LANES)
    return pl.pallas_call(
        functools.partial(_attn_kernel, tq=tq),
        grid=(w // LANES, t // tq),
        in_specs=[pl.BlockSpec((tq, LANES), lambda p, i: (i, p)),
                  pl.BlockSpec((t, LANES), lambda p, i: (0, p)),
                  pl.BlockSpec((t, LANES), lambda p, i: (0, p))],
        out_specs=pl.BlockSpec((tq, LANES), lambda p, i: (i, p)),
        out_shape=jax.ShapeDtypeStruct((t, w), BF16),
        scratch_shapes=[pltpu.VMEM((chains, LANES, LANES), F32),
                        pltpu.VMEM((chains, LANES, LANES), F32)],
        compiler_params=_cparams("arbitrary", "arbitrary"),
        name="sb_attention",
    )(q, k, v)


def _ssd_kernel(z_ref, xbc_ref, dt_ref, cw_ref, cb_ref, dtb_ref, alog_ref, dsk_ref, g_ref, e_ref,
                o_ref, state_ref, xcat_ref):
    c = pl.program_id(0)
    L = z_ref.shape[0]
    half = SSM_INNER // SSM_GROUPS
    pad = 8

    @pl.when(c == 0)
    def _():
        state_ref[...] = jnp.zeros(state_ref.shape, F32)
        xcat_ref[0:pad, :] = jnp.zeros((pad, SSM_CONV_DIM), F32)

    xcat_ref[pad:pad + L, :] = xbc_ref[...]
    acc = jnp.broadcast_to(cb_ref[...], (L, SSM_CONV_DIM))
    for kk in range(SSM_CONV):
        s = pad - (SSM_CONV - 1) + kk
        acc = acc + cw_ref[kk:kk + 1, :] * xcat_ref[s:s + L, :]
    xcat_ref[0:pad, :] = xcat_ref[L:L + pad, :]
    xc = acc * _sigmoid(acc)
    xs = xc[:, :SSM_INNER]
    bm = xc[:, SSM_INNER:SSM_INNER + SSM_GROUPS * SSM_STATE]
    cm = xc[:, SSM_INNER + SSM_GROUPS * SSM_STATE:]

    dtv = _softplus(dt_ref[...] + dtb_ref[...])
    adt = dtv * (-jnp.exp(alog_ref[...]))
    ri = lax.broadcasted_iota(jnp.int32, (L, L), 0)
    ci = lax.broadcasted_iota(jnp.int32, (L, L), 1)
    causal = ri >= ci
    tri = jnp.where(causal, 1.0, 0.0).astype(F32)
    acum = jnp.dot(tri, adt, precision=HIGHEST, preferred_element_type=F32)
    expand = e_ref[...]
    acx = jnp.dot(acum, expand, precision=HIGHEST, preferred_element_type=F32)
    dtx = jnp.dot(dtv, expand, precision=HIGHEST, preferred_element_type=F32)
    alx = acx[L - 1:L, :]
    xdt = xs * dtx
    acum_t = acum.T
    lo = lax.broadcasted_iota(jnp.int32, (L, LANES), 1) < HEAD_DIM
    nt = (((1,), (1,)), ((), ()))

    xdt_b = xdt.astype(BF16)
    y_parts = []
    for g in range(SSM_GROUPS):
        cg = cm[:, g * SSM_STATE:(g + 1) * SSM_STATE].astype(BF16)
        bg = bm[:, g * SSM_STATE:(g + 1) * SSM_STATE]
        y_off = jnp.dot(cg, state_ref[:, g * half:(g + 1) * half].astype(BF16),
                        preferred_element_type=F32)
        cb = lax.dot_general(cg, bg.astype(BF16), nt, preferred_element_type=F32)
        for p in range(half // LANES):
            pair = g * (half // LANES) + p
            xp = xdt_b[:, pair * LANES:(pair + 1) * LANES]
            yh = []
            for hh in range(2):
                h = 2 * pair + hh
                seg = acum[:, h:h + 1] - acum_t[h:h + 1, :]
                dec = jnp.where(causal, jnp.exp(jnp.minimum(seg, 0.0)), 0.0)
                yh.append(jnp.dot((cb * dec).astype(BF16), xp, preferred_element_type=F32))
            y_parts.append(jnp.where(lo, yh[0], yh[1])
                           + y_off[:, p * LANES:(p + 1) * LANES]
                           * jnp.exp(acx[:, pair * LANES:(pair + 1) * LANES]))
        xd = xdt[:, g * half:(g + 1) * half] * jnp.exp(alx[:, g * half:(g + 1) * half]
                                                       - acx[:, g * half:(g + 1) * half])
        upd = jnp.dot(bg.T.astype(BF16), xd.astype(BF16), preferred_element_type=F32)
        state_ref[:, g * half:(g + 1) * half] = (
            jnp.exp(alx[:, g * half:(g + 1) * half]) * state_ref[:, g * half:(g + 1) * half] + upd)

    y = jnp.concatenate(y_parts, axis=1) + dsk_ref[...] * xs
    zz = z_ref[...]
    y = y * (zz * _sigmoid(zz))
    outs = []
    for g in range(SSM_GROUPS):
        yg = y[:, g * half:(g + 1) * half]
        outs.append(yg * lax.rsqrt(jnp.mean(yg * yg, axis=-1, keepdims=True) + RMS_EPS))
    o_ref[...] = (jnp.concatenate(outs, axis=1) * g_ref[...]).astype(o_ref.dtype)


def _ssd(z, xbc, dt, conv_w, conv_b, dt_bias, a_log, d_skip, g_ssm, chunk=128):
    t = z.shape[0]
    chunk = min(chunk, t)
    padh = lambda a: jnp.pad(a.reshape(1, -1), ((0, 0), (0, LANES - a.shape[-1])))
    heads = jnp.arange(SSM_INNER, dtype=jnp.int32) // HEAD_DIM
    expand = (jnp.arange(LANES, dtype=jnp.int32)[:, None] == heads[None, :]).astype(F32)
    row = lambda w: pl.BlockSpec((chunk, w), lambda c: (c, 0))
    full = lambda a: pl.BlockSpec(a.shape, lambda c: (0, 0))
    params = [conv_w, conv_b.reshape(1, -1), padh(dt_bias), padh(a_log),
              jnp.repeat(d_skip, HEAD_DIM).reshape(1, -1), g_ssm.reshape(1, -1), expand]
    return pl.pallas_call(
        _ssd_kernel,
        grid=(t // chunk,),
        in_specs=[row(SSM_INNER), row(SSM_CONV_DIM), row(LANES)] + [full(a) for a in params],
        out_specs=row(SSM_INNER),
        out_shape=jax.ShapeDtypeStruct((t, SSM_INNER), BF16),
        scratch_shapes=[pltpu.VMEM((SSM_STATE, SSM_INNER), F32),
                        pltpu.VMEM((chunk + 8, SSM_CONV_DIM), F32)],
        compiler_params=_cparams("arbitrary"),
        name="ssd_mixer",
    )(z, xbc, dt, *params)


def _out_proj_kernel(x_ref, a_ref, s_ref, w_ref, o_ref):
    o_ref[...] = (x_ref[...]
                  + jnp.dot(a_ref[...], w_ref[0:SB_WIDTH, :], preferred_element_type=F32)
                  + jnp.dot(s_ref[...], w_ref[SB_WIDTH:, :], preferred_element_type=F32))


def _out_proj(x, attn, ssm, w_bf, tm=512):
    t, d = x.shape
    tm = min(tm, t)
    return pl.pallas_call(
        _out_proj_kernel,
        grid=(t // tm,),
        in_specs=[pl.BlockSpec((tm, d), lambda i: (i, 0)),
                  pl.BlockSpec((tm, SB_WIDTH), lambda i: (i, 0)),
                  pl.BlockSpec((tm, SSM_INNER), lambda i: (i, 0)),
                  pl.BlockSpec(w_bf.shape, lambda i: (0, 0), pipeline_mode=pl.Buffered(1))],
        out_specs=pl.BlockSpec((tm, d), lambda i: (i, 0)),
        out_shape=jax.ShapeDtypeStruct((t, d), F32),
        compiler_params=_cparams("arbitrary"),
        name="out_proj",
    )(x, attn, ssm, w_bf)


def _router_kernel(x_ref, g_ref, wt_ref, b_ref, idx_ref, gate_ref, pos_ref, cnt_ref, base_ref):
    i = pl.program_id(0)
    tm = x_ref.shape[0]

    @pl.when(i == 0)
    def _():
        base_ref[...] = jnp.zeros(base_ref.shape, F32)

    h = _rms(x_ref[...], g_ref[...])
    logits = lax.dot_general(wt_ref[...], h, (((1,), (1,)), ((), ())), precision=HIGHEST,
                             preferred_element_type=F32) + b_ref[:, 0:1]
    eio = lax.broadcasted_iota(jnp.int32, (N_EXPERTS, tm), 0).astype(F32)
    work = logits
    vals, hots = [], []
    for k in range(TOP_K):
        m = jnp.max(work, axis=0, keepdims=True)
        idx = jnp.min(jnp.where(work == m, eio, float(N_EXPERTS)), axis=0, keepdims=True)
        hot = eio == idx
        work = jnp.where(hot, -jnp.inf, work)
        vals.append(m)
        hots.append(hot)
        idx_ref[k:k + 1, :] = idx.astype(jnp.int32)
    ex = [jnp.exp(v - vals[0]) for v in vals]
    den = ex[0] + ex[1] + ex[2] + ex[3]
    picked = jnp.zeros((N_EXPERTS, tm), F32)
    for k in range(TOP_K):
        gate_ref[k:k + 1, :] = ex[k] / den
        picked = picked + jnp.where(hots[k], 1.0, 0.0)
    ti = lax.broadcasted_iota(jnp.int32, (tm, tm), 0)
    tj = lax.broadcasted_iota(jnp.int32, (tm, tm), 1)
    before = jnp.where(ti < tj, 1.0, 0.0).astype(BF16)
    rank = jnp.dot(picked.astype(BF16), before, preferred_element_type=F32) + base_ref[:, 0:1]
    for k in range(TOP_K):
        pos = jnp.sum(jnp.where(hots[k], rank, 0.0), axis=0, keepdims=True)
        pos_ref[k:k + 1, :] = pos.astype(jnp.int32)
    base_ref[...] = base_ref[...] + jnp.sum(picked, axis=1, keepdims=True)
    cnt_ref[...] = base_ref[...].astype(jnp.int32)


def _router(x1, g, w_router, b_router, tm=512):
    t, d = x1.shape
    tm = min(tm, t)
    wt = w_router.T
    b = jnp.broadcast_to(b_router.reshape(-1, 1), (N_EXPERTS, LANES))
    tok = lambda dt: (pl.BlockSpec((TOP_K, tm), lambda i: (0, i)), jax.ShapeDtypeStruct((TOP_K, t), dt))
    specs = [tok(jnp.int32), tok(F32), tok(jnp.int32),
             (pl.BlockSpec((N_EXPERTS, LANES), lambda i: (0, 0)),
              jax.ShapeDtypeStruct((N_EXPERTS, LANES), jnp.int32))]
    return pl.pallas_call(
        _router_kernel,
        grid=(t // tm,),
        in_specs=[pl.BlockSpec((tm, d), lambda i: (i, 0)),
                  pl.BlockSpec((1, d), lambda i: (0, 0)),
                  pl.BlockSpec((N_EXPERTS, d), lambda i: (0, 0)),
                  pl.BlockSpec((N_EXPERTS, LANES), lambda i: (0, 0))],
        out_specs=[s for s, _ in specs],
        out_shape=[s for _, s in specs],
        scratch_shapes=[pltpu.VMEM((N_EXPERTS, LANES), F32)],
        compiler_params=_cparams("arbitrary"),
        name="router",
    )(x1, g, wt, b)


def _dispatch_kernel(dest_ref, lo_ref, hi_ref, nu_ref, x_ref, g_ref, xs_hbm, hbuf, zbuf, sem, zsem,
                     *, n_blocks):
    i = pl.program_id(0)
    last = pl.num_programs(0) - 1
    tm = x_ref.shape[0]
    slot = i % 2

    def row_copies(step, sl, r):
        return [pltpu.make_async_copy(hbuf.at[sl, pl.ds(r, 1), :],
                                      xs_hbm.at[pl.ds(dest_ref[k, step * tm + r], 1), :], sem.at[sl])
                for k in range(TOP_K)]

    def wait_step(step, sl):
        def body(r, c):
            for cp in row_copies(step, sl, r):
                cp.wait()
            return c
        lax.fori_loop(0, tm, body, 0)

    @pl.when(i >= 2)
    def _():
        wait_step(i - 2, slot)

    hbuf[slot] = _rms(x_ref[...], g_ref[...])

    def start(r, c):
        for k, cp in enumerate(row_copies(i, slot, r)):
            cp.start(priority=k % 2)
        return c
    lax.fori_loop(0, tm, start, 0)

    @pl.when(i == 0)
    def _():
        zbuf[...] = jnp.zeros(zbuf.shape, F32)

        def pad_row(r):
            return pltpu.make_async_copy(zbuf.at[pl.ds(0, 1), :], xs_hbm.at[pl.ds(r, 1), :], zsem.at[0])

        def tail_block(b):
            return pltpu.make_async_copy(zbuf, xs_hbm.at[pl.ds(pl.multiple_of(b * tm, tm), tm), :],
                                         zsem.at[0])

        def each(fn):
            def per_expert(e, c):
                return lax.fori_loop(lo_ref[e], hi_ref[e], lambda r, c2: (fn(pad_row(r)), c2)[1], c)
            lax.fori_loop(0, N_EXPERTS, per_expert, 0)
            lax.fori_loop(nu_ref[0], n_blocks, lambda b, c: (fn(tail_block(b)), c)[1], 0)

        each(lambda cp: cp.start())
        each(lambda cp: cp.wait())

    @pl.when(i == last)
    def _():
        @pl.when(i >= 1)
        def _():
            wait_step(i - 1, 1 - slot)
        wait_step(i, slot)


def _dispatch(x1, g, dest, pad_lo, pad_hi, n_used, n_blocks, tm):
    t, d = x1.shape
    return pl.pallas_call(
        functools.partial(_dispatch_kernel, n_blocks=n_blocks),
        grid_spec=pltpu.PrefetchScalarGridSpec(
            num_scalar_prefetch=4,
            grid=(t // tm,),
            in_specs=[pl.BlockSpec((tm, d), lambda i, *_: (i, 0)),
                      pl.BlockSpec((1, d), lambda i, *_: (0, 0))],
            out_specs=pl.BlockSpec(memory_space=pl.ANY),
            scratch_shapes=[pltpu.VMEM((2, tm, d), F32), pltpu.VMEM((tm, d), F32),
                            pltpu.SemaphoreType.DMA((2,)), pltpu.SemaphoreType.DMA((1,))]),
        out_shape=jax.ShapeDtypeStruct((n_blocks * tm, d), F32),
        compiler_params=_cparams("arbitrary"),
        name="moe_dispatch",
    )(dest, pad_lo, pad_hi, n_used, x1, g)


def _swiglu_pick(y, pick):
    tm, n2 = y.shape
    even = lax.broadcasted_iota(jnp.int32, (tm, LANES), 1) % 2 == 0
    outs = []
    for cblk in range(n2 // (2 * LANES)):
        prods = []
        for s in range(2):
            c0 = cblk * 2 * LANES + s * LANES
            yc = y[:, c0:c0 + LANES]
            gate = jnp.minimum(yc, SWIGLU_LIMIT)
            gate = gate * _sigmoid(SWIGLU_ALPHA * gate)
            up = jnp.clip(yc, -SWIGLU_LIMIT, SWIGLU_LIMIT) + 1.0
            a = jnp.where(even, gate, up)
            prods.append((a * pltpu.roll(a, LANES - 1, 1)).astype(BF16))
        outs.append(jnp.dot(jnp.concatenate(prods, axis=1), pick, preferred_element_type=F32))
    return jnp.concatenate(outs, axis=1)


def _gmm_kernel(ps_ref, nb_ref, nu_ref, w_ref, b_ref, *rest, swiglu, tm, n_blocks):
    if swiglu:
        pick_ref, x_hbm, o_hbm, wbf, xbuf, obuf, sin, sout = rest
    else:
        x_hbm, o_hbm, wbf, xbuf, obuf, sin, sout = rest
    e = pl.program_id(0)
    nb = nb_ref[e]
    tn_out = obuf.shape[2]
    col = pl.multiple_of(pl.program_id(1) * tn_out, tn_out)

    def rows(b):
        return pl.ds(pl.multiple_of(ps_ref[e] + b * tm, tm), tm)

    def x_copy(b, sl):
        return pltpu.make_async_copy(x_hbm.at[rows(b), :], xbuf.at[sl], sin.at[sl])

    def o_copy(b, sl):
        return pltpu.make_async_copy(obuf.at[sl], o_hbm.at[rows(b), pl.ds(col, tn_out)], sout.at[sl])

    @pl.when(nb > 0)
    def _():
        x_copy(0, 0).start(priority=1)
        wbf[...] = w_ref[...].astype(BF16)

    def block(b, c):
        sl = b % 2

        @pl.when(b + 1 < nb)
        def _():
            x_copy(b + 1, 1 - sl).start(priority=1)

        x_copy(b, sl).wait()

        @pl.when(b >= 2)
        def _():
            o_copy(b - 2, sl).wait()

        y = jnp.dot(xbuf[sl].astype(BF16), wbf[...], preferred_element_type=F32) + b_ref[...]
        if swiglu:
            y = _swiglu_pick(y, pick_ref[...])
        obuf[sl] = y.astype(obuf.dtype)
        o_copy(b, sl).start(priority=1)
        return c

    lax.fori_loop(0, nb, block, 0)

    @pl.when(nb >= 2)
    def _():
        o_copy(nb - 2, nb % 2).wait()

    @pl.when(nb >= 1)
    def _():
        o_copy(nb - 1, (nb + 1) % 2).wait()

    @pl.when(e == pl.num_programs(0) - 1)
    def _():
        obuf[0] = jnp.zeros(obuf.shape[1:], obuf.dtype)

        def tail(b):
            return pltpu.make_async_copy(
                obuf.at[0], o_hbm.at[pl.ds(pl.multiple_of(b * tm, tm), tm), pl.ds(col, tn_out)],
                sout.at[0])

        lax.fori_loop(nu_ref[0], n_blocks, lambda b, c: (tail(b).start(), c)[1], 0)
        lax.fori_loop(nu_ref[0], n_blocks, lambda b, c: (tail(b).wait(), c)[1], 0)


def _grouped_matmul(x, w, b, pstart, nblk, n_used, *, tm, tn, swiglu, out_dtype):
    n_slots, kdim = x.shape
    n_exp, _, n = w.shape
    n_out = n // 2 if swiglu else n
    tn_out = tn // 2 if swiglu else tn
    in_specs = [pl.BlockSpec((None, kdim, tn), lambda e, j, *_: (e, 0, j)),
                pl.BlockSpec((None, 1, tn), lambda e, j, *_: (e, 0, j))]
    args = [w, b.reshape(n_exp, 1, n)]
    if swiglu:
        src = jnp.arange(2 * LANES, dtype=jnp.int32)[:, None]
        dst = jnp.arange(LANES, dtype=jnp.int32)[None, :]
        args.append((src == 2 * dst).astype(BF16))
        in_specs.append(pl.BlockSpec((2 * LANES, LANES), lambda e, j, *_: (0, 0)))
    in_specs.append(pl.BlockSpec(memory_space=pl.ANY))
    args.append(x)
    return pl.pallas_call(
        functools.partial(_gmm_kernel, swiglu=swiglu, tm=tm, n_blocks=n_slots // tm),
        grid_spec=pltpu.PrefetchScalarGridSpec(
            num_scalar_prefetch=3,
            grid=(n_exp, n // tn),
            in_specs=in_specs,
            out_specs=pl.BlockSpec(memory_space=pl.ANY),
            scratch_shapes=[pltpu.VMEM((kdim, tn), BF16), pltpu.VMEM((2, tm, kdim), x.dtype),
                            pltpu.VMEM((2, tm, tn_out), out_dtype),
                            pltpu.SemaphoreType.DMA((2,)), pltpu.SemaphoreType.DMA((2,))]),
        out_shape=jax.ShapeDtypeStruct((n_slots, n_out), out_dtype),
        compiler_params=_cparams("arbitrary", "arbitrary"),
        name="moe_gate_up" if swiglu else "moe_down",
    )(pstart, nblk, n_used, *args)


def _row_copy(src_hbm, row, dst, r, sem):
    return pltpu.make_async_copy(src_hbm.at[pl.ds(row, 1), :], dst.at[pl.ds(r, 1), :], sem)


def _combine_kernel(dest_ref, x_ref, gate_ref, g_ref, ys_hbm, o_ref, buf_ref, sem, *, norm):
    tc = x_ref.shape[0]
    t0 = pl.program_id(0) * tc

    def start(r, c):
        for k in range(TOP_K):
            _row_copy(ys_hbm, dest_ref[k, t0 + r], buf_ref.at[k], r, sem.at[0]).start(priority=k % 2)
        return c

    def wait(r, c):
        for k in range(TOP_K):
            _row_copy(ys_hbm, dest_ref[k, t0 + r], buf_ref.at[k], r, sem.at[0]).wait()
        return c

    lax.fori_loop(0, tc, start, 0)
    lax.fori_loop(0, tc, wait, 0)
    y = x_ref[...]
    for k in range(TOP_K):
        y = y + gate_ref[:, k:k + 1] * buf_ref[k]
    o_ref[...] = _rms(y, g_ref[...]) if norm else y


def _combine(x1, gates_t, dest, ys, g_final, norm, tc=128):
    t, d = x1.shape
    tc = min(tc, t)
    return pl.pallas_call(
        functools.partial(_combine_kernel, norm=norm),
        grid_spec=pltpu.PrefetchScalarGridSpec(
            num_scalar_prefetch=1,
            grid=(t // tc,),
            in_specs=[pl.BlockSpec((tc, d), lambda i, dest: (i, 0)),
                      pl.BlockSpec((tc, TOP_K), lambda i, dest: (i, 0)),
                      pl.BlockSpec((1, d), lambda i, dest: (0, 0)),
                      pl.BlockSpec(memory_space=pl.ANY)],
            out_specs=pl.BlockSpec((tc, d), lambda i, dest: (i, 0)),
            scratch_shapes=[pltpu.VMEM((TOP_K, tc, d), F32), pltpu.SemaphoreType.DMA((1,))]),
        out_shape=jax.ShapeDtypeStruct((t, d), F32),
        compiler_params=_cparams("arbitrary"),
        name="moe_combine",
    )(dest, x1, gates_t, g_final, ys)


def _mixer(x, layer, g_mix, w_in, conv_w, conv_b, dt_bias, a_log, d_skip, g_ssm, w_out):
    n_main = 3 * SB_WIDTH + SSM_INNER + SSM_CONV_DIM
    w_dt = jnp.pad(w_in[layer, :, n_main:], ((0, 0), (0, LANES - SSM_HEADS)))
    q, k, v, z, xbc, dt = _in_proj(x, g_mix.reshape(1, -1), _to_bf16(w_in, layer), w_dt)
    attn = _attention(q, k, v)
    ssm = _ssd(z, xbc, dt, conv_w, conv_b, dt_bias, a_log, d_skip, g_ssm)
    return _out_proj(x, attn, ssm, _to_bf16(w_out, layer))


def _moe(x1, g_ffn, w_router, b_router, w_gate_up, b_gate_up, w_down, b_down, g_out, norm, tm=256):
    t, d = x1.shape
    g_ffn = g_ffn.reshape(1, -1)
    idx, gates, pos, cnt = _router(x1, g_ffn, w_router, b_router)
    counts = cnt[:, 0]
    padded = (counts + tm - 1) // tm * tm
    pend = jnp.cumsum(padded)
    pstart = pend - padded
    experts = jnp.arange(N_EXPERTS, dtype=jnp.int32)
    dest = pos + jnp.sum(jnp.where(idx[None] == experts[:, None, None],
                                   pstart[:, None, None], 0), axis=0)
    n_blocks = (t * TOP_K) // tm + N_EXPERTS
    n_used = (pend[-1] // tm).reshape(1)
    nblk = padded // tm

    xs = _dispatch(x1, g_ffn, dest, pstart + counts, pend, n_used, n_blocks, tm)
    act = _grouped_matmul(xs, w_gate_up, b_gate_up, pstart, nblk, n_used,
                          tm=tm, tn=2048, swiglu=True, out_dtype=BF16)
    ys = _grouped_matmul(act, w_down, b_down, pstart, nblk, n_used,
                         tm=tm, tn=2048, swiglu=False, out_dtype=F32)
    return _combine(x1, gates.T, dest, ys, g_out.reshape(1, -1), norm)


def kernel(x, g_mix, w_in, conv_w, conv_b, dt_bias, a_log, d_skip, g_ssm, w_out, g_ffn, w_router,
           b_router, w_gate_up, b_gate_up, w_down, b_down, g_final):
    b, s, d = x.shape
    depth = g_mix.shape[0]
    outs = []
    for bi in range(b):
        xb = x[bi]
        for l in range(depth):
            x1 = _mixer(xb, l, g_mix[l], w_in, conv_w[l], conv_b[l], dt_bias[l], a_log[l],
                        d_skip[l], g_ssm[l], w_out)
            xb = _moe(x1, g_ffn[l], w_router[l], b_router[l], w_gate_up[l], b_gate_up[l],
                      w_down[l], b_down[l], g_final, norm=(l == depth - 1))
        outs.append(xb)
    return outs[0][None] if b == 1 else jnp.stack(outs)
```

```python
import functools

import jax
import jax.numpy as jnp
from jax import lax
from jax.experimental import pallas as pl
from jax.experimental.pallas import tpu as pltpu

F32 = jnp.float32
BF16 = jnp.bfloat16
HIGHEST = lax.Precision.HIGHEST

LANES = 128
HEAD_DIM = 64
SB_WIDTH = 1024
SSM_INNER = 1024
SSM_GROUPS = 2
SSM_STATE = 128
SSM_CONV = 4
SSM_CONV_DIM = SSM_INNER + 2 * SSM_GROUPS * SSM_STATE
SSM_HEADS = SSM_INNER // HEAD_DIM
N_EXPERTS = 32
TOP_K = 4
SWIGLU_LIMIT = 7.0
SWIGLU_ALPHA = 1.702
RMS_EPS = 1e-5

VMEM_LIMIT = 56 * 1024 * 1024

ATTN_DEAD = -110.0

W_CHUNK_ROWS = 256


def _cparams(*sem):
    return pltpu.CompilerParams(dimension_semantics=sem, vmem_limit_bytes=VMEM_LIMIT)


def _rms(x, g):
    return x * lax.rsqrt(jnp.mean(x * x, axis=-1, keepdims=True) + RMS_EPS) * g


def _sigmoid(x):
    return 1.0 / (1.0 + jnp.exp(-x))


def _softplus(x):
    return jnp.maximum(x, 0.0) + jnp.log(1.0 + jnp.exp(-jnp.abs(x)))


def _cast_kernel(w_ref, o_ref):
    o_ref[...] = w_ref[...].astype(o_ref.dtype)


def _to_bf16(w, layer, rows=256):
    _, k, n = w.shape
    return pl.pallas_call(
        _cast_kernel,
        grid=(k // rows,),
        in_specs=[pl.BlockSpec((None, rows, n), lambda i: (layer, i, 0))],
        out_specs=pl.BlockSpec((rows, n), lambda i: (i, 0)),
        out_shape=jax.ShapeDtypeStruct((k, n), BF16),
        compiler_params=_cparams("arbitrary"),
        name="cast_bf16",
    )(w)


def _in_proj_kernel(x_ref, g_ref, w_ref, wdt_ref, q_ref, k_ref, v_ref, z_ref, xbc_ref, dt_ref):
    hb = _rms(x_ref[...], g_ref[...]).astype(BF16)

    def mm(a, b):
        return jnp.dot(hb, w_ref[:, a:b], preferred_element_type=F32)

    o = 0
    q_ref[...] = (mm(o, o + SB_WIDTH) * (HEAD_DIM ** -0.5)).astype(BF16)
    o += SB_WIDTH
    k_ref[...] = mm(o, o + SB_WIDTH).astype(BF16)
    o += SB_WIDTH
    v_ref[...] = mm(o, o + SB_WIDTH).astype(BF16)
    o += SB_WIDTH
    z_ref[...] = mm(o, o + SSM_INNER)
    o += SSM_INNER
    xbc_ref[...] = mm(o, o + SSM_CONV_DIM)
    dt_ref[...] = jnp.dot(hb, wdt_ref[...].astype(BF16), preferred_element_type=F32)


def _in_proj(x, g, w_bf, w_dt, tm=256):
    t, d = x.shape
    n = w_bf.shape[1]
    row = lambda w: pl.BlockSpec((tm, w), lambda i: (i, 0))
    outs = [(SB_WIDTH, BF16)] * 3 + [(SSM_INNER, F32), (SSM_CONV_DIM, F32), (LANES, F32)]
    return pl.pallas_call(
        _in_proj_kernel,
        grid=(t // tm,),
        in_specs=[row(d),
                  pl.BlockSpec((1, d), lambda i: (0, 0)),
                  pl.BlockSpec((d, n), lambda i: (0, 0), pipeline_mode=pl.Buffered(1)),
                  pl.BlockSpec((d, LANES), lambda i: (0, 0))],
        out_specs=[row(w) for w, _ in outs],
        out_shape=[jax.ShapeDtypeStruct((t, w), dt) for w, dt in outs],
        compiler_params=_cparams("arbitrary"),
        name="in_proj",
    )(x, g, w_bf, w_dt)


def _attn_kernel(q_ref, k_ref, v_ref, o_ref, acc_ref, carry_ref, *, tq):
    i = pl.program_id(1)
    nsub = tq // LANES
    qb0 = i * nsub
    row = lax.broadcasted_iota(jnp.int32, (LANES, LANES), 0)
    col = lax.broadcasted_iota(jnp.int32, (LANES, LANES), 1)
    past = col < row
    lo = col < HEAD_DIM
    jj = lax.broadcasted_iota(jnp.int32, (LANES, 2 * LANES), 0)
    ss = lax.broadcasted_iota(jnp.int32, (LANES, 2 * LANES), 1)
    neg_later = jnp.where((ss >= LANES) | (jj > ss), -1.0, 0.0).astype(BF16)
    nt = (((1,), (1,)), ((), ()))

    def step(n, diag):
        k0s = [pl.multiple_of(jnp.maximum(qb0 + s - n, 0) * LANES, LANES) for s in range(nsub)]
        chains = [(s, hd) for s in range(nsub) for hd in range(2)]
        logits = []
        for s, hd in chains:
            q = q_ref[s * LANES:(s + 1) * LANES, :]
            qh = jnp.where(lo if hd == 0 else jnp.logical_not(lo), q, jnp.zeros(q.shape, q.dtype))
            kt = k_ref[pl.ds(k0s[s], LANES), :]
            logits.append(lax.dot_general(qh, kt, nt, preferred_element_type=F32))
        logsig, sums = [], []
        for l in logits:
            sp = _softplus(l)
            lk = jnp.where(past, sp, 0.0) if diag else sp
            sums.append(jnp.dot(lk.astype(BF16), neg_later, preferred_element_type=F32))
            logsig.append(l - sp)
        top = None
        for c, (s, hd) in enumerate(chains):
            r = sums[c]
            if diag:
                w = jnp.where(past, jnp.exp(logsig[c] + r[:, :LANES]), 0.0)
                carry = r[:, LANES:]
            else:
                before = jnp.where(qb0 + s - n >= 0, carry_ref[c], -1e30)
                w = jnp.exp(logsig[c] + r[:, :LANES] + before)
                carry = before + r[:, LANES:]
            vt = v_ref[pl.ds(k0s[s], LANES), :]
            pv = jnp.dot(w.astype(BF16), vt, preferred_element_type=F32)
            if diag:
                acc_ref[c] = pv
            else:
                acc_ref[c] += pv
            carry_ref[c] = carry
            top = carry if top is None else jnp.maximum(top, carry)
        return jnp.max(top)

    top0 = step(0, True)

    def cond(st):
        n, top = st
        return jnp.logical_and(n <= qb0 + nsub - 1, top > ATTN_DEAD)

    def body(st):
        n, _ = st
        return n + 1, step(n, False)

    lax.while_loop(cond, body, (jnp.int32(1), top0))
    for s in range(nsub):
        o_ref[s * LANES:(s + 1) * LANES, :] = jnp.where(
            lo, acc_ref[2 * s], acc_ref[2 * s + 1]).astype(o_ref.dtype)


def _attention(q, k, v, tq=512):
    t, w = q.shape
    tq = min(tq, t)
    chains = 2 * (tq // LANES)
    return pl.pallas_call(
        functools.partial(_attn_kernel, tq=tq),
        grid=(w // LANES, t // tq),
        in_specs=[pl.BlockSpec((tq, LANES), lambda p, i: (i, p)),
                  pl.BlockSpec((t, LANES), lambda p, i: (0, p)),
                  pl.BlockSpec((t, LANES), lambda p, i: (0, p))],
        out_specs=pl.BlockSpec((tq, LANES), lambda p, i: (i, p)),
        out_shape=jax.ShapeDtypeStruct((t, w), BF16),
        scratch_shapes=[pltpu.VMEM((chains, LANES, LANES), F32),
                        pltpu.VMEM((chains, LANES, LANES), F32)],
        compiler_params=_cparams("arbitrary", "arbitrary"),
        name="sb_attention",
    )(q, k, v)


def _ssd_kernel(z_ref, xbc_ref, dt_ref, cw_ref, cb_ref, dtb_ref, alog_ref, dsk_ref, g_ref, e_ref,
                o_ref, state_ref, xcat_ref):
    c = pl.program_id(0)
    L = z_ref.shape[0]
    half = SSM_INNER // SSM_GROUPS
    pad = 8

    @pl.when(c == 0)
    def _():
        state_ref[...] = jnp.zeros(state_ref.shape, F32)
        xcat_ref[0:pad, :] = jnp.zeros((pad, SSM_CONV_DIM), F32)

    xcat_ref[pad:pad + L, :] = xbc_ref[...]
    acc = jnp.broadcast_to(cb_ref[...], (L, SSM_CONV_DIM))
    for kk in range(SSM_CONV):
        s = pad - (SSM_CONV - 1) + kk
        acc = acc + cw_ref[kk:kk + 1, :] * xcat_ref[s:s + L, :]
    xcat_ref[0:pad, :] = xcat_ref[L:L + pad, :]
    xc = acc * _sigmoid(acc)
    xs = xc[:, :SSM_INNER]
    bm = xc[:, SSM_INNER:SSM_INNER + SSM_GROUPS * SSM_STATE]
    cm = xc[:, SSM_INNER + SSM_GROUPS * SSM_STATE:]

    dtv = _softplus(dt_ref[...] + dtb_ref[...])
    adt = dtv * (-jnp.exp(alog_ref[...]))
    ri = lax.broadcasted_iota(jnp.int32, (L, L), 0)
    ci = lax.broadcasted_iota(jnp.int32, (L, L), 1)
    causal = ri >= ci
    tri = jnp.where(causal, 1.0, 0.0).astype(F32)
    acum = jnp.dot(tri, adt, precision=HIGHEST, preferred_element_type=F32)
    expand = e_ref[...]
    acx = jnp.dot(acum, expand, precision=HIGHEST, preferred_element_type=F32)
    dtx = jnp.dot(dtv, expand, precision=HIGHEST, preferred_element_type=F32)
    alx = acx[L - 1:L, :]
    xdt = xs * dtx
    acum_t = acum.T
    lo = lax.broadcasted_iota(jnp.int32, (L, LANES), 1) < HEAD_DIM
    nt = (((1,), (1,)), ((), ()))

    xdt_b = xdt.astype(BF16)
    y_parts = []
    for g in range(SSM_GROUPS):
        cg = cm[:, g * SSM_STATE:(g + 1) * SSM_STATE].astype(BF16)
        bg = bm[:, g * SSM_STATE:(g + 1) * SSM_STATE]
        y_off = jnp.dot(cg, state_ref[:, g * half:(g + 1) * half].astype(BF16),
                        preferred_element_type=F32)
        cb = lax.dot_general(cg, bg.astype(BF16), nt, preferred_element_type=F32)
        for p in range(half // LANES):
            pair = g * (half // LANES) + p
            xp = xdt_b[:, pair * LANES:(pair + 1) * LANES]
            yh = []
            for hh in range(2):
                h = 2 * pair + hh
                seg = acum[:, h:h + 1] - acum_t[h:h + 1, :]
                dec = jnp.where(causal, jnp.exp(jnp.minimum(seg, 0.0)), 0.0)
                yh.append(jnp.dot((cb * dec).astype(BF16), xp, preferred_element_type=F32))
            y_parts.append(jnp.where(lo, yh[0], yh[1])
                           + y_off[:, p * LANES:(p + 1) * LANES]
                           * jnp.exp(acx[:, pair * LANES:(pair + 1) * LANES]))
        xd = xdt[:, g * half:(g + 1) * half] * jnp.exp(alx[:, g * half:(g + 1) * half]
                                                       - acx[:, g * half:(g + 1) * half])
        upd = jnp.dot(bg.T.astype(BF16), xd.astype(BF16), preferred_element_type=F32)
        state_ref[:, g * half:(g + 1) * half] = (
            jnp.exp(alx[:, g * half:(g + 1) * half]) * state_ref[:, g * half:(g + 1) * half] + upd)

    y = jnp.concatenate(y_parts, axis=1) + dsk_ref[...] * xs
    zz = z_ref[...]
    y = y * (zz * _sigmoid(zz))
    outs = []
    for g in range(SSM_GROUPS):
        yg = y[:, g * half:(g + 1) * half]
        outs.append(yg * lax.rsqrt(jnp.mean(yg * yg, axis=-1, keepdims=True) + RMS_EPS))
    o_ref[...] = (jnp.concatenate(outs, axis=1) * g_ref[...]).astype(o_ref.dtype)


def _ssd(z, xbc, dt, conv_w, conv_b, dt_bias, a_log, d_skip, g_ssm, chunk=128):
    t = z.shape[0]
    chunk = min(chunk, t)
    padh = lambda a: jnp.pad(a.reshape(1, -1), ((0, 0), (0, LANES - a.shape[-1])))
    heads = jnp.arange(SSM_INNER, dtype=jnp.int32) // HEAD_DIM
    expand = (jnp.arange(LANES, dtype=jnp.int32)[:, None] == heads[None, :]).astype(F32)
    row = lambda w: pl.BlockSpec((chunk, w), lambda c: (c, 0))
    full = lambda a: pl.BlockSpec(a.shape, lambda c: (0, 0))
    params = [conv_w, conv_b.reshape(1, -1), padh(dt_bias), padh(a_log),
              jnp.repeat(d_skip, HEAD_DIM).reshape(1, -1), g_ssm.reshape(1, -1), expand]
    return pl.pallas_call(
        _ssd_kernel,
        grid=(t // chunk,),
        in_specs=[row(SSM_INNER), row(SSM_CONV_DIM), row(LANES)] + [full(a) for a in params],
        out_specs=row(SSM_INNER),
        out_shape=jax.ShapeDtypeStruct((t, SSM_INNER), BF16),
        scratch_shapes=[pltpu.VMEM((SSM_STATE, SSM_INNER), F32),
                        pltpu.VMEM((chunk + 8, SSM_CONV_DIM), F32)],
        compiler_params=_cparams("arbitrary"),
        name="ssd_mixer",
    )(z, xbc, dt, *params)


def _out_proj_kernel(x_ref, a_ref, s_ref, w_ref, o_ref):
    o_ref[...] = (x_ref[...]
                  + jnp.dot(a_ref[...], w_ref[0:SB_WIDTH, :], preferred_element_type=F32)
                  + jnp.dot(s_ref[...], w_ref[SB_WIDTH:, :], preferred_element_type=F32))


def _out_proj(x, attn, ssm, w_bf, tm=512):
    t, d = x.shape
    tm = min(tm, t)
    return pl.pallas_call(
        _out_proj_kernel,
        grid=(t // tm,),
        in_specs=[pl.BlockSpec((tm, d), lambda i: (i, 0)),
                  pl.BlockSpec((tm, SB_WIDTH), lambda i: (i, 0)),
                  pl.BlockSpec((tm, SSM_INNER), lambda i: (i, 0)),
                  pl.BlockSpec(w_bf.shape, lambda i: (0, 0), pipeline_mode=pl.Buffered(1))],
        out_specs=pl.BlockSpec((tm, d), lambda i: (i, 0)),
        out_shape=jax.ShapeDtypeStruct((t, d), F32),
        compiler_params=_cparams("arbitrary"),
        name="out_proj",
    )(x, attn, ssm, w_bf)


def _router_kernel(x_ref, g_ref, wt_ref, b_ref, idx_ref, gate_ref, pos_ref, cnt_ref, base_ref):
    i = pl.program_id(0)
    tm = x_ref.shape[0]

    @pl.when(i == 0)
    def _():
        base_ref[...] = jnp.zeros(base_ref.shape, F32)

    h = _rms(x_ref[...], g_ref[...])
    logits = lax.dot_general(wt_ref[...], h, (((1,), (1,)), ((), ())), precision=HIGHEST,
                             preferred_element_type=F32) + b_ref[:, 0:1]
    eio = lax.broadcasted_iota(jnp.int32, (N_EXPERTS, tm), 0).astype(F32)
    work = logits
    vals, hots = [], []
    for k in range(TOP_K):
        m = jnp.max(work, axis=0, keepdims=True)
        idx = jnp.min(jnp.where(work == m, eio, float(N_EXPERTS)), axis=0, keepdims=True)
        hot = eio == idx
        work = jnp.where(hot, -jnp.inf, work)
        vals.append(m)
        hots.append(hot)
        idx_ref[k:k + 1, :] = idx.astype(jnp.int32)
    ex = [jnp.exp(v - vals[0]) for v in vals]
    den = ex[0] + ex[1] + ex[2] + ex[3]
    picked = jnp.zeros((N_EXPERTS, tm), F32)
    for k in range(TOP_K):
        gate_ref[k:k + 1, :] = ex[k] / den
        picked = picked + jnp.where(hots[k], 1.0, 0.0)
    ti = lax.broadcasted_iota(jnp.int32, (tm, tm), 0)
    tj = lax.broadcasted_iota(jnp.int32, (tm, tm), 1)
    before = jnp.where(ti < tj, 1.0, 0.0).astype(BF16)
    rank = jnp.dot(picked.astype(BF16), before, preferred_element_type=F32) + base_ref[:, 0:1]
    for k in range(TOP_K):
        pos = jnp.sum(jnp.where(hots[k], rank, 0.0), axis=0, keepdims=True)
        pos_ref[k:k + 1, :] = pos.astype(jnp.int32)
    base_ref[...] = base_ref[...] + jnp.sum(picked, axis=1, keepdims=True)
    cnt_ref[...] = base_ref[...].astype(jnp.int32)


def _router(x1, g, w_router, b_router, tm=512):
    t, d = x1.shape
    tm = min(tm, t)
    wt = w_router.T
    b = jnp.broadcast_to(b_router.reshape(-1, 1), (N_EXPERTS, LANES))
    tok = lambda dt: (pl.BlockSpec((TOP_K, tm), lambda i: (0, i)), jax.ShapeDtypeStruct((TOP_K, t), dt))
    specs = [tok(jnp.int32), tok(F32), tok(jnp.int32),
             (pl.BlockSpec((N_EXPERTS, LANES), lambda i: (0, 0)),
              jax.ShapeDtypeStruct((N_EXPERTS, LANES), jnp.int32))]
    return pl.pallas_call(
        _router_kernel,
        grid=(t // tm,),
        in_specs=[pl.BlockSpec((tm, d), lambda i: (i, 0)),
                  pl.BlockSpec((1, d), lambda i: (0, 0)),
                  pl.BlockSpec((N_EXPERTS, d), lambda i: (0, 0)),
                  pl.BlockSpec((N_EXPERTS, LANES), lambda i: (0, 0))],
        out_specs=[s for s, _ in specs],
        out_shape=[s for _, s in specs],
        scratch_shapes=[pltpu.VMEM((N_EXPERTS, LANES), F32)],
        compiler_params=_cparams("arbitrary"),
        name="router",
    )(x1, g, wt, b)


def _dispatch_kernel(dest_ref, lo_ref, hi_ref, nu_ref, x_ref, g_ref, xs_hbm, hbuf, zbuf, sem, zsem,
                     *, n_blocks):
    i = pl.program_id(0)
    last = pl.num_programs(0) - 1
    tm = x_ref.shape[0]
    slot = i % 2

    def row_copies(step, sl, r):
        return [pltpu.make_async_copy(hbuf.at[sl, pl.ds(r, 1), :],
                                      xs_hbm.at[pl.ds(dest_ref[k, step * tm + r], 1), :], sem.at[sl])
                for k in range(TOP_K)]

    def wait_step(step, sl):
        def body(r, c):
            for cp in row_copies(step, sl, r):
                cp.wait()
            return c
        lax.fori_loop(0, tm, body, 0)

    @pl.when(i >= 2)
    def _():
        wait_step(i - 2, slot)

    hbuf[slot] = _rms(x_ref[...], g_ref[...])

    def start(r, c):
        for k, cp in enumerate(row_copies(i, slot, r)):
            cp.start(priority=k % 2)
        return c
    lax.fori_loop(0, tm, start, 0)

    @pl.when(i == 0)
    def _():
        zbuf[...] = jnp.zeros(zbuf.shape, F32)

        def pad_row(r):
            return pltpu.make_async_copy(zbuf.at[pl.ds(0, 1), :], xs_hbm.at[pl.ds(r, 1), :], zsem.at[0])

        def tail_block(b):
            return pltpu.make_async_copy(zbuf, xs_hbm.at[pl.ds(pl.multiple_of(b * tm, tm), tm), :],
                                         zsem.at[0])

        def each(fn):
            def per_expert(e, c):
                return lax.fori_loop(lo_ref[e], hi_ref[e], lambda r, c2: (fn(pad_row(r)), c2)[1], c)
            lax.fori_loop(0, N_EXPERTS, per_expert, 0)
            lax.fori_loop(nu_ref[0], n_blocks, lambda b, c: (fn(tail_block(b)), c)[1], 0)

        each(lambda cp: cp.start())
        each(lambda cp: cp.wait())

    @pl.when(i == last)
    def _():
        @pl.when(i >= 1)
        def _():
            wait_step(i - 1, 1 - slot)
        wait_step(i, slot)


def _dispatch(x1, g, dest, pad_lo, pad_hi, n_used, n_blocks, tm):
    t, d = x1.shape
    return pl.pallas_call(
        functools.partial(_dispatch_kernel, n_blocks=n_blocks),
        grid_spec=pltpu.PrefetchScalarGridSpec(
            num_scalar_prefetch=4,
            grid=(t // tm,),
            in_specs=[pl.BlockSpec((tm, d), lambda i, *_: (i, 0)),
                      pl.BlockSpec((1, d), lambda i, *_: (0, 0))],
            out_specs=pl.BlockSpec(memory_space=pl.ANY),
            scratch_shapes=[pltpu.VMEM((2, tm, d), F32), pltpu.VMEM((tm, d), F32),
                            pltpu.SemaphoreType.DMA((2,)), pltpu.SemaphoreType.DMA((1,))]),
        out_shape=jax.ShapeDtypeStruct((n_blocks * tm, d), F32),
        compiler_params=_cparams("arbitrary"),
        name="moe_dispatch",
    )(dest, pad_lo, pad_hi, n_used, x1, g)


def _swiglu_pick(y, pick):
    tm, n2 = y.shape
    even = lax.broadcasted_iota(jnp.int32, (tm, LANES), 1) % 2 == 0
    outs = []
    for cblk in range(n2 // (2 * LANES)):
        prods = []
        for s in range(2):
            c0 = cblk * 2 * LANES + s * LANES
            yc = y[:, c0:c0 + LANES]
            gate = jnp.minimum(yc, SWIGLU_LIMIT)
            gate = gate * _sigmoid(SWIGLU_ALPHA * gate)
            up = jnp.clip(yc, -SWIGLU_LIMIT, SWIGLU_LIMIT) + 1.0
            a = jnp.where(even, gate, up)
            prods.append((a * pltpu.roll(a, LANES - 1, 1)).astype(BF16))
        outs.append(jnp.dot(jnp.concatenate(prods, axis=1), pick, preferred_element_type=F32))
    return jnp.concatenate(outs, axis=1)


def _gmm_kernel(ps_ref, nb_ref, nu_ref, b_ref, *rest, swiglu, tm, n_blocks):
    if swiglu:
        pick_ref, w_hbm, x_hbm, o_hbm, wbuf, wbf, xbuf, obuf, wsem, sin, sout = rest
    else:
        w_hbm, x_hbm, o_hbm, wbuf, wbf, xbuf, obuf, wsem, sin, sout = rest
    e = pl.program_id(0)
    nj = pl.num_programs(1)
    step = e * nj + pl.program_id(1)
    have_next = step < pl.num_programs(0) * nj - 1
    nb = nb_ref[e]
    kdim, tn = wbf.shape
    tn_out = obuf.shape[2]
    col = pl.multiple_of(pl.program_id(1) * tn_out, tn_out)
    nch = kdim // W_CHUNK_ROWS

    def w_chunk(st, c):
        r = pl.ds(pl.multiple_of(c * W_CHUNK_ROWS, W_CHUNK_ROWS), W_CHUNK_ROWS)
        cols = pl.ds(pl.multiple_of((st % nj) * tn, tn), tn)
        return pltpu.make_async_copy(w_hbm.at[st // nj, r, cols], wbuf.at[st % 2, r, :],
                                     wsem.at[st % 2])

    def start_chunks(st, lo, hi):
        lax.fori_loop(lo, hi, lambda c, x: (w_chunk(st, c).start(), x)[1], 0)

    def rows(b):
        return pl.ds(pl.multiple_of(ps_ref[e] + b * tm, tm), tm)

    def x_copy(b, sl):
        return pltpu.make_async_copy(x_hbm.at[rows(b), :], xbuf.at[sl], sin.at[sl])

    def o_copy(b, sl):
        return pltpu.make_async_copy(obuf.at[sl], o_hbm.at[rows(b), pl.ds(col, tn_out)], sout.at[sl])

    @pl.when(step == 0)
    def _():
        start_chunks(step, 0, nch)

    @pl.when(nb > 0)
    def _():
        x_copy(0, 0).start()

    lax.fori_loop(0, nch, lambda c, x: (w_chunk(step, c).wait(), x)[1], 0)

    @pl.when(nb > 0)
    def _():
        wbf[...] = wbuf[step % 2].astype(BF16)

    per_block = (nch + jnp.maximum(nb, 1) - 1) // jnp.maximum(nb, 1)

    def block(b, c):
        sl = b % 2

        @pl.when(b + 1 < nb)
        def _():
            x_copy(b + 1, 1 - sl).start()

        @pl.when(have_next)
        def _():
            start_chunks(step + 1, jnp.minimum(b * per_block, nch),
                         jnp.minimum((b + 1) * per_block, nch))

        x_copy(b, sl).wait()

        @pl.when(b >= 2)
        def _():
            o_copy(b - 2, sl).wait()

        y = jnp.dot(xbuf[sl].astype(BF16), wbf[...], preferred_element_type=F32) + b_ref[...]
        if swiglu:
            y = _swiglu_pick(y, pick_ref[...])
        obuf[sl] = y.astype(obuf.dtype)
        o_copy(b, sl).start()
        return c

    lax.fori_loop(0, nb, block, 0)

    @pl.when(jnp.logical_and(have_next, nb == 0))
    def _():
        start_chunks(step + 1, 0, nch)

    @pl.when(nb >= 2)
    def _():
        o_copy(nb - 2, nb % 2).wait()

    @pl.when(nb >= 1)
    def _():
        o_copy(nb - 1, (nb + 1) % 2).wait()

    @pl.when(e == pl.num_programs(0) - 1)
    def _():
        obuf[0] = jnp.zeros(obuf.shape[1:], obuf.dtype)

        def tail(b):
            return pltpu.make_async_copy(
                obuf.at[0], o_hbm.at[pl.ds(pl.multiple_of(b * tm, tm), tm), pl.ds(col, tn_out)],
                sout.at[0])

        lax.fori_loop(nu_ref[0], n_blocks, lambda b, c: (tail(b).start(), c)[1], 0)
        lax.fori_loop(nu_ref[0], n_blocks, lambda b, c: (tail(b).wait(), c)[1], 0)


def _grouped_matmul(x, w, b, pstart, nblk, n_used, *, tm, tn, swiglu, out_dtype):
    n_slots, kdim = x.shape
    n_exp, _, n = w.shape
    n_out = n // 2 if swiglu else n
    tn_out = tn // 2 if swiglu else tn
    in_specs = [pl.BlockSpec((None, 1, tn), lambda e, j, *_: (e, 0, j))]
    args = [b.reshape(n_exp, 1, n)]
    if swiglu:
        src = jnp.arange(2 * LANES, dtype=jnp.int32)[:, None]
        dst = jnp.arange(LANES, dtype=jnp.int32)[None, :]
        args.append((src == 2 * dst).astype(BF16))
        in_specs.append(pl.BlockSpec((2 * LANES, LANES), lambda e, j, *_: (0, 0)))
    in_specs += [pl.BlockSpec(memory_space=pl.ANY)] * 2
    args += [w, x]
    return pl.pallas_call(
        functools.partial(_gmm_kernel, swiglu=swiglu, tm=tm, n_blocks=n_slots // tm),
        grid_spec=pltpu.PrefetchScalarGridSpec(
            num_scalar_prefetch=3,
            grid=(n_exp, n // tn),
            in_specs=in_specs,
            out_specs=pl.BlockSpec(memory_space=pl.ANY),
            scratch_shapes=[pltpu.VMEM((2, kdim, tn), F32), pltpu.VMEM((kdim, tn), BF16),
                            pltpu.VMEM((2, tm, kdim), x.dtype), pltpu.VMEM((2, tm, tn_out), out_dtype),
                            pltpu.SemaphoreType.DMA((2,)), pltpu.SemaphoreType.DMA((2,)),
                            pltpu.SemaphoreType.DMA((2,))]),
        out_shape=jax.ShapeDtypeStruct((n_slots, n_out), out_dtype),
        compiler_params=_cparams("arbitrary", "arbitrary"),
        name="moe_gate_up" if swiglu else "moe_down",
    )(pstart, nblk, n_used, *args)


def _row_copy(src_hbm, row, dst, r, sem):
    return pltpu.make_async_copy(src_hbm.at[pl.ds(row, 1), :], dst.at[pl.ds(r, 1), :], sem)


def _combine_kernel(dest_ref, x_ref, gate_ref, g_ref, ys_hbm, o_ref, buf_ref, sem, *, norm):
    tc = x_ref.shape[0]
    t0 = pl.program_id(0) * tc

    def start(r, c):
        for k in range(TOP_K):
            _row_copy(ys_hbm, dest_ref[k, t0 + r], buf_ref.at[k], r, sem.at[0]).start(priority=k % 2)
        return c

    def wait(r, c):
        for k in range(TOP_K):
            _row_copy(ys_hbm, dest_ref[k, t0 + r], buf_ref.at[k], r, sem.at[0]).wait()
        return c

    lax.fori_loop(0, tc, start, 0)
    lax.fori_loop(0, tc, wait, 0)
    y = x_ref[...]
    for k in range(TOP_K):
        y = y + gate_ref[:, k:k + 1] * buf_ref[k]
    o_ref[...] = _rms(y, g_ref[...]) if norm else y


def _combine(x1, gates_t, dest, ys, g_final, norm, tc=128):
    t, d = x1.shape
    tc = min(tc, t)
    return pl.pallas_call(
        functools.partial(_combine_kernel, norm=norm),
        grid_spec=pltpu.PrefetchScalarGridSpec(
            num_scalar_prefetch=1,
            grid=(t // tc,),
            in_specs=[pl.BlockSpec((tc, d), lambda i, dest: (i, 0)),
                      pl.BlockSpec((tc, TOP_K), lambda i, dest: (i, 0)),
                      pl.BlockSpec((1, d), lambda i, dest: (0, 0)),
                      pl.BlockSpec(memory_space=pl.ANY)],
            out_specs=pl.BlockSpec((tc, d), lambda i, dest: (i, 0)),
            scratch_shapes=[pltpu.VMEM((TOP_K, tc, d), F32), pltpu.SemaphoreType.DMA((1,))]),
        out_shape=jax.ShapeDtypeStruct((t, d), F32),
        compiler_params=_cparams("arbitrary"),
        name="moe_combine",
    )(dest, x1, gates_t, g_final, ys)


def _mixer(x, layer, g_mix, w_in, conv_w, conv_b, dt_bias, a_log, d_skip, g_ssm, w_out):
    n_main = 3 * SB_WIDTH + SSM_INNER + SSM_CONV_DIM
    w_dt = jnp.pad(w_in[layer, :, n_main:], ((0, 0), (0, LANES - SSM_HEADS)))
    q, k, v, z, xbc, dt = _in_proj(x, g_mix.reshape(1, -1), _to_bf16(w_in, layer), w_dt)
    attn = _attention(q, k, v)
    ssm = _ssd(z, xbc, dt, conv_w, conv_b, dt_bias, a_log, d_skip, g_ssm)
    return _out_proj(x, attn, ssm, _to_bf16(w_out, layer))


def _moe(x1, g_ffn, w_router, b_router, w_gate_up, b_gate_up, w_down, b_down, g_out, norm, tm=256):
    t, d = x1.shape
    g_ffn = g_ffn.reshape(1, -1)
    idx, gates, pos, cnt = _router(x1, g_ffn, w_router, b_router)
    counts = cnt[:, 0]
    padded = (counts + tm - 1) // tm * tm
    pend = jnp.cumsum(padded)
    pstart = pend - padded
    experts = jnp.arange(N_EXPERTS, dtype=jnp.int32)
    dest = pos + jnp.sum(jnp.where(idx[None] == experts[:, None, None],
                                   pstart[:, None, None], 0), axis=0)
    n_blocks = (t * TOP_K) // tm + N_EXPERTS
    n_used = (pend[-1] // tm).reshape(1)
    nblk = padded // tm

    xs = _dispatch(x1, g_ffn, dest, pstart + counts, pend, n_used, n_blocks, tm)
    act = _grouped_matmul(xs, w_gate_up, b_gate_up, pstart, nblk, n_used,
                          tm=tm, tn=2048, swiglu=True, out_dtype=BF16)
    ys = _grouped_matmul(act, w_down, b_down, pstart, nblk, n_used,
                         tm=tm, tn=2048, swiglu=False, out_dtype=F32)
    return _combine(x1, gates.T, dest, ys, g_out.reshape(1, -1), norm)


def kernel(x, g_mix, w_in, conv_w, conv_b, dt_bias, a_log, d_skip, g_ssm, w_out, g_ffn, w_router,
           b_router, w_gate_up, b_gate_up, w_down, b_down, g_final):
    b, s, d = x.shape
    depth = g_mix.shape[0]
    outs = []
    for bi in range(b):
        xb = x[bi]
        for l in range(depth):
            x1 = _mixer(xb, l, g_mix[l], w_in, conv_w[l], conv_b[l], dt_bias[l], a_log[l],
                        d_skip[l], g_ssm[l], w_out)
            xb = _moe(x1, g_ffn[l], w_router[l], b_router[l], w_gate_up[l], b_gate_up[l],
                      w_down[l], b_down[l], g_final, norm=(l == depth - 1))
        outs.append(xb)
    return outs[0][None] if b == 1 else jnp.stack(outs)
```

```python
import functools

import jax
import jax.numpy as jnp
from jax import lax
from jax.experimental import pallas as pl
from jax.experimental.pallas import tpu as pltpu

F32 = jnp.float32
BF16 = jnp.bfloat16
HIGHEST = lax.Precision.HIGHEST

LANES = 128
HEAD_DIM = 64
SB_WIDTH = 1024
SSM_INNER = 1024
SSM_GROUPS = 2
SSM_STATE = 128
SSM_CONV = 4
SSM_CONV_DIM = SSM_INNER + 2 * SSM_GROUPS * SSM_STATE
SSM_HEADS = SSM_INNER // HEAD_DIM
N_EXPERTS = 32
TOP_K = 4
SWIGLU_LIMIT = 7.0
SWIGLU_ALPHA = 1.702
RMS_EPS = 1e-5

VMEM_LIMIT = 56 * 1024 * 1024

ATTN_DEAD = -110.0

W_CHUNK_ROWS = 256


def _cparams(*sem):
    return pltpu.CompilerParams(dimension_semantics=sem, vmem_limit_bytes=VMEM_LIMIT)


def _rms(x, g):
    return x * lax.rsqrt(jnp.mean(x * x, axis=-1, keepdims=True) + RMS_EPS) * g


def _sigmoid(x):
    return 1.0 / (1.0 + jnp.exp(-x))


def _softplus(x):
    return jnp.maximum(x, 0.0) + jnp.log(1.0 + jnp.exp(-jnp.abs(x)))


def _cast_kernel(w_ref, o_ref, *tail_ref, ntail):
    w = w_ref[...]
    o_ref[...] = w.astype(o_ref.dtype)
    if ntail:
        tail = w[:, w.shape[1] - ntail:]
        tail_ref[0][...] = jnp.concatenate(
            [tail, jnp.zeros((w.shape[0], LANES - ntail), F32)], axis=1)


def _to_bf16(w, layer, rows=256, ntail=0):
    _, k, n = w.shape
    out_specs = [pl.BlockSpec((rows, n), lambda i: (i, 0))]
    out_shape = [jax.ShapeDtypeStruct((k, n), BF16)]
    if ntail:
        out_specs.append(pl.BlockSpec((rows, LANES), lambda i: (i, 0)))
        out_shape.append(jax.ShapeDtypeStruct((k, LANES), F32))
    outs = pl.pallas_call(
        functools.partial(_cast_kernel, ntail=ntail),
        grid=(k // rows,),
        in_specs=[pl.BlockSpec((None, rows, n), lambda i: (layer, i, 0))],
        out_specs=out_specs,
        out_shape=out_shape,
        compiler_params=_cparams("arbitrary"),
        name="cast_bf16",
    )(w)
    return outs if ntail else outs[0]


def _in_proj_kernel(x_ref, g_ref, w_ref, wdt_ref, q_ref, k_ref, v_ref, z_ref, xbc_ref, dt_ref):
    hb = _rms(x_ref[...], g_ref[...]).astype(BF16)

    def mm(a, b):
        return jnp.dot(hb, w_ref[:, a:b], preferred_element_type=F32)

    o = 0
    q_ref[...] = (mm(o, o + SB_WIDTH) * (HEAD_DIM ** -0.5)).astype(BF16)
    o += SB_WIDTH
    k_ref[...] = mm(o, o + SB_WIDTH).astype(BF16)
    o += SB_WIDTH
    v_ref[...] = mm(o, o + SB_WIDTH).astype(BF16)
    o += SB_WIDTH
    z_ref[...] = mm(o, o + SSM_INNER)
    o += SSM_INNER
    xbc_ref[...] = mm(o, o + SSM_CONV_DIM)
    dt_ref[...] = jnp.dot(hb, wdt_ref[...].astype(BF16), preferred_element_type=F32)


def _in_proj(x, g, w_bf, w_dt, tm=256):
    t, d = x.shape
    n = w_bf.shape[1]
    row = lambda w: pl.BlockSpec((tm, w), lambda i: (i, 0))
    outs = [(SB_WIDTH, BF16)] * 3 + [(SSM_INNER, F32), (SSM_CONV_DIM, F32), (LANES, F32)]
    return pl.pallas_call(
        _in_proj_kernel,
        grid=(t // tm,),
        in_specs=[row(d),
                  pl.BlockSpec((1, d), lambda i: (0, 0)),
                  pl.BlockSpec((d, n), lambda i: (0, 0), pipeline_mode=pl.Buffered(1)),
                  pl.BlockSpec((d, LANES), lambda i: (0, 0))],
        out_specs=[row(w) for w, _ in outs],
        out_shape=[jax.ShapeDtypeStruct((t, w), dt) for w, dt in outs],
        compiler_params=_cparams("arbitrary"),
        name="in_proj",
    )(x, g, w_bf, w_dt)


def _attn_kernel(q_ref, k_ref, v_ref, o_ref, acc_ref, carry_ref, *, tq):
    i = pl.program_id(1)
    nsub = tq // LANES
    qb0 = i * nsub
    row = lax.broadcasted_iota(jnp.int32, (LANES, LANES), 0)
    col = lax.broadcasted_iota(jnp.int32, (LANES, LANES), 1)
    past = col < row
    lo = col < HEAD_DIM
    jj = lax.broadcasted_iota(jnp.int32, (LANES, 2 * LANES), 0)
    ss = lax.broadcasted_iota(jnp.int32, (LANES, 2 * LANES), 1)
    neg_later = jnp.where((ss >= LANES) | (jj > ss), -1.0, 0.0).astype(BF16)
    nt = (((1,), (1,)), ((), ()))

    def step(n, diag):
        k0s = [pl.multiple_of(jnp.maximum(qb0 + s - n, 0) * LANES, LANES) for s in range(nsub)]
        chains = [(s, hd) for s in range(nsub) for hd in range(2)]
        logits = []
        for s, hd in chains:
            q = q_ref[s * LANES:(s + 1) * LANES, :]
            qh = jnp.where(lo if hd == 0 else jnp.logical_not(lo), q, jnp.zeros(q.shape, q.dtype))
            kt = k_ref[pl.ds(k0s[s], LANES), :]
            logits.append(lax.dot_general(qh, kt, nt, preferred_element_type=F32))
        logsig, sums = [], []
        for l in logits:
            sp = _softplus(l)
            lk = jnp.where(past, sp, 0.0) if diag else sp
            sums.append(jnp.dot(lk.astype(BF16), neg_later, preferred_element_type=F32))
            logsig.append(l - sp)
        top = None
        for c, (s, hd) in enumerate(chains):
            r = sums[c]
            if diag:
                w = jnp.where(past, jnp.exp(logsig[c] + r[:, :LANES]), 0.0)
                carry = r[:, LANES:]
            else:
                before = jnp.where(qb0 + s - n >= 0, carry_ref[c], -1e30)
                w = jnp.exp(logsig[c] + r[:, :LANES] + before)
                carry = before + r[:, LANES:]
            vt = v_ref[pl.ds(k0s[s], LANES), :]
            pv = jnp.dot(w.astype(BF16), vt, preferred_element_type=F32)
            if diag:
                acc_ref[c] = pv
            else:
                acc_ref[c] += pv
            carry_ref[c] = carry
            top = carry if top is None else jnp.maximum(top, carry)
        return jnp.max(top)

    top0 = step(0, True)

    def cond(st):
        n, top = st
        return jnp.logical_and(n <= qb0 + nsub - 1, top > ATTN_DEAD)

    def body(st):
        n, _ = st
        return n + 1, step(n, False)

    lax.while_loop(cond, body, (jnp.int32(1), top0))
    for s in range(nsub):
        o_ref[s * LANES:(s + 1) * LANES, :] = jnp.where(
            lo, acc_ref[2 * s], acc_ref[2 * s + 1]).astype(o_ref.dtype)


def _attention(q, k, v, tq=1024):
    t, w = q.shape
    tq = min(tq, t)
    chains = 2 * (tq // LANES)
    return pl.pallas_call(
        functools.partial(_attn_kernel, tq=tq),
        grid=(w // LANES, t // tq),
        in_specs=[pl.BlockSpec((tq, LANES), lambda p, i: (i, p)),
                  pl.BlockSpec((t, LANES), lambda p, i: (0, p)),
                  pl.BlockSpec((t, LANES), lambda p, i: (0, p))],
        out_specs=pl.BlockSpec((tq, LANES), lambda p, i: (i, p)),
        out_shape=jax.ShapeDtypeStruct((t, w), BF16),
        scratch_shapes=[pltpu.VMEM((chains, LANES, LANES), F32),
                        pltpu.VMEM((chains, LANES, LANES), F32)],
        compiler_params=_cparams("arbitrary", "arbitrary"),
        name="sb_attention",
    )(q, k, v)


def _ssd_kernel(z_ref, xbc_ref, dt_ref, cw_ref, cb_ref, dtb_ref, alog_ref, dsk_ref, g_ref, e_ref,
                o_ref, state_ref, xcat_ref):
    c = pl.program_id(0)
    L = z_ref.shape[0]
    half = SSM_INNER // SSM_GROUPS
    pad = 8

    @pl.when(c == 0)
    def _():
        state_ref[...] = jnp.zeros(state_ref.shape, F32)
        xcat_ref[0:pad, :] = jnp.zeros((pad, SSM_CONV_DIM), F32)

    xcat_ref[pad:pad + L, :] = xbc_ref[...]
    acc = jnp.broadcast_to(cb_ref[...], (L, SSM_CONV_DIM))
    for kk in range(SSM_CONV):
        s = pad - (SSM_CONV - 1) + kk
        acc = acc + cw_ref[kk:kk + 1, :] * xcat_ref[s:s + L, :]
    xcat_ref[0:pad, :] = xcat_ref[L:L + pad, :]
    xc = acc * _sigmoid(acc)
    xs = xc[:, :SSM_INNER]
    bm = xc[:, SSM_INNER:SSM_INNER + SSM_GROUPS * SSM_STATE]
    cm = xc[:, SSM_INNER + SSM_GROUPS * SSM_STATE:]

    dtv = _softplus(dt_ref[...] + dtb_ref[...])
    adt = dtv * (-jnp.exp(alog_ref[...]))
    ri = lax.broadcasted_iota(jnp.int32, (L, L), 0)
    ci = lax.broadcasted_iota(jnp.int32, (L, L), 1)
    causal = ri >= ci
    tri = jnp.where(causal, 1.0, 0.0).astype(F32)
    acum = jnp.dot(tri, adt, precision=HIGHEST, preferred_element_type=F32)
    expand = e_ref[...]
    acx = jnp.dot(acum, expand, precision=HIGHEST, preferred_element_type=F32)
    dtx = jnp.dot(dtv, expand, precision=HIGHEST, preferred_element_type=F32)
    alx = acx[L - 1:L, :]
    xdt = xs * dtx
    acum_t = acum.T
    lo = lax.broadcasted_iota(jnp.int32, (L, LANES), 1) < HEAD_DIM
    nt = (((1,), (1,)), ((), ()))

    xdt_b = xdt.astype(BF16)
    y_parts = []
    for g in range(SSM_GROUPS):
        cg = cm[:, g * SSM_STATE:(g + 1) * SSM_STATE].astype(BF16)
        bg = bm[:, g * SSM_STATE:(g + 1) * SSM_STATE]
        y_off = jnp.dot(cg, state_ref[:, g * half:(g + 1) * half].astype(BF16),
                        preferred_element_type=F32)
        cb = lax.dot_general(cg, bg.astype(BF16), nt, preferred_element_type=F32)
        for p in range(half // LANES):
            pair = g * (half // LANES) + p
            xp = xdt_b[:, pair * LANES:(pair + 1) * LANES]
            yh = []
            for hh in range(2):
                h = 2 * pair + hh
                seg = acum[:, h:h + 1] - acum_t[h:h + 1, :]
                dec = jnp.where(causal, jnp.exp(jnp.minimum(seg, 0.0)), 0.0)
                yh.append(jnp.dot((cb * dec).astype(BF16), xp, preferred_element_type=F32))
            y_parts.append(jnp.where(lo, yh[0], yh[1])
                           + y_off[:, p * LANES:(p + 1) * LANES]
                           * jnp.exp(acx[:, pair * LANES:(pair + 1) * LANES]))
        xd = xdt[:, g * half:(g + 1) * half] * jnp.exp(alx[:, g * half:(g + 1) * half]
                                                       - acx[:, g * half:(g + 1) * half])
        upd = jnp.dot(bg.T.astype(BF16), xd.astype(BF16), preferred_element_type=F32)
        state_ref[:, g * half:(g + 1) * half] = (
            jnp.exp(alx[:, g * half:(g + 1) * half]) * state_ref[:, g * half:(g + 1) * half] + upd)

    y = jnp.concatenate(y_parts, axis=1) + dsk_ref[...] * xs
    zz = z_ref[...]
    y = y * (zz * _sigmoid(zz))
    outs = []
    for g in range(SSM_GROUPS):
        yg = y[:, g * half:(g + 1) * half]
        outs.append(yg * lax.rsqrt(jnp.mean(yg * yg, axis=-1, keepdims=True) + RMS_EPS))
    o_ref[...] = (jnp.concatenate(outs, axis=1) * g_ref[...]).astype(o_ref.dtype)


def _ssd(z, xbc, dt, conv_w, conv_b, dt_bias, a_log, d_skip, g_ssm, chunk=128):
    t = z.shape[0]
    chunk = min(chunk, t)
    padh = lambda a: jnp.pad(a.reshape(1, -1), ((0, 0), (0, LANES - a.shape[-1])))
    heads = jnp.arange(SSM_INNER, dtype=jnp.int32) // HEAD_DIM
    expand = (jnp.arange(LANES, dtype=jnp.int32)[:, None] == heads[None, :]).astype(F32)
    row = lambda w: pl.BlockSpec((chunk, w), lambda c: (c, 0))
    full = lambda a: pl.BlockSpec(a.shape, lambda c: (0, 0))
    params = [conv_w, conv_b.reshape(1, -1), padh(dt_bias), padh(a_log),
              jnp.repeat(d_skip, HEAD_DIM).reshape(1, -1), g_ssm.reshape(1, -1), expand]
    return pl.pallas_call(
        _ssd_kernel,
        grid=(t // chunk,),
        in_specs=[row(SSM_INNER), row(SSM_CONV_DIM), row(LANES)] + [full(a) for a in params],
        out_specs=row(SSM_INNER),
        out_shape=jax.ShapeDtypeStruct((t, SSM_INNER), BF16),
        scratch_shapes=[pltpu.VMEM((SSM_STATE, SSM_INNER), F32),
                        pltpu.VMEM((chunk + 8, SSM_CONV_DIM), F32)],
        compiler_params=_cparams("arbitrary"),
        name="ssd_mixer",
    )(z, xbc, dt, *params)


def _out_proj_kernel(x_ref, a_ref, s_ref, w_ref, o_ref):
    o_ref[...] = (x_ref[...]
                  + jnp.dot(a_ref[...], w_ref[0:SB_WIDTH, :], preferred_element_type=F32)
                  + jnp.dot(s_ref[...], w_ref[SB_WIDTH:, :], preferred_element_type=F32))


def _out_proj(x, attn, ssm, w_bf, tm=512):
    t, d = x.shape
    tm = min(tm, t)
    return pl.pallas_call(
        _out_proj_kernel,
        grid=(t // tm,),
        in_specs=[pl.BlockSpec((tm, d), lambda i: (i, 0)),
                  pl.BlockSpec((tm, SB_WIDTH), lambda i: (i, 0)),
                  pl.BlockSpec((tm, SSM_INNER), lambda i: (i, 0)),
                  pl.BlockSpec(w_bf.shape, lambda i: (0, 0), pipeline_mode=pl.Buffered(1))],
        out_specs=pl.BlockSpec((tm, d), lambda i: (i, 0)),
        out_shape=jax.ShapeDtypeStruct((t, d), F32),
        compiler_params=_cparams("arbitrary"),
        name="out_proj",
    )(x, attn, ssm, w_bf)


def _router_kernel(x_ref, g_ref, wt_ref, b_ref, idx_ref, gate_ref, pos_ref, cnt_ref, base_ref):
    i = pl.program_id(0)
    tm = x_ref.shape[0]

    @pl.when(i == 0)
    def _():
        base_ref[...] = jnp.zeros(base_ref.shape, F32)

    h = _rms(x_ref[...], g_ref[...])
    logits = lax.dot_general(wt_ref[...], h, (((1,), (1,)), ((), ())), precision=HIGHEST,
                             preferred_element_type=F32) + b_ref[:, 0:1]
    eio = lax.broadcasted_iota(jnp.int32, (N_EXPERTS, tm), 0).astype(F32)
    work = logits
    vals, hots = [], []
    for k in range(TOP_K):
        m = jnp.max(work, axis=0, keepdims=True)
        idx = jnp.min(jnp.where(work == m, eio, float(N_EXPERTS)), axis=0, keepdims=True)
        hot = eio == idx
        work = jnp.where(hot, -jnp.inf, work)
        vals.append(m)
        hots.append(hot)
        idx_ref[k:k + 1, :] = idx.astype(jnp.int32)
    ex = [jnp.exp(v - vals[0]) for v in vals]
    den = ex[0] + ex[1] + ex[2] + ex[3]
    picked = jnp.zeros((N_EXPERTS, tm), F32)
    for k in range(TOP_K):
        gate_ref[k:k + 1, :] = ex[k] / den
        picked = picked + jnp.where(hots[k], 1.0, 0.0)
    ti = lax.broadcasted_iota(jnp.int32, (tm, tm), 0)
    tj = lax.broadcasted_iota(jnp.int32, (tm, tm), 1)
    before = jnp.where(ti < tj, 1.0, 0.0).astype(BF16)
    rank = jnp.dot(picked.astype(BF16), before, preferred_element_type=F32) + base_ref[:, 0:1]
    for k in range(TOP_K):
        pos = jnp.sum(jnp.where(hots[k], rank, 0.0), axis=0, keepdims=True)
        pos_ref[k:k + 1, :] = pos.astype(jnp.int32)
    base_ref[...] = base_ref[...] + jnp.sum(picked, axis=1, keepdims=True)
    cnt_ref[...] = base_ref[...].astype(jnp.int32)


def _router(x1, g, w_router, b_router, tm=512):
    t, d = x1.shape
    tm = min(tm, t)
    wt = w_router.T
    b = jnp.broadcast_to(b_router.reshape(-1, 1), (N_EXPERTS, LANES))
    tok = lambda dt: (pl.BlockSpec((TOP_K, tm), lambda i: (0, i)), jax.ShapeDtypeStruct((TOP_K, t), dt))
    specs = [tok(jnp.int32), tok(F32), tok(jnp.int32),
             (pl.BlockSpec((N_EXPERTS, LANES), lambda i: (0, 0)),
              jax.ShapeDtypeStruct((N_EXPERTS, LANES), jnp.int32))]
    return pl.pallas_call(
        _router_kernel,
        grid=(t // tm,),
        in_specs=[pl.BlockSpec((tm, d), lambda i: (i, 0)),
                  pl.BlockSpec((1, d), lambda i: (0, 0)),
                  pl.BlockSpec((N_EXPERTS, d), lambda i: (0, 0)),
                  pl.BlockSpec((N_EXPERTS, LANES), lambda i: (0, 0))],
        out_specs=[s for s, _ in specs],
        out_shape=[s for _, s in specs],
        scratch_shapes=[pltpu.VMEM((N_EXPERTS, LANES), F32)],
        compiler_params=_cparams("arbitrary"),
        name="router",
    )(x1, g, wt, b)


def _dispatch_kernel(dest_ref, lo_ref, hi_ref, nu_ref, x_ref, g_ref, xs_hbm, hbuf, zbuf, sem, zsem,
                     *, n_blocks):
    i = pl.program_id(0)
    last = pl.num_programs(0) - 1
    tm = x_ref.shape[0]
    slot = i % 2

    def row_copies(step, sl, r):
        return [pltpu.make_async_copy(hbuf.at[sl, pl.ds(r, 1), :],
                                      xs_hbm.at[pl.ds(dest_ref[k, step * tm + r], 1), :], sem.at[sl])
                for k in range(TOP_K)]

    def wait_step(step, sl):
        def body(r, c):
            for cp in row_copies(step, sl, r):
                cp.wait()
            return c
        lax.fori_loop(0, tm, body, 0)

    @pl.when(i >= 2)
    def _():
        wait_step(i - 2, slot)

    hbuf[slot] = _rms(x_ref[...], g_ref[...])

    def start(r, c):
        for k, cp in enumerate(row_copies(i, slot, r)):
            cp.start(priority=k % 2)
        return c
    lax.fori_loop(0, tm, start, 0)

    @pl.when(i == 0)
    def _():
        zbuf[...] = jnp.zeros(zbuf.shape, F32)

        def pad_row(r):
            return pltpu.make_async_copy(zbuf.at[pl.ds(0, 1), :], xs_hbm.at[pl.ds(r, 1), :], zsem.at[0])

        def tail_block(b):
            return pltpu.make_async_copy(zbuf, xs_hbm.at[pl.ds(pl.multiple_of(b * tm, tm), tm), :],
                                         zsem.at[0])

        def each(fn):
            def per_expert(e, c):
                return lax.fori_loop(lo_ref[e], hi_ref[e], lambda r, c2: (fn(pad_row(r)), c2)[1], c)
            lax.fori_loop(0, N_EXPERTS, per_expert, 0)
            lax.fori_loop(nu_ref[0], n_blocks, lambda b, c: (fn(tail_block(b)), c)[1], 0)

        each(lambda cp: cp.start())
        each(lambda cp: cp.wait())

    @pl.when(i == last)
    def _():
        @pl.when(i >= 1)
        def _():
            wait_step(i - 1, 1 - slot)
        wait_step(i, slot)


def _dispatch(x1, g, dest, pad_lo, pad_hi, n_used, n_blocks, tm):
    t, d = x1.shape
    return pl.pallas_call(
        functools.partial(_dispatch_kernel, n_blocks=n_blocks),
        grid_spec=pltpu.PrefetchScalarGridSpec(
            num_scalar_prefetch=4,
            grid=(t // tm,),
            in_specs=[pl.BlockSpec((tm, d), lambda i, *_: (i, 0)),
                      pl.BlockSpec((1, d), lambda i, *_: (0, 0))],
            out_specs=pl.BlockSpec(memory_space=pl.ANY),
            scratch_shapes=[pltpu.VMEM((2, tm, d), F32), pltpu.VMEM((tm, d), F32),
                            pltpu.SemaphoreType.DMA((2,)), pltpu.SemaphoreType.DMA((1,))]),
        out_shape=jax.ShapeDtypeStruct((n_blocks * tm, d), F32),
        compiler_params=_cparams("arbitrary"),
        name="moe_dispatch",
    )(dest, pad_lo, pad_hi, n_used, x1, g)


def _swiglu_pick(y, pick):
    tm, n2 = y.shape
    even = lax.broadcasted_iota(jnp.int32, (tm, 2 * LANES), 1) % 2 == 0
    outs = []
    for cblk in range(n2 // (2 * LANES)):
        yc = y[:, cblk * 2 * LANES:(cblk + 1) * 2 * LANES]
        capped = jnp.minimum(yc, SWIGLU_LIMIT)
        gate = capped * _sigmoid(SWIGLU_ALPHA * capped)
        up = jnp.maximum(capped, -SWIGLU_LIMIT) + 1.0
        factors = jnp.where(even, gate, up).astype(BF16)
        split = jnp.dot(factors, pick, preferred_element_type=F32)
        outs.append(split[:, :LANES] * split[:, LANES:])
    return jnp.concatenate(outs, axis=1)


def _gmm_kernel(ps_ref, nb_ref, nu_ref, b_ref, *rest, swiglu, tm, n_blocks):
    if swiglu:
        pick_ref, w_hbm, x_hbm, o_hbm, wbuf, wbf, xbuf, obuf, wsem, sin, sout = rest
    else:
        w_hbm, x_hbm, o_hbm, wbuf, wbf, xbuf, obuf, wsem, sin, sout = rest
    e = pl.program_id(0)
    nj = pl.num_programs(1)
    step = e * nj + pl.program_id(1)
    have_next = step < pl.num_programs(0) * nj - 1
    nb = nb_ref[e]
    kdim, tn = wbf.shape
    tn_out = obuf.shape[2]
    col = pl.multiple_of(pl.program_id(1) * tn_out, tn_out)
    nch = kdim // W_CHUNK_ROWS

    def w_chunk(st, c):
        r = pl.ds(pl.multiple_of(c * W_CHUNK_ROWS, W_CHUNK_ROWS), W_CHUNK_ROWS)
        cols = pl.ds(pl.multiple_of((st % nj) * tn, tn), tn)
        return pltpu.make_async_copy(w_hbm.at[st // nj, r, cols], wbuf.at[r, :], wsem.at[0])

    def start_chunks(st, lo, hi):
        lax.fori_loop(lo, hi, lambda c, x: (w_chunk(st, c).start(), x)[1], 0)

    def rows(b):
        return pl.ds(pl.multiple_of(ps_ref[e] + b * tm, tm), tm)

    def x_copy(b, sl):
        return pltpu.make_async_copy(x_hbm.at[rows(b), :], xbuf.at[sl], sin.at[sl])

    def o_copy(b, sl):
        return pltpu.make_async_copy(obuf.at[sl], o_hbm.at[rows(b), pl.ds(col, tn_out)], sout.at[sl])

    @pl.when(step == 0)
    def _():
        start_chunks(step, 0, nch)

    @pl.when(nb > 0)
    def _():
        x_copy(0, 0).start()

    lax.fori_loop(0, nch, lambda c, x: (w_chunk(step, c).wait(), x)[1], 0)

    @pl.when(nb > 0)
    def _():
        wbf[...] = wbuf[...].astype(BF16)

    per_block = (nch + jnp.maximum(nb, 1) - 1) // jnp.maximum(nb, 1)

    def block(b, c):
        sl = b % 2

        @pl.when(b + 1 < nb)
        def _():
            x_copy(b + 1, 1 - sl).start()

        @pl.when(have_next)
        def _():
            start_chunks(step + 1, jnp.minimum(b * per_block, nch),
                         jnp.minimum((b + 1) * per_block, nch))

        x_copy(b, sl).wait()

        @pl.when(b >= 2)
        def _():
            o_copy(b - 2, sl).wait()

        y = jnp.dot(xbuf[sl].astype(BF16), wbf[...], preferred_element_type=F32) + b_ref[...]
        if swiglu:
            y = _swiglu_pick(y, pick_ref[...])
        obuf[sl] = y.astype(obuf.dtype)
        o_copy(b, sl).start()
        return c

    lax.fori_loop(0, nb, block, 0)

    @pl.when(jnp.logical_and(have_next, nb == 0))
    def _():
        start_chunks(step + 1, 0, nch)

    @pl.when(nb >= 2)
    def _():
        o_copy(nb - 2, nb % 2).wait()

    @pl.when(nb >= 1)
    def _():
        o_copy(nb - 1, (nb + 1) % 2).wait()

    @pl.when(e == pl.num_programs(0) - 1)
    def _():
        obuf[0] = jnp.zeros(obuf.shape[1:], obuf.dtype)

        def tail(b):
            return pltpu.make_async_copy(
                obuf.at[0], o_hbm.at[pl.ds(pl.multiple_of(b * tm, tm), tm), pl.ds(col, tn_out)],
                sout.at[0])

        lax.fori_loop(nu_ref[0], n_blocks, lambda b, c: (tail(b).start(), c)[1], 0)
        lax.fori_loop(nu_ref[0], n_blocks, lambda b, c: (tail(b).wait(), c)[1], 0)


def _grouped_matmul(x, w, b, pstart, nblk, n_used, *, tm, tn, swiglu, out_dtype):
    n_slots, kdim = x.shape
    n_exp, _, n = w.shape
    n_out = n // 2 if swiglu else n
    tn_out = tn // 2 if swiglu else tn
    in_specs = [pl.BlockSpec((None, 1, tn), lambda e, j, *_: (e, 0, j))]
    args = [b.reshape(n_exp, 1, n)]
    if swiglu:
        src = jnp.arange(2 * LANES, dtype=jnp.int32)[:, None]
        dst = jnp.arange(2 * LANES, dtype=jnp.int32)[None, :]
        args.append((src == 2 * (dst % LANES) + dst // LANES).astype(BF16))
        in_specs.append(pl.BlockSpec((2 * LANES, 2 * LANES), lambda e, j, *_: (0, 0)))
    in_specs += [pl.BlockSpec(memory_space=pl.ANY)] * 2
    args += [w, x]
    return pl.pallas_call(
        functools.partial(_gmm_kernel, swiglu=swiglu, tm=tm, n_blocks=n_slots // tm),
        grid_spec=pltpu.PrefetchScalarGridSpec(
            num_scalar_prefetch=3,
            grid=(n_exp, n // tn),
            in_specs=in_specs,
            out_specs=pl.BlockSpec(memory_space=pl.ANY),
            scratch_shapes=[pltpu.VMEM((kdim, tn), F32), pltpu.VMEM((kdim, tn), BF16),
                            pltpu.VMEM((2, tm, kdim), x.dtype), pltpu.VMEM((2, tm, tn_out), out_dtype),
                            pltpu.SemaphoreType.DMA((1,)), pltpu.SemaphoreType.DMA((2,)),
                            pltpu.SemaphoreType.DMA((2,))]),
        out_shape=jax.ShapeDtypeStruct((n_slots, n_out), out_dtype),
        compiler_params=_cparams("arbitrary", "arbitrary"),
        name="moe_gate_up" if swiglu else "moe_down",
    )(pstart, nblk, n_used, *args)


def _row_copy(src_hbm, row, dst, r, sem):
    return pltpu.make_async_copy(src_hbm.at[pl.ds(row, 1), :], dst.at[pl.ds(r, 1), :], sem)


def _combine_kernel(dest_ref, x_ref, gate_ref, g_ref, ys_hbm, o_ref, buf_ref, sem, *, norm):
    tc = x_ref.shape[0]
    t0 = pl.program_id(0) * tc

    def start(r, c):
        for k in range(TOP_K):
            _row_copy(ys_hbm, dest_ref[k, t0 + r], buf_ref.at[k], r, sem.at[0]).start(priority=k % 2)
        return c

    def wait(r, c):
        for k in range(TOP_K):
            _row_copy(ys_hbm, dest_ref[k, t0 + r], buf_ref.at[k], r, sem.at[0]).wait()
        return c

    lax.fori_loop(0, tc, start, 0)
    lax.fori_loop(0, tc, wait, 0)
    y = x_ref[...]
    for k in range(TOP_K):
        y = y + gate_ref[:, k:k + 1] * buf_ref[k]
    o_ref[...] = _rms(y, g_ref[...]) if norm else y


def _combine(x1, gates_t, dest, ys, g_final, norm, tc=128):
    t, d = x1.shape
    tc = min(tc, t)
    return pl.pallas_call(
        functools.partial(_combine_kernel, norm=norm),
        grid_spec=pltpu.PrefetchScalarGridSpec(
            num_scalar_prefetch=1,
            grid=(t // tc,),
            in_specs=[pl.BlockSpec((tc, d), lambda i, dest: (i, 0)),
                      pl.BlockSpec((tc, TOP_K), lambda i, dest: (i, 0)),
                      pl.BlockSpec((1, d), lambda i, dest: (0, 0)),
                      pl.BlockSpec(memory_space=pl.ANY)],
            out_specs=pl.BlockSpec((tc, d), lambda i, dest: (i, 0)),
            scratch_shapes=[pltpu.VMEM((TOP_K, tc, d), F32), pltpu.SemaphoreType.DMA((1,))]),
        out_shape=jax.ShapeDtypeStruct((t, d), F32),
        compiler_params=_cparams("arbitrary"),
        name="moe_combine",
    )(dest, x1, gates_t, g_final, ys)


def _mixer(x, layer, g_mix, w_in, conv_w, conv_b, dt_bias, a_log, d_skip, g_ssm, w_out):
    w_bf, w_dt = _to_bf16(w_in, layer, ntail=SSM_HEADS)
    q, k, v, z, xbc, dt = _in_proj(x, g_mix.reshape(1, -1), w_bf, w_dt)
    attn = _attention(q, k, v)
    ssm = _ssd(z, xbc, dt, conv_w, conv_b, dt_bias, a_log, d_skip, g_ssm)
    return _out_proj(x, attn, ssm, _to_bf16(w_out, layer))


def _moe(x1, g_ffn, w_router, b_router, w_gate_up, b_gate_up, w_down, b_down, g_out, norm, tm=256):
    t, d = x1.shape
    g_ffn = g_ffn.reshape(1, -1)
    idx, gates, pos, cnt = _router(x1, g_ffn, w_router, b_router)
    counts = cnt[:, 0]
    padded = (counts + tm - 1) // tm * tm
    pend = jnp.cumsum(padded)
    pstart = pend - padded
    experts = jnp.arange(N_EXPERTS, dtype=jnp.int32)
    dest = pos + jnp.sum(jnp.where(idx[None] == experts[:, None, None],
                                   pstart[:, None, None], 0), axis=0)
    n_blocks = (t * TOP_K) // tm + N_EXPERTS
    n_used = (pend[-1] // tm).reshape(1)
    nblk = padded // tm

    xs = _dispatch(x1, g_ffn, dest, pstart + counts, pend, n_used, n_blocks, tm)
    act = _grouped_matmul(xs, w_gate_up, b_gate_up, pstart, nblk, n_used,
                          tm=tm, tn=2048, swiglu=True, out_dtype=BF16)
    ys = _grouped_matmul(act, w_down, b_down, pstart, nblk, n_used,
                         tm=tm, tn=2048, swiglu=False, out_dtype=F32)
    return _combine(x1, gates.T, dest, ys, g_out.reshape(1, -1), norm)


def kernel(x, g_mix, w_in, conv_w, conv_b, dt_bias, a_log, d_skip, g_ssm, w_out, g_ffn, w_router,
           b_router, w_gate_up, b_gate_up, w_down, b_down, g_final):
    b, s, d = x.shape
    depth = g_mix.shape[0]
    outs = []
    for bi in range(b):
        xb = x[bi]
        for l in range(depth):
            x1 = _mixer(xb, l, g_mix[l], w_in, conv_w[l], conv_b[l], dt_bias[l], a_log[l],
                        d_skip[l], g_ssm[l], w_out)
            xb = _moe(x1, g_ffn[l], w_router[l], b_router[l], w_gate_up[l], b_gate_up[l],
                      w_down[l], b_down[l], g_final, norm=(l == depth - 1))
        outs.append(xb)
    return outs[0][None] if b == 1 else jnp.stack(outs)
```

```python
import functools

import jax
import jax.numpy as jnp
from jax import lax
from jax.experimental import pallas as pl
from jax.experimental.pallas import tpu as pltpu

F32 = jnp.float32
BF16 = jnp.bfloat16
HIGHEST = lax.Precision.HIGHEST

LANES = 128
HEAD_DIM = 64
SB_WIDTH = 1024
SSM_INNER = 1024
SSM_GROUPS = 2
SSM_STATE = 128
SSM_CONV = 4
SSM_CONV_DIM = SSM_INNER + 2 * SSM_GROUPS * SSM_STATE
SSM_HEADS = SSM_INNER // HEAD_DIM
N_EXPERTS = 32
TOP_K = 4
SWIGLU_LIMIT = 7.0
SWIGLU_ALPHA = 1.702
RMS_EPS = 1e-5

VMEM_LIMIT = 56 * 1024 * 1024

ATTN_DEAD = -110.0

W_CHUNK_ROWS = 256


def _cparams(*sem):
    return pltpu.CompilerParams(dimension_semantics=sem, vmem_limit_bytes=VMEM_LIMIT)


def _rms(x, g):
    return x * lax.rsqrt(jnp.mean(x * x, axis=-1, keepdims=True) + RMS_EPS) * g


def _sigmoid(x):
    return 1.0 / (1.0 + jnp.exp(-x))


def _softplus(x):
    return jnp.maximum(x, 0.0) + jnp.log(1.0 + jnp.exp(-jnp.abs(x)))


def _cast_kernel(w_ref, o_ref, *tail_ref, ntail):
    w = w_ref[...]
    o_ref[...] = w.astype(o_ref.dtype)
    if ntail:
        tail = w[:, w.shape[1] - ntail:]
        tail_ref[0][...] = jnp.concatenate(
            [tail, jnp.zeros((w.shape[0], LANES - ntail), F32)], axis=1)


def _to_bf16(w, layer, rows=256, ntail=0):
    _, k, n = w.shape
    out_specs = [pl.BlockSpec((rows, n), lambda i: (i, 0))]
    out_shape = [jax.ShapeDtypeStruct((k, n), BF16)]
    if ntail:
        out_specs.append(pl.BlockSpec((rows, LANES), lambda i: (i, 0)))
        out_shape.append(jax.ShapeDtypeStruct((k, LANES), F32))
    outs = pl.pallas_call(
        functools.partial(_cast_kernel, ntail=ntail),
        grid=(k // rows,),
        in_specs=[pl.BlockSpec((None, rows, n), lambda i: (layer, i, 0))],
        out_specs=out_specs,
        out_shape=out_shape,
        compiler_params=_cparams("arbitrary"),
        name="cast_bf16",
    )(w)
    return outs if ntail else outs[0]


def _in_proj_kernel(x_ref, g_ref, w_ref, wdt_ref, q_ref, k_ref, v_ref, z_ref, xbc_ref, dt_ref):
    hb = _rms(x_ref[...], g_ref[...]).astype(BF16)

    def mm(a, b):
        return jnp.dot(hb, w_ref[:, a:b], preferred_element_type=F32)

    o = 0
    q_ref[...] = (mm(o, o + SB_WIDTH) * (HEAD_DIM ** -0.5)).astype(BF16)
    o += SB_WIDTH
    k_ref[...] = mm(o, o + SB_WIDTH).astype(BF16)
    o += SB_WIDTH
    v_ref[...] = mm(o, o + SB_WIDTH).astype(BF16)
    o += SB_WIDTH
    z_ref[...] = mm(o, o + SSM_INNER)
    o += SSM_INNER
    xbc_ref[...] = mm(o, o + SSM_CONV_DIM)
    dt_ref[...] = jnp.dot(hb, wdt_ref[...].astype(BF16), preferred_element_type=F32)


def _in_proj(x, g, w_bf, w_dt, tm=256):
    t, d = x.shape
    n = w_bf.shape[1]
    row = lambda w: pl.BlockSpec((tm, w), lambda i: (i, 0))
    outs = [(SB_WIDTH, BF16)] * 3 + [(SSM_INNER, F32), (SSM_CONV_DIM, F32), (LANES, F32)]
    return pl.pallas_call(
        _in_proj_kernel,
        grid=(t // tm,),
        in_specs=[row(d),
                  pl.BlockSpec((1, d), lambda i: (0, 0)),
                  pl.BlockSpec((d, n), lambda i: (0, 0), pipeline_mode=pl.Buffered(1)),
                  pl.BlockSpec((d, LANES), lambda i: (0, 0))],
        out_specs=[row(w) for w, _ in outs],
        out_shape=[jax.ShapeDtypeStruct((t, w), dt) for w, dt in outs],
        compiler_params=_cparams("arbitrary"),
        name="in_proj",
    )(x, g, w_bf, w_dt)


def _attn_kernel(q_ref, k_ref, v_ref, o_ref, acc_ref, carry_ref, *, tq):
    i = pl.program_id(1)
    nsub = tq // LANES
    qb0 = i * nsub
    row = lax.broadcasted_iota(jnp.int32, (LANES, LANES), 0)
    col = lax.broadcasted_iota(jnp.int32, (LANES, LANES), 1)
    past = col < row
    lo = col < HEAD_DIM
    jj = lax.broadcasted_iota(jnp.int32, (LANES, 2 * LANES), 0)
    ss = lax.broadcasted_iota(jnp.int32, (LANES, 2 * LANES), 1)
    neg_later = jnp.where((ss >= LANES) | (jj > ss), -1.0, 0.0).astype(BF16)
    nt = (((1,), (1,)), ((), ()))

    def step(n, diag):
        k0s = [pl.multiple_of(jnp.maximum(qb0 + s - n, 0) * LANES, LANES) for s in range(nsub)]
        chains = [(s, hd) for s in range(nsub) for hd in range(2)]
        logits = []
        for s, hd in chains:
            q = q_ref[s * LANES:(s + 1) * LANES, :]
            qh = jnp.where(lo if hd == 0 else jnp.logical_not(lo), q, jnp.zeros(q.shape, q.dtype))
            kt = k_ref[pl.ds(k0s[s], LANES), :]
            logits.append(lax.dot_general(qh, kt, nt, preferred_element_type=F32))
        logsig, sums = [], []
        for l in logits:
            sp = _softplus(l)
            lk = jnp.where(past, sp, 0.0) if diag else sp
            sums.append(jnp.dot(lk.astype(BF16), neg_later, preferred_element_type=F32))
            logsig.append(l - sp)
        top = None
        for c, (s, hd) in enumerate(chains):
            r = sums[c]
            if diag:
                w = jnp.where(past, jnp.exp(logsig[c] + r[:, :LANES]), 0.0)
                carry = r[:, LANES:]
            else:
                before = jnp.where(qb0 + s - n >= 0, carry_ref[c], -1e30)
                w = jnp.exp(logsig[c] + r[:, :LANES] + before)
                carry = before + r[:, LANES:]
            vt = v_ref[pl.ds(k0s[s], LANES), :]
            pv = jnp.dot(w.astype(BF16), vt, preferred_element_type=F32)
            if diag:
                acc_ref[c] = pv
            else:
                acc_ref[c] += pv
            carry_ref[c] = carry
            top = carry if top is None else jnp.maximum(top, carry)
        return jnp.max(top)

    top0 = step(0, True)

    def cond(st):
        n, top = st
        return jnp.logical_and(n <= qb0 + nsub - 1, top > ATTN_DEAD)

    def body(st):
        n, _ = st
        return n + 1, step(n, False)

    lax.while_loop(cond, body, (jnp.int32(1), top0))
    for s in range(nsub):
        o_ref[s * LANES:(s + 1) * LANES, :] = jnp.where(
            lo, acc_ref[2 * s], acc_ref[2 * s + 1]).astype(o_ref.dtype)


def _attention(q, k, v, tq=1024):
    t, w = q.shape
    tq = min(tq, t)
    chains = 2 * (tq // LANES)
    return pl.pallas_call(
        functools.partial(_attn_kernel, tq=tq),
        grid=(w // LANES, t // tq),
        in_specs=[pl.BlockSpec((tq, LANES), lambda p, i: (i, p)),
                  pl.BlockSpec((t, LANES), lambda p, i: (0, p)),
                  pl.BlockSpec((t, LANES), lambda p, i: (0, p))],
        out_specs=pl.BlockSpec((tq, LANES), lambda p, i: (i, p)),
        out_shape=jax.ShapeDtypeStruct((t, w), BF16),
        scratch_shapes=[pltpu.VMEM((chains, LANES, LANES), F32),
                        pltpu.VMEM((chains, LANES, LANES), F32)],
        compiler_params=_cparams("arbitrary", "arbitrary"),
        name="sb_attention",
    )(q, k, v)


def _ssd_kernel(z_ref, xbc_ref, dt_ref, cw_ref, cb_ref, dtb_ref, alog_ref, dsk_ref, g_ref, e_ref,
                o_ref, state_ref, xcat_ref):
    c = pl.program_id(0)
    L = z_ref.shape[0]
    half = SSM_INNER // SSM_GROUPS
    pad = 8

    @pl.when(c == 0)
    def _():
        state_ref[...] = jnp.zeros(state_ref.shape, F32)
        xcat_ref[0:pad, :] = jnp.zeros((pad, SSM_CONV_DIM), F32)

    xcat_ref[pad:pad + L, :] = xbc_ref[...]
    acc = jnp.broadcast_to(cb_ref[...], (L, SSM_CONV_DIM))
    for kk in range(SSM_CONV):
        s = pad - (SSM_CONV - 1) + kk
        acc = acc + cw_ref[kk:kk + 1, :] * xcat_ref[s:s + L, :]
    xcat_ref[0:pad, :] = xcat_ref[L:L + pad, :]
    xc = acc * _sigmoid(acc)
    xs = xc[:, :SSM_INNER]
    bm = xc[:, SSM_INNER:SSM_INNER + SSM_GROUPS * SSM_STATE]
    cm = xc[:, SSM_INNER + SSM_GROUPS * SSM_STATE:]

    dtv = _softplus(dt_ref[...] + dtb_ref[...])
    adt = dtv * (-jnp.exp(alog_ref[...]))
    ri = lax.broadcasted_iota(jnp.int32, (L, L), 0)
    ci = lax.broadcasted_iota(jnp.int32, (L, L), 1)
    causal = ri >= ci
    tri = jnp.where(causal, 1.0, 0.0).astype(F32)
    acum = jnp.dot(tri, adt, precision=HIGHEST, preferred_element_type=F32)
    expand = e_ref[...]
    acx = jnp.dot(acum, expand, precision=HIGHEST, preferred_element_type=F32)
    dtx = jnp.dot(dtv, expand, precision=HIGHEST, preferred_element_type=F32)
    alx = acx[L - 1:L, :]
    xdt = xs * dtx
    acum_t = acum.T
    lo = lax.broadcasted_iota(jnp.int32, (L, LANES), 1) < HEAD_DIM
    nt = (((1,), (1,)), ((), ()))

    xdt_b = xdt.astype(BF16)
    y_parts = []
    for g in range(SSM_GROUPS):
        cg = cm[:, g * SSM_STATE:(g + 1) * SSM_STATE].astype(BF16)
        bg = bm[:, g * SSM_STATE:(g + 1) * SSM_STATE]
        y_off = jnp.dot(cg, state_ref[:, g * half:(g + 1) * half].astype(BF16),
                        preferred_element_type=F32)
        cb = lax.dot_general(cg, bg.astype(BF16), nt, preferred_element_type=F32)
        for p in range(half // LANES):
            pair = g * (half // LANES) + p
            xp = xdt_b[:, pair * LANES:(pair + 1) * LANES]
            yh = []
            for hh in range(2):
                h = 2 * pair + hh
                seg = acum[:, h:h + 1] - acum_t[h:h + 1, :]
                dec = jnp.where(causal, jnp.exp(jnp.minimum(seg, 0.0)), 0.0)
                yh.append(jnp.dot((cb * dec).astype(BF16), xp, preferred_element_type=F32))
            y_parts.append(jnp.where(lo, yh[0], yh[1])
                           + y_off[:, p * LANES:(p + 1) * LANES]
                           * jnp.exp(acx[:, pair * LANES:(pair + 1) * LANES]))
        xd = xdt[:, g * half:(g + 1) * half] * jnp.exp(alx[:, g * half:(g + 1) * half]
                                                       - acx[:, g * half:(g + 1) * half])
        upd = jnp.dot(bg.T.astype(BF16), xd.astype(BF16), preferred_element_type=F32)
        state_ref[:, g * half:(g + 1) * half] = (
            jnp.exp(alx[:, g * half:(g + 1) * half]) * state_ref[:, g * half:(g + 1) * half] + upd)

    y = jnp.concatenate(y_parts, axis=1) + dsk_ref[...] * xs
    zz = z_ref[...]
    y = y * (zz * _sigmoid(zz))
    outs = []
    for g in range(SSM_GROUPS):
        yg = y[:, g * half:(g + 1) * half]
        outs.append(yg * lax.rsqrt(jnp.mean(yg * yg, axis=-1, keepdims=True) + RMS_EPS))
    o_ref[...] = (jnp.concatenate(outs, axis=1) * g_ref[...]).astype(o_ref.dtype)


def _ssd(z, xbc, dt, conv_w, conv_b, dt_bias, a_log, d_skip, g_ssm, chunk=128):
    t = z.shape[0]
    chunk = min(chunk, t)
    padh = lambda a: jnp.pad(a.reshape(1, -1), ((0, 0), (0, LANES - a.shape[-1])))
    heads = jnp.arange(SSM_INNER, dtype=jnp.int32) // HEAD_DIM
    expand = (jnp.arange(LANES, dtype=jnp.int32)[:, None] == heads[None, :]).astype(F32)
    row = lambda w: pl.BlockSpec((chunk, w), lambda c: (c, 0))
    full = lambda a: pl.BlockSpec(a.shape, lambda c: (0, 0))
    params = [conv_w, conv_b.reshape(1, -1), padh(dt_bias), padh(a_log),
              jnp.repeat(d_skip, HEAD_DIM).reshape(1, -1), g_ssm.reshape(1, -1), expand]
    return pl.pallas_call(
        _ssd_kernel,
        grid=(t // chunk,),
        in_specs=[row(SSM_INNER), row(SSM_CONV_DIM), row(LANES)] + [full(a) for a in params],
        out_specs=row(SSM_INNER),
        out_shape=jax.ShapeDtypeStruct((t, SSM_INNER), BF16),
        scratch_shapes=[pltpu.VMEM((SSM_STATE, SSM_INNER), F32),
                        pltpu.VMEM((chunk + 8, SSM_CONV_DIM), F32)],
        compiler_params=_cparams("arbitrary"),
        name="ssd_mixer",
    )(z, xbc, dt, *params)


def _out_proj_kernel(x_ref, a_ref, s_ref, w_ref, o_ref):
    o_ref[...] = (x_ref[...]
                  + jnp.dot(a_ref[...], w_ref[0:SB_WIDTH, :], preferred_element_type=F32)
                  + jnp.dot(s_ref[...], w_ref[SB_WIDTH:, :], preferred_element_type=F32))


def _out_proj(x, attn, ssm, w_bf, tm=512):
    t, d = x.shape
    tm = min(tm, t)
    return pl.pallas_call(
        _out_proj_kernel,
        grid=(t // tm,),
        in_specs=[pl.BlockSpec((tm, d), lambda i: (i, 0)),
                  pl.BlockSpec((tm, SB_WIDTH), lambda i: (i, 0)),
                  pl.BlockSpec((tm, SSM_INNER), lambda i: (i, 0)),
                  pl.BlockSpec(w_bf.shape, lambda i: (0, 0), pipeline_mode=pl.Buffered(1))],
        out_specs=pl.BlockSpec((tm, d), lambda i: (i, 0)),
        out_shape=jax.ShapeDtypeStruct((t, d), F32),
        compiler_params=_cparams("arbitrary"),
        name="out_proj",
    )(x, attn, ssm, w_bf)


def _router_kernel(x_ref, g_ref, wt_ref, b_ref, idx_ref, gate_ref, pos_ref, cnt_ref, base_ref):
    i = pl.program_id(0)
    tm = x_ref.shape[0]

    @pl.when(i == 0)
    def _():
        base_ref[...] = jnp.zeros(base_ref.shape, F32)

    h = _rms(x_ref[...], g_ref[...])
    logits = lax.dot_general(wt_ref[...], h, (((1,), (1,)), ((), ())), precision=HIGHEST,
                             preferred_element_type=F32) + b_ref[:, 0:1]
    eio = lax.broadcasted_iota(jnp.int32, (N_EXPERTS, tm), 0).astype(F32)
    work = logits
    vals, hots = [], []
    for k in range(TOP_K):
        m = jnp.max(work, axis=0, keepdims=True)
        idx = jnp.min(jnp.where(work == m, eio, float(N_EXPERTS)), axis=0, keepdims=True)
        hot = eio == idx
        work = jnp.where(hot, -jnp.inf, work)
        vals.append(m)
        hots.append(hot)
        idx_ref[k:k + 1, :] = idx.astype(jnp.int32)
    ex = [jnp.exp(v - vals[0]) for v in vals]
    den = ex[0] + ex[1] + ex[2] + ex[3]
    picked = jnp.zeros((N_EXPERTS, tm), F32)
    for k in range(TOP_K):
        gate_ref[k:k + 1, :] = ex[k] / den
        picked = picked + jnp.where(hots[k], 1.0, 0.0)
    ti = lax.broadcasted_iota(jnp.int32, (tm, tm), 0)
    tj = lax.broadcasted_iota(jnp.int32, (tm, tm), 1)
    before = jnp.where(ti < tj, 1.0, 0.0).astype(BF16)
    rank = jnp.dot(picked.astype(BF16), before, preferred_element_type=F32) + base_ref[:, 0:1]
    for k in range(TOP_K):
        pos = jnp.sum(jnp.where(hots[k], rank, 0.0), axis=0, keepdims=True)
        pos_ref[k:k + 1, :] = pos.astype(jnp.int32)
    base_ref[...] = base_ref[...] + jnp.sum(picked, axis=1, keepdims=True)
    cnt_ref[...] = base_ref[...].astype(jnp.int32)


def _router(x1, g, w_router, b_router, tm=512):
    t, d = x1.shape
    tm = min(tm, t)
    wt = w_router.T
    b = jnp.broadcast_to(b_router.reshape(-1, 1), (N_EXPERTS, LANES))
    tok = lambda dt: (pl.BlockSpec((TOP_K, tm), lambda i: (0, i)), jax.ShapeDtypeStruct((TOP_K, t), dt))
    specs = [tok(jnp.int32), tok(F32), tok(jnp.int32),
             (pl.BlockSpec((N_EXPERTS, LANES), lambda i: (0, 0)),
              jax.ShapeDtypeStruct((N_EXPERTS, LANES), jnp.int32))]
    return pl.pallas_call(
        _router_kernel,
        grid=(t // tm,),
        in_specs=[pl.BlockSpec((tm, d), lambda i: (i, 0)),
                  pl.BlockSpec((1, d), lambda i: (0, 0)),
                  pl.BlockSpec((N_EXPERTS, d), lambda i: (0, 0)),
                  pl.BlockSpec((N_EXPERTS, LANES), lambda i: (0, 0))],
        out_specs=[s for s, _ in specs],
        out_shape=[s for _, s in specs],
        scratch_shapes=[pltpu.VMEM((N_EXPERTS, LANES), F32)],
        compiler_params=_cparams("arbitrary"),
        name="router",
    )(x1, g, wt, b)


def _dispatch_kernel(dest_ref, lo_ref, hi_ref, nu_ref, x_ref, g_ref, xs_hbm, hbuf, zbuf, sem, zsem,
                     *, n_blocks):
    i = pl.program_id(0)
    last = pl.num_programs(0) - 1
    tm = x_ref.shape[0]
    slot = i % 2

    def row_copies(step, sl, r):
        return [pltpu.make_async_copy(hbuf.at[sl, pl.ds(r, 1), :],
                                      xs_hbm.at[pl.ds(dest_ref[k, step * tm + r], 1), :], sem.at[sl])
                for k in range(TOP_K)]

    def wait_step(step, sl):
        def body(r, c):
            for cp in row_copies(step, sl, r):
                cp.wait()
            return c
        lax.fori_loop(0, tm, body, 0)

    @pl.when(i >= 2)
    def _():
        wait_step(i - 2, slot)

    hbuf[slot] = _rms(x_ref[...], g_ref[...])

    def start(r, c):
        for k, cp in enumerate(row_copies(i, slot, r)):
            cp.start(priority=k % 2)
        return c
    lax.fori_loop(0, tm, start, 0)

    @pl.when(i == 0)
    def _():
        zbuf[...] = jnp.zeros(zbuf.shape, F32)

        def pad_row(r):
            return pltpu.make_async_copy(zbuf.at[pl.ds(0, 1), :], xs_hbm.at[pl.ds(r, 1), :], zsem.at[0])

        def tail_block(b):
            return pltpu.make_async_copy(zbuf, xs_hbm.at[pl.ds(pl.multiple_of(b * tm, tm), tm), :],
                                         zsem.at[0])

        def each(fn):
            def per_expert(e, c):
                return lax.fori_loop(lo_ref[e], hi_ref[e], lambda r, c2: (fn(pad_row(r)), c2)[1], c)
            lax.fori_loop(0, N_EXPERTS, per_expert, 0)
            lax.fori_loop(nu_ref[0], n_blocks, lambda b, c: (fn(tail_block(b)), c)[1], 0)

        each(lambda cp: cp.start())
        each(lambda cp: cp.wait())

    @pl.when(i == last)
    def _():
        @pl.when(i >= 1)
        def _():
            wait_step(i - 1, 1 - slot)
        wait_step(i, slot)


def _dispatch(x1, g, dest, pad_lo, pad_hi, n_used, n_blocks, tm):
    t, d = x1.shape
    return pl.pallas_call(
        functools.partial(_dispatch_kernel, n_blocks=n_blocks),
        grid_spec=pltpu.PrefetchScalarGridSpec(
            num_scalar_prefetch=4,
            grid=(t // tm,),
            in_specs=[pl.BlockSpec((tm, d), lambda i, *_: (i, 0)),
                      pl.BlockSpec((1, d), lambda i, *_: (0, 0))],
            out_specs=pl.BlockSpec(memory_space=pl.ANY),
            scratch_shapes=[pltpu.VMEM((2, tm, d), F32), pltpu.VMEM((tm, d), F32),
                            pltpu.SemaphoreType.DMA((2,)), pltpu.SemaphoreType.DMA((1,))]),
        out_shape=jax.ShapeDtypeStruct((n_blocks * tm, d), F32),
        compiler_params=_cparams("arbitrary"),
        name="moe_dispatch",
    )(dest, pad_lo, pad_hi, n_used, x1, g)


def _swiglu_pick(y, pick):
    tm, n2 = y.shape
    even = lax.broadcasted_iota(jnp.int32, (tm, 2 * LANES), 1) % 2 == 0
    outs = []
    for cblk in range(n2 // (2 * LANES)):
        yc = y[:, cblk * 2 * LANES:(cblk + 1) * 2 * LANES]
        capped = jnp.minimum(yc, SWIGLU_LIMIT)
        gate = capped * _sigmoid(SWIGLU_ALPHA * capped)
        up = jnp.maximum(capped, -SWIGLU_LIMIT) + 1.0
        factors = jnp.where(even, gate, up).astype(BF16)
        split = jnp.dot(factors, pick, preferred_element_type=F32)
        outs.append(split[:, :LANES] * split[:, LANES:])
    return jnp.concatenate(outs, axis=1)


def _gmm_kernel(ps_ref, nb_ref, nu_ref, b_ref, *rest, swiglu, tm, n_blocks):
    if swiglu:
        pick_ref, w_hbm, x_hbm, o_hbm, wbuf, wbf, xbuf, obuf, zbuf, wsem, sin, sout, zsem = rest
    else:
        w_hbm, x_hbm, o_hbm, wbuf, wbf, xbuf, obuf, zbuf, wsem, sin, sout, zsem = rest
    e = pl.program_id(0)
    j = pl.program_id(1)
    n_exp = pl.num_programs(0)
    nj = pl.num_programs(1)
    step = e * nj + j
    have_next = step < n_exp * nj - 1
    nb = nb_ref[e]
    g0 = nj * (ps_ref[e] // tm) + j * nb
    e_next = jnp.minimum(jnp.where(j + 1 < nj, e, e + 1), n_exp - 1)
    e_prev = jnp.maximum(jnp.where(j > 0, e, e - 1), 0)
    next_has = jnp.logical_and(have_next, nb_ref[e_next] > 0)
    first_here = jnp.logical_or(step == 0, nb_ref[e_prev] == 0)
    kdim, tn = wbf.shape
    tn_out = obuf.shape[2]
    col = pl.multiple_of(j * tn_out, tn_out)
    nch = kdim // W_CHUNK_ROWS

    def w_chunk(st, c):
        r = pl.ds(pl.multiple_of(c * W_CHUNK_ROWS, W_CHUNK_ROWS), W_CHUNK_ROWS)
        cols = pl.ds(pl.multiple_of((st % nj) * tn, tn), tn)
        return pltpu.make_async_copy(w_hbm.at[st // nj, r, cols], wbuf.at[r, :], wsem.at[0])

    def start_chunks(st, lo, hi):
        lax.fori_loop(lo, hi, lambda c, x: (w_chunk(st, c).start(), x)[1], 0)

    def x_copy(row0, sl):
        return pltpu.make_async_copy(x_hbm.at[pl.ds(pl.multiple_of(row0, tm), tm), :], xbuf.at[sl],
                                     sin.at[sl])

    def o_copy(b, sl):
        r = pl.ds(pl.multiple_of(ps_ref[e] + b * tm, tm), tm)
        return pltpu.make_async_copy(obuf.at[sl], o_hbm.at[r, pl.ds(col, tn_out)], sout.at[sl])

    @pl.when(step == 0)
    def _():
        start_chunks(step, 0, nch)

    @pl.when(jnp.logical_and(nb > 0, first_here))
    def _():
        x_copy(ps_ref[e], g0 % 2).start()

    lax.fori_loop(0, nch, lambda c, x: (w_chunk(step, c).wait(), x)[1], 0)

    @pl.when(nb > 0)
    def _():
        wbf[...] = wbuf[...].astype(BF16)

    per_block = (nch + jnp.maximum(nb, 1) - 1) // jnp.maximum(nb, 1)

    def block(b, c):
        sl = (g0 + b) % 2
        more = b + 1 < nb

        @pl.when(jnp.logical_or(more, next_has))
        def _():
            x_copy(jnp.where(more, ps_ref[e] + (b + 1) * tm, ps_ref[e_next]), 1 - sl).start()

        @pl.when(have_next)
        def _():
            start_chunks(step + 1, jnp.minimum(b * per_block, nch),
                         jnp.minimum((b + 1) * per_block, nch))

        x_copy(ps_ref[e] + b * tm, sl).wait()

        @pl.when(g0 + b >= 2)
        def _():
            o_copy(b, sl).wait()

        y = jnp.dot(xbuf[sl].astype(BF16), wbf[...], preferred_element_type=F32) + b_ref[...]
        if swiglu:
            y = _swiglu_pick(y, pick_ref[...])
        obuf[sl] = y.astype(obuf.dtype)
        o_copy(b, sl).start()
        return c

    lax.fori_loop(0, nb, block, 0)

    @pl.when(jnp.logical_and(have_next, nb == 0))
    def _():
        start_chunks(step + 1, 0, nch)

    @pl.when(e == n_exp - 1)
    def _():
        zbuf[...] = jnp.zeros(zbuf.shape, zbuf.dtype)

        def tail(b):
            return pltpu.make_async_copy(
                zbuf, o_hbm.at[pl.ds(pl.multiple_of(b * tm, tm), tm), pl.ds(col, tn_out)], zsem.at[0])

        lax.fori_loop(nu_ref[0], n_blocks, lambda b, c: (tail(b).start(), c)[1], 0)
        lax.fori_loop(nu_ref[0], n_blocks, lambda b, c: (tail(b).wait(), c)[1], 0)

    @pl.when(jnp.logical_not(have_next))
    def _():
        done = g0 + nb

        @pl.when(done >= 2)
        def _():
            o_copy(0, done % 2).wait()

        @pl.when(done >= 1)
        def _():
            o_copy(0, (done + 1) % 2).wait()


def _grouped_matmul(x, w, b, pstart, nblk, n_used, *, tm, tn, swiglu, out_dtype):
    n_slots, kdim = x.shape
    n_exp, _, n = w.shape
    n_out = n // 2 if swiglu else n
    tn_out = tn // 2 if swiglu else tn
    in_specs = [pl.BlockSpec((None, 1, tn), lambda e, j, *_: (e, 0, j))]
    args = [b.reshape(n_exp, 1, n)]
    if swiglu:
        src = jnp.arange(2 * LANES, dtype=jnp.int32)[:, None]
        dst = jnp.arange(2 * LANES, dtype=jnp.int32)[None, :]
        args.append((src == 2 * (dst % LANES) + dst // LANES).astype(BF16))
        in_specs.append(pl.BlockSpec((2 * LANES, 2 * LANES), lambda e, j, *_: (0, 0)))
    in_specs += [pl.BlockSpec(memory_space=pl.ANY)] * 2
    args += [w, x]
    return pl.pallas_call(
        functools.partial(_gmm_kernel, swiglu=swiglu, tm=tm, n_blocks=n_slots // tm),
        grid_spec=pltpu.PrefetchScalarGridSpec(
            num_scalar_prefetch=3,
            grid=(n_exp, n // tn),
            in_specs=in_specs,
            out_specs=pl.BlockSpec(memory_space=pl.ANY),
            scratch_shapes=[pltpu.VMEM((kdim, tn), F32), pltpu.VMEM((kdim, tn), BF16),
                            pltpu.VMEM((2, tm, kdim), x.dtype), pltpu.VMEM((2, tm, tn_out), out_dtype),
                            pltpu.VMEM((tm, tn_out), out_dtype),
                            pltpu.SemaphoreType.DMA((1,)), pltpu.SemaphoreType.DMA((2,)),
                            pltpu.SemaphoreType.DMA((2,)), pltpu.SemaphoreType.DMA((1,))]),
        out_shape=jax.ShapeDtypeStruct((n_slots, n_out), out_dtype),
        compiler_params=_cparams("arbitrary", "arbitrary"),
        name="moe_gate_up" if swiglu else "moe_down",
    )(pstart, nblk, n_used, *args)


def _row_copy(src_hbm, row, dst, r, sem):
    return pltpu.make_async_copy(src_hbm.at[pl.ds(row, 1), :], dst.at[pl.ds(r, 1), :], sem)


def _combine_kernel(dest_ref, x_ref, gate_ref, g_ref, ys_hbm, o_ref, buf_ref, sem, *, norm):
    tc = x_ref.shape[0]
    t0 = pl.program_id(0) * tc

    def start(r, c):
        for k in range(TOP_K):
            _row_copy(ys_hbm, dest_ref[k, t0 + r], buf_ref.at[k], r, sem.at[0]).start(priority=k % 2)
        return c

    def wait(r, c):
        for k in range(TOP_K):
            _row_copy(ys_hbm, dest_ref[k, t0 + r], buf_ref.at[k], r, sem.at[0]).wait()
        return c

    lax.fori_loop(0, tc, start, 0)
    lax.fori_loop(0, tc, wait, 0)
    y = x_ref[...]
    for k in range(TOP_K):
        y = y + gate_ref[:, k:k + 1] * buf_ref[k]
    o_ref[...] = _rms(y, g_ref[...]) if norm else y


def _combine(x1, gates_t, dest, ys, g_final, norm, tc=128):
    t, d = x1.shape
    tc = min(tc, t)
    return pl.pallas_call(
        functools.partial(_combine_kernel, norm=norm),
        grid_spec=pltpu.PrefetchScalarGridSpec(
            num_scalar_prefetch=1,
            grid=(t // tc,),
            in_specs=[pl.BlockSpec((tc, d), lambda i, dest: (i, 0)),
                      pl.BlockSpec((tc, TOP_K), lambda i, dest: (i, 0)),
                      pl.BlockSpec((1, d), lambda i, dest: (0, 0)),
                      pl.BlockSpec(memory_space=pl.ANY)],
            out_specs=pl.BlockSpec((tc, d), lambda i, dest: (i, 0)),
            scratch_shapes=[pltpu.VMEM((TOP_K, tc, d), F32), pltpu.SemaphoreType.DMA((1,))]),
        out_shape=jax.ShapeDtypeStruct((t, d), F32),
        compiler_params=_cparams("arbitrary"),
        name="moe_combine",
    )(dest, x1, gates_t, g_final, ys)


def _mixer(x, layer, g_mix, w_in, conv_w, conv_b, dt_bias, a_log, d_skip, g_ssm, w_out):
    w_bf, w_dt = _to_bf16(w_in, layer, ntail=SSM_HEADS)
    q, k, v, z, xbc, dt = _in_proj(x, g_mix.reshape(1, -1), w_bf, w_dt)
    attn = _attention(q, k, v)
    ssm = _ssd(z, xbc, dt, conv_w, conv_b, dt_bias, a_log, d_skip, g_ssm)
    return _out_proj(x, attn, ssm, _to_bf16(w_out, layer))


def _moe(x1, g_ffn, w_router, b_router, w_gate_up, b_gate_up, w_down, b_down, g_out, norm, tm=256):
    t, d = x1.shape
    g_ffn = g_ffn.reshape(1, -1)
    idx, gates, pos, cnt = _router(x1, g_ffn, w_router, b_router)
    counts = cnt[:, 0]
    padded = (counts + tm - 1) // tm * tm
    pend = jnp.cumsum(padded)
    pstart = pend - padded
    experts = jnp.arange(N_EXPERTS, dtype=jnp.int32)
    dest = pos + jnp.sum(jnp.where(idx[None] == experts[:, None, None],
                                   pstart[:, None, None], 0), axis=0)
    n_blocks = (t * TOP_K) // tm + N_EXPERTS
    n_used = (pend[-1] // tm).reshape(1)
    nblk = padded // tm

    xs = _dispatch(x1, g_ffn, dest, pstart + counts, pend, n_used, n_blocks, tm)
    act = _grouped_matmul(xs, w_gate_up, b_gate_up, pstart, nblk, n_used,
                          tm=tm, tn=2048, swiglu=True, out_dtype=BF16)
    ys = _grouped_matmul(act, w_down, b_down, pstart, nblk, n_used,
                         tm=tm, tn=2048, swiglu=False, out_dtype=F32)
    return _combine(x1, gates.T, dest, ys, g_out.reshape(1, -1), norm)


def kernel(x, g_mix, w_in, conv_w, conv_b, dt_bias, a_log, d_skip, g_ssm, w_out, g_ffn, w_router,
           b_router, w_gate_up, b_gate_up, w_down, b_down, g_final):
    b, s, d = x.shape
    depth = g_mix.shape[0]
    outs = []
    for bi in range(b):
        xb = x[bi]
        for l in range(depth):
            x1 = _mixer(xb, l, g_mix[l], w_in, conv_w[l], conv_b[l], dt_bias[l], a_log[l],
                        d_skip[l], g_ssm[l], w_out)
            xb = _moe(x1, g_ffn[l], w_router[l], b_router[l], w_gate_up[l], b_gate_up[l],
                      w_down[l], b_down[l], g_final, norm=(l == depth - 1))
        outs.append(xb)
    return outs[0][None] if b == 1 else jnp.stack(outs)
```

```python
import functools

import jax
import jax.numpy as jnp
from jax import lax
from jax.experimental import pallas as pl
from jax.experimental.pallas import tpu as pltpu

F32 = jnp.float32
BF16 = jnp.bfloat16
HIGHEST = lax.Precision.HIGHEST

LANES = 128
HEAD_DIM = 64
SB_WIDTH = 1024
SSM_INNER = 1024
SSM_GROUPS = 2
SSM_STATE = 128
SSM_CONV = 4
SSM_CONV_DIM = SSM_INNER + 2 * SSM_GROUPS * SSM_STATE
SSM_HEADS = SSM_INNER // HEAD_DIM
N_EXPERTS = 32
TOP_K = 4
SWIGLU_LIMIT = 7.0
SWIGLU_ALPHA = 1.702
RMS_EPS = 1e-5

VMEM_LIMIT = 56 * 1024 * 1024

ATTN_DEAD = -110.0

W_CHUNK_ROWS = 256


def _cparams(*sem):
    return pltpu.CompilerParams(dimension_semantics=sem, vmem_limit_bytes=VMEM_LIMIT)


def _rms(x, g):
    return x * lax.rsqrt(jnp.mean(x * x, axis=-1, keepdims=True) + RMS_EPS) * g


def _sigmoid(x):
    return 1.0 / (1.0 + jnp.exp(-x))


def _softplus(x):
    return jnp.maximum(x, 0.0) + jnp.log(1.0 + jnp.exp(-jnp.abs(x)))


def _cast_kernel(w_ref, o_ref, *tail_ref, ntail):
    w = w_ref[...]
    o_ref[...] = w.astype(o_ref.dtype)
    if ntail:
        tail = w[:, w.shape[1] - ntail:]
        tail_ref[0][...] = jnp.concatenate(
            [tail, jnp.zeros((w.shape[0], LANES - ntail), F32)], axis=1)


def _to_bf16(w, layer, rows=256, ntail=0):
    _, k, n = w.shape
    out_specs = [pl.BlockSpec((rows, n), lambda i: (i, 0))]
    out_shape = [jax.ShapeDtypeStruct((k, n), BF16)]
    if ntail:
        out_specs.append(pl.BlockSpec((rows, LANES), lambda i: (i, 0)))
        out_shape.append(jax.ShapeDtypeStruct((k, LANES), F32))
    outs = pl.pallas_call(
        functools.partial(_cast_kernel, ntail=ntail),
        grid=(k // rows,),
        in_specs=[pl.BlockSpec((None, rows, n), lambda i: (layer, i, 0))],
        out_specs=out_specs,
        out_shape=out_shape,
        compiler_params=_cparams("arbitrary"),
        name="cast_bf16",
    )(w)
    return outs if ntail else outs[0]


def _in_proj_kernel(x_ref, g_ref, w_ref, wdt_ref, q_ref, k_ref, v_ref, z_ref, xbc_ref, dt_ref):
    hb = _rms(x_ref[...], g_ref[...]).astype(BF16)

    def mm(a, b):
        return jnp.dot(hb, w_ref[:, a:b], preferred_element_type=F32)

    o = 0
    q_ref[...] = (mm(o, o + SB_WIDTH) * (HEAD_DIM ** -0.5)).astype(BF16)
    o += SB_WIDTH
    k_ref[...] = mm(o, o + SB_WIDTH).astype(BF16)
    o += SB_WIDTH
    v_ref[...] = mm(o, o + SB_WIDTH).astype(BF16)
    o += SB_WIDTH
    z_ref[...] = mm(o, o + SSM_INNER)
    o += SSM_INNER
    xbc_ref[...] = mm(o, o + SSM_CONV_DIM)
    dt_ref[...] = jnp.dot(hb, wdt_ref[...].astype(BF16), preferred_element_type=F32)


def _in_proj(x, g, w_bf, w_dt, tm=256):
    t, d = x.shape
    n = w_bf.shape[1]
    row = lambda w: pl.BlockSpec((tm, w), lambda i: (i, 0))
    outs = [(SB_WIDTH, BF16)] * 3 + [(SSM_INNER, F32), (SSM_CONV_DIM, F32), (LANES, F32)]
    return pl.pallas_call(
        _in_proj_kernel,
        grid=(t // tm,),
        in_specs=[row(d),
                  pl.BlockSpec((1, d), lambda i: (0, 0)),
                  pl.BlockSpec((d, n), lambda i: (0, 0), pipeline_mode=pl.Buffered(1)),
                  pl.BlockSpec((d, LANES), lambda i: (0, 0))],
        out_specs=[row(w) for w, _ in outs],
        out_shape=[jax.ShapeDtypeStruct((t, w), dt) for w, dt in outs],
        compiler_params=_cparams("arbitrary"),
        name="in_proj",
    )(x, g, w_bf, w_dt)


def _attn_kernel(q_ref, k_ref, v_ref, o_ref, acc_ref, carry_ref, *, tq):
    i = pl.program_id(1)
    nsub = tq // LANES
    qb0 = i * nsub
    row = lax.broadcasted_iota(jnp.int32, (LANES, LANES), 0)
    col = lax.broadcasted_iota(jnp.int32, (LANES, LANES), 1)
    past = col < row
    lo = col < HEAD_DIM
    jj = lax.broadcasted_iota(jnp.int32, (LANES, 2 * LANES), 0)
    ss = lax.broadcasted_iota(jnp.int32, (LANES, 2 * LANES), 1)
    neg_later = jnp.where((ss >= LANES) | (jj > ss), -1.0, 0.0).astype(BF16)
    nt = (((1,), (1,)), ((), ()))

    def step(n, diag):
        k0s = [pl.multiple_of(jnp.maximum(qb0 + s - n, 0) * LANES, LANES) for s in range(nsub)]
        chains = [(s, hd) for s in range(nsub) for hd in range(2)]
        logits = []
        for s, hd in chains:
            q = q_ref[s * LANES:(s + 1) * LANES, :]
            qh = jnp.where(lo if hd == 0 else jnp.logical_not(lo), q, jnp.zeros(q.shape, q.dtype))
            kt = k_ref[pl.ds(k0s[s], LANES), :]
            logits.append(lax.dot_general(qh, kt, nt, preferred_element_type=F32))
        logsig, sums = [], []
        for l in logits:
            sp = _softplus(l)
            lk = jnp.where(past, sp, 0.0) if diag else sp
            sums.append(jnp.dot(lk.astype(BF16), neg_later, preferred_element_type=F32))
            logsig.append(l - sp)
        top = None
        for c, (s, hd) in enumerate(chains):
            r = sums[c]
            if diag:
                w = jnp.where(past, jnp.exp(logsig[c] + r[:, :LANES]), 0.0)
                carry = r[:, LANES:]
            else:
                before = jnp.where(qb0 + s - n >= 0, carry_ref[c], -1e30)
                w = jnp.exp(logsig[c] + r[:, :LANES] + before)
                carry = before + r[:, LANES:]
            vt = v_ref[pl.ds(k0s[s], LANES), :]
            pv = jnp.dot(w.astype(BF16), vt, preferred_element_type=F32)
            if diag:
                acc_ref[c] = pv
            else:
                acc_ref[c] += pv
            carry_ref[c] = carry
            top = carry if top is None else jnp.maximum(top, carry)
        return jnp.max(top)

    top0 = step(0, True)

    def cond(st):
        n, top = st
        return jnp.logical_and(n <= qb0 + nsub - 1, top > ATTN_DEAD)

    def body(st):
        n, _ = st
        return n + 1, step(n, False)

    lax.while_loop(cond, body, (jnp.int32(1), top0))
    for s in range(nsub):
        o_ref[s * LANES:(s + 1) * LANES, :] = jnp.where(
            lo, acc_ref[2 * s], acc_ref[2 * s + 1]).astype(o_ref.dtype)


def _attention(q, k, v, tq=1024):
    t, w = q.shape
    tq = min(tq, t)
    chains = 2 * (tq // LANES)
    return pl.pallas_call(
        functools.partial(_attn_kernel, tq=tq),
        grid=(w // LANES, t // tq),
        in_specs=[pl.BlockSpec((tq, LANES), lambda p, i: (i, p)),
                  pl.BlockSpec((t, LANES), lambda p, i: (0, p)),
                  pl.BlockSpec((t, LANES), lambda p, i: (0, p))],
        out_specs=pl.BlockSpec((tq, LANES), lambda p, i: (i, p)),
        out_shape=jax.ShapeDtypeStruct((t, w), BF16),
        scratch_shapes=[pltpu.VMEM((chains, LANES, LANES), F32),
                        pltpu.VMEM((chains, LANES, LANES), F32)],
        compiler_params=_cparams("arbitrary", "arbitrary"),
        name="sb_attention",
    )(q, k, v)


def _ssd_kernel(z_ref, xbc_ref, dt_ref, cw_ref, cb_ref, dtb_ref, alog_ref, dsk_ref, g_ref, e_ref,
                o_ref, state_ref, xcat_ref):
    c = pl.program_id(0)
    L = z_ref.shape[0]
    half = SSM_INNER // SSM_GROUPS
    pad = 8

    @pl.when(c == 0)
    def _():
        state_ref[...] = jnp.zeros(state_ref.shape, F32)
        xcat_ref[0:pad, :] = jnp.zeros((pad, SSM_CONV_DIM), F32)

    xcat_ref[pad:pad + L, :] = xbc_ref[...]
    acc = jnp.broadcast_to(cb_ref[...], (L, SSM_CONV_DIM))
    for kk in range(SSM_CONV):
        s = pad - (SSM_CONV - 1) + kk
        acc = acc + cw_ref[kk:kk + 1, :] * xcat_ref[s:s + L, :]
    xcat_ref[0:pad, :] = xcat_ref[L:L + pad, :]
    xc = acc * _sigmoid(acc)
    xs = xc[:, :SSM_INNER]
    bm = xc[:, SSM_INNER:SSM_INNER + SSM_GROUPS * SSM_STATE]
    cm = xc[:, SSM_INNER + SSM_GROUPS * SSM_STATE:]

    dtv = _softplus(dt_ref[...] + dtb_ref[...])
    adt = dtv * (-jnp.exp(alog_ref[...]))
    ri = lax.broadcasted_iota(jnp.int32, (L, L), 0)
    ci = lax.broadcasted_iota(jnp.int32, (L, L), 1)
    causal = ri >= ci
    tri = jnp.where(causal, 1.0, 0.0).astype(F32)
    acum = jnp.dot(tri, adt, precision=HIGHEST, preferred_element_type=F32)
    expand = e_ref[...]
    acx = jnp.dot(acum, expand, precision=HIGHEST, preferred_element_type=F32)
    dtx = jnp.dot(dtv, expand, precision=HIGHEST, preferred_element_type=F32)
    alx = acx[L - 1:L, :]
    xdt = xs * dtx
    acum_t = acum.T
    lo = lax.broadcasted_iota(jnp.int32, (L, LANES), 1) < HEAD_DIM
    nt = (((1,), (1,)), ((), ()))

    xdt_b = xdt.astype(BF16)
    y_parts = []
    for g in range(SSM_GROUPS):
        cg = cm[:, g * SSM_STATE:(g + 1) * SSM_STATE].astype(BF16)
        bg = bm[:, g * SSM_STATE:(g + 1) * SSM_STATE]
        y_off = jnp.dot(cg, state_ref[:, g * half:(g + 1) * half].astype(BF16),
                        preferred_element_type=F32)
        cb = lax.dot_general(cg, bg.astype(BF16), nt, preferred_element_type=F32)
        for p in range(half // LANES):
            pair = g * (half // LANES) + p
            xp = xdt_b[:, pair * LANES:(pair + 1) * LANES]
            yh = []
            for hh in range(2):
                h = 2 * pair + hh
                seg = acum[:, h:h + 1] - acum_t[h:h + 1, :]
                dec = jnp.where(causal, jnp.exp(jnp.minimum(seg, 0.0)), 0.0)
                yh.append(jnp.dot((cb * dec).astype(BF16), xp, preferred_element_type=F32))
            y_parts.append(jnp.where(lo, yh[0], yh[1])
                           + y_off[:, p * LANES:(p + 1) * LANES]
                           * jnp.exp(acx[:, pair * LANES:(pair + 1) * LANES]))
        xd = xdt[:, g * half:(g + 1) * half] * jnp.exp(alx[:, g * half:(g + 1) * half]
                                                       - acx[:, g * half:(g + 1) * half])
        upd = jnp.dot(bg.T.astype(BF16), xd.astype(BF16), preferred_element_type=F32)
        state_ref[:, g * half:(g + 1) * half] = (
            jnp.exp(alx[:, g * half:(g + 1) * half]) * state_ref[:, g * half:(g + 1) * half] + upd)

    y = jnp.concatenate(y_parts, axis=1) + dsk_ref[...] * xs
    zz = z_ref[...]
    y = y * (zz * _sigmoid(zz))
    outs = []
    for g in range(SSM_GROUPS):
        yg = y[:, g * half:(g + 1) * half]
        outs.append(yg * lax.rsqrt(jnp.mean(yg * yg, axis=-1, keepdims=True) + RMS_EPS))
    o_ref[...] = (jnp.concatenate(outs, axis=1) * g_ref[...]).astype(o_ref.dtype)


def _ssd(z, xbc, dt, conv_w, conv_b, dt_bias, a_log, d_skip, g_ssm, chunk=128):
    t = z.shape[0]
    chunk = min(chunk, t)
    padh = lambda a: jnp.pad(a.reshape(1, -1), ((0, 0), (0, LANES - a.shape[-1])))
    heads = jnp.arange(SSM_INNER, dtype=jnp.int32) // HEAD_DIM
    expand = (jnp.arange(LANES, dtype=jnp.int32)[:, None] == heads[None, :]).astype(F32)
    row = lambda w: pl.BlockSpec((chunk, w), lambda c: (c, 0))
    full = lambda a: pl.BlockSpec(a.shape, lambda c: (0, 0))
    params = [conv_w, conv_b.reshape(1, -1), padh(dt_bias), padh(a_log),
              jnp.repeat(d_skip, HEAD_DIM).reshape(1, -1), g_ssm.reshape(1, -1), expand]
    return pl.pallas_call(
        _ssd_kernel,
        grid=(t // chunk,),
        in_specs=[row(SSM_INNER), row(SSM_CONV_DIM), row(LANES)] + [full(a) for a in params],
        out_specs=row(SSM_INNER),
        out_shape=jax.ShapeDtypeStruct((t, SSM_INNER), BF16),
        scratch_shapes=[pltpu.VMEM((SSM_STATE, SSM_INNER), F32),
                        pltpu.VMEM((chunk + 8, SSM_CONV_DIM), F32)],
        compiler_params=_cparams("arbitrary"),
        name="ssd_mixer",
    )(z, xbc, dt, *params)


def _out_proj_kernel(x_ref, a_ref, s_ref, w_ref, o_ref):
    o_ref[...] = (x_ref[...]
                  + jnp.dot(a_ref[...], w_ref[0:SB_WIDTH, :], preferred_element_type=F32)
                  + jnp.dot(s_ref[...], w_ref[SB_WIDTH:, :], preferred_element_type=F32))


def _out_proj(x, attn, ssm, w_bf, tm=512):
    t, d = x.shape
    tm = min(tm, t)
    return pl.pallas_call(
        _out_proj_kernel,
        grid=(t // tm,),
        in_specs=[pl.BlockSpec((tm, d), lambda i: (i, 0)),
                  pl.BlockSpec((tm, SB_WIDTH), lambda i: (i, 0)),
                  pl.BlockSpec((tm, SSM_INNER), lambda i: (i, 0)),
                  pl.BlockSpec(w_bf.shape, lambda i: (0, 0), pipeline_mode=pl.Buffered(1))],
        out_specs=pl.BlockSpec((tm, d), lambda i: (i, 0)),
        out_shape=jax.ShapeDtypeStruct((t, d), F32),
        compiler_params=_cparams("arbitrary"),
        name="out_proj",
    )(x, attn, ssm, w_bf)


def _router_kernel(x_ref, g_ref, wt_ref, b_ref, idx_ref, gate_ref, pos_ref, cnt_ref, base_ref):
    i = pl.program_id(0)
    tm = x_ref.shape[0]

    @pl.when(i == 0)
    def _():
        base_ref[...] = jnp.zeros(base_ref.shape, F32)

    h = _rms(x_ref[...], g_ref[...])
    logits = lax.dot_general(wt_ref[...], h, (((1,), (1,)), ((), ())), precision=HIGHEST,
                             preferred_element_type=F32) + b_ref[:, 0:1]
    eio = lax.broadcasted_iota(jnp.int32, (N_EXPERTS, tm), 0).astype(F32)
    work = logits
    vals, hots = [], []
    for k in range(TOP_K):
        m = jnp.max(work, axis=0, keepdims=True)
        idx = jnp.min(jnp.where(work == m, eio, float(N_EXPERTS)), axis=0, keepdims=True)
        hot = eio == idx
        work = jnp.where(hot, -jnp.inf, work)
        vals.append(m)
        hots.append(hot)
        idx_ref[k:k + 1, :] = idx.astype(jnp.int32)
    ex = [jnp.exp(v - vals[0]) for v in vals]
    den = ex[0] + ex[1] + ex[2] + ex[3]
    picked = jnp.zeros((N_EXPERTS, tm), F32)
    for k in range(TOP_K):
        gate_ref[k:k + 1, :] = ex[k] / den
        picked = picked + jnp.where(hots[k], 1.0, 0.0)
    ti = lax.broadcasted_iota(jnp.int32, (tm, tm), 0)
    tj = lax.broadcasted_iota(jnp.int32, (tm, tm), 1)
    before = jnp.where(ti < tj, 1.0, 0.0).astype(BF16)
    rank = jnp.dot(picked.astype(BF16), before, preferred_element_type=F32) + base_ref[:, 0:1]
    for k in range(TOP_K):
        pos = jnp.sum(jnp.where(hots[k], rank, 0.0), axis=0, keepdims=True)
        pos_ref[k:k + 1, :] = pos.astype(jnp.int32)
    base_ref[...] = base_ref[...] + jnp.sum(picked, axis=1, keepdims=True)
    cnt_ref[...] = base_ref[...].astype(jnp.int32)


def _router(x1, g, w_router, b_router, tm=512):
    t, d = x1.shape
    tm = min(tm, t)
    wt = w_router.T
    b = jnp.broadcast_to(b_router.reshape(-1, 1), (N_EXPERTS, LANES))
    tok = lambda dt: (pl.BlockSpec((TOP_K, tm), lambda i: (0, i)), jax.ShapeDtypeStruct((TOP_K, t), dt))
    specs = [tok(jnp.int32), tok(F32), tok(jnp.int32),
             (pl.BlockSpec((N_EXPERTS, LANES), lambda i: (0, 0)),
              jax.ShapeDtypeStruct((N_EXPERTS, LANES), jnp.int32))]
    return pl.pallas_call(
        _router_kernel,
        grid=(t // tm,),
        in_specs=[pl.BlockSpec((tm, d), lambda i: (i, 0)),
                  pl.BlockSpec((1, d), lambda i: (0, 0)),
                  pl.BlockSpec((N_EXPERTS, d), lambda i: (0, 0)),
                  pl.BlockSpec((N_EXPERTS, LANES), lambda i: (0, 0))],
        out_specs=[s for s, _ in specs],
        out_shape=[s for _, s in specs],
        scratch_shapes=[pltpu.VMEM((N_EXPERTS, LANES), F32)],
        compiler_params=_cparams("arbitrary"),
        name="router",
    )(x1, g, wt, b)


def _dispatch_kernel(dest_ref, lo_ref, hi_ref, nu_ref, x_ref, g_ref, xs_hbm, hbuf, zbuf, sem, zsem,
                     *, n_blocks):
    i = pl.program_id(0)
    last = pl.num_programs(0) - 1
    tm = x_ref.shape[0]
    slot = i % 2

    def row_copies(step, sl, r):
        return [pltpu.make_async_copy(hbuf.at[sl, pl.ds(r, 1), :],
                                      xs_hbm.at[pl.ds(dest_ref[k, step * tm + r], 1), :], sem.at[sl])
                for k in range(TOP_K)]

    def wait_step(sl):
        for _ in range(TOP_K):
            pltpu.make_async_copy(hbuf.at[sl], xs_hbm.at[pl.ds(0, tm), :], sem.at[sl]).wait()

    @pl.when(i >= 2)
    def _():
        wait_step(slot)

    hbuf[slot] = _rms(x_ref[...], g_ref[...])

    def start(r, c):
        for k, cp in enumerate(row_copies(i, slot, r)):
            cp.start(priority=k % 2)
        return c
    lax.fori_loop(0, tm, start, 0)

    @pl.when(i == 0)
    def _():
        zbuf[...] = jnp.zeros(zbuf.shape, F32)

        def pad_row(r):
            return pltpu.make_async_copy(zbuf.at[pl.ds(0, 1), :], xs_hbm.at[pl.ds(r, 1), :], zsem.at[0])

        def tail_block(b):
            return pltpu.make_async_copy(zbuf, xs_hbm.at[pl.ds(pl.multiple_of(b * tm, tm), tm), :],
                                         zsem.at[0])

        def each(fn):
            def per_expert(e, c):
                return lax.fori_loop(lo_ref[e], hi_ref[e], lambda r, c2: (fn(pad_row(r)), c2)[1], c)
            lax.fori_loop(0, N_EXPERTS, per_expert, 0)
            lax.fori_loop(nu_ref[0], n_blocks, lambda b, c: (fn(tail_block(b)), c)[1], 0)

        each(lambda cp: cp.start())
        each(lambda cp: cp.wait())

    @pl.when(i == last)
    def _():
        @pl.when(i >= 1)
        def _():
            wait_step(1 - slot)
        wait_step(slot)


def _dispatch(x1, g, dest, pad_lo, pad_hi, n_used, n_blocks, tm):
    t, d = x1.shape
    return pl.pallas_call(
        functools.partial(_dispatch_kernel, n_blocks=n_blocks),
        grid_spec=pltpu.PrefetchScalarGridSpec(
            num_scalar_prefetch=4,
            grid=(t // tm,),
            in_specs=[pl.BlockSpec((tm, d), lambda i, *_: (i, 0)),
                      pl.BlockSpec((1, d), lambda i, *_: (0, 0))],
            out_specs=pl.BlockSpec(memory_space=pl.ANY),
            scratch_shapes=[pltpu.VMEM((2, tm, d), F32), pltpu.VMEM((tm, d), F32),
                            pltpu.SemaphoreType.DMA((2,)), pltpu.SemaphoreType.DMA((1,))]),
        out_shape=jax.ShapeDtypeStruct((n_blocks * tm, d), F32),
        compiler_params=_cparams("arbitrary"),
        name="moe_dispatch",
    )(dest, pad_lo, pad_hi, n_used, x1, g)


def _swiglu_pick(y, pick):
    tm, n2 = y.shape
    even = lax.broadcasted_iota(jnp.int32, (tm, 2 * LANES), 1) % 2 == 0
    outs = []
    for cblk in range(n2 // (2 * LANES)):
        yc = y[:, cblk * 2 * LANES:(cblk + 1) * 2 * LANES]
        capped = jnp.minimum(yc, SWIGLU_LIMIT)
        gate = capped * _sigmoid(SWIGLU_ALPHA * capped)
        up = jnp.maximum(capped, -SWIGLU_LIMIT) + 1.0
        factors = jnp.where(even, gate, up).astype(BF16)
        split = jnp.dot(factors, pick, preferred_element_type=F32)
        outs.append(split[:, :LANES] * split[:, LANES:])
    return jnp.concatenate(outs, axis=1)


def _gmm_kernel(ps_ref, nb_ref, nu_ref, b_ref, *rest, swiglu, tm, n_blocks):
    if swiglu:
        pick_ref, w_hbm, x_hbm, o_hbm, wbuf, wbf, xbuf, obuf, zbuf, wsem, sin, sout, zsem = rest
    else:
        w_hbm, x_hbm, o_hbm, wbuf, wbf, xbuf, obuf, zbuf, wsem, sin, sout, zsem = rest
    e = pl.program_id(0)
    j = pl.program_id(1)
    n_exp = pl.num_programs(0)
    nj = pl.num_programs(1)
    step = e * nj + j
    have_next = step < n_exp * nj - 1
    nb = nb_ref[e]
    g0 = nj * (ps_ref[e] // tm) + j * nb
    e_next = jnp.minimum(jnp.where(j + 1 < nj, e, e + 1), n_exp - 1)
    e_prev = jnp.maximum(jnp.where(j > 0, e, e - 1), 0)
    next_has = jnp.logical_and(have_next, nb_ref[e_next] > 0)
    first_here = jnp.logical_or(step == 0, nb_ref[e_prev] == 0)
    kdim, tn = wbf.shape
    tn_out = obuf.shape[2]
    col = pl.multiple_of(j * tn_out, tn_out)
    nch = kdim // W_CHUNK_ROWS

    def w_chunk(st, c):
        r = pl.ds(pl.multiple_of(c * W_CHUNK_ROWS, W_CHUNK_ROWS), W_CHUNK_ROWS)
        cols = pl.ds(pl.multiple_of((st % nj) * tn, tn), tn)
        return pltpu.make_async_copy(w_hbm.at[st // nj, r, cols], wbuf.at[r, :], wsem.at[0])

    def start_chunks(st, lo, hi):
        lax.fori_loop(lo, hi, lambda c, x: (w_chunk(st, c).start(), x)[1], 0)

    def x_copy(row0, sl):
        return pltpu.make_async_copy(x_hbm.at[pl.ds(pl.multiple_of(row0, tm), tm), :], xbuf.at[sl],
                                     sin.at[sl])

    def o_copy(b, sl):
        r = pl.ds(pl.multiple_of(ps_ref[e] + b * tm, tm), tm)
        return pltpu.make_async_copy(obuf.at[sl], o_hbm.at[r, pl.ds(col, tn_out)], sout.at[sl])

    @pl.when(step == 0)
    def _():
        start_chunks(step, 0, nch)

    @pl.when(jnp.logical_and(nb > 0, first_here))
    def _():
        x_copy(ps_ref[e], g0 % 2).start()

    lax.fori_loop(0, nch, lambda c, x: (w_chunk(step, c).wait(), x)[1], 0)

    @pl.when(nb > 0)
    def _():
        wbf[...] = wbuf[...].astype(BF16)

    per_block = (nch + jnp.maximum(nb, 1) - 1) // jnp.maximum(nb, 1)

    def block(b, c):
        sl = (g0 + b) % 2
        more = b + 1 < nb

        @pl.when(jnp.logical_or(more, next_has))
        def _():
            x_copy(jnp.where(more, ps_ref[e] + (b + 1) * tm, ps_ref[e_next]), 1 - sl).start()

        @pl.when(have_next)
        def _():
            start_chunks(step + 1, jnp.minimum(b * per_block, nch),
                         jnp.minimum((b + 1) * per_block, nch))

        x_copy(ps_ref[e] + b * tm, sl).wait()

        @pl.when(g0 + b >= 2)
        def _():
            o_copy(b, sl).wait()

        y = jnp.dot(xbuf[sl].astype(BF16), wbf[...], preferred_element_type=F32) + b_ref[...]
        if swiglu:
            y = _swiglu_pick(y, pick_ref[...])
        obuf[sl] = y.astype(obuf.dtype)
        o_copy(b, sl).start()
        return c

    lax.fori_loop(0, nb, block, 0)

    @pl.when(jnp.logical_and(have_next, nb == 0))
    def _():
        start_chunks(step + 1, 0, nch)

    @pl.when(e == n_exp - 1)
    def _():
        zbuf[...] = jnp.zeros(zbuf.shape, zbuf.dtype)

        def tail(b):
            return pltpu.make_async_copy(
                zbuf, o_hbm.at[pl.ds(pl.multiple_of(b * tm, tm), tm), pl.ds(col, tn_out)], zsem.at[0])

        lax.fori_loop(nu_ref[0], n_blocks, lambda b, c: (tail(b).start(), c)[1], 0)
        lax.fori_loop(nu_ref[0], n_blocks, lambda b, c: (tail(b).wait(), c)[1], 0)

    @pl.when(jnp.logical_not(have_next))
    def _():
        done = g0 + nb

        @pl.when(done >= 2)
        def _():
            o_copy(0, done % 2).wait()

        @pl.when(done >= 1)
        def _():
            o_copy(0, (done + 1) % 2).wait()


def _grouped_matmul(x, w, b, pstart, nblk, n_used, *, tm, tn, swiglu, out_dtype):
    n_slots, kdim = x.shape
    n_exp, _, n = w.shape
    n_out = n // 2 if swiglu else n
    tn_out = tn // 2 if swiglu else tn
    in_specs = [pl.BlockSpec((None, 1, tn), lambda e, j, *_: (e, 0, j))]
    args = [b.reshape(n_exp, 1, n)]
    if swiglu:
        src = jnp.arange(2 * LANES, dtype=jnp.int32)[:, None]
        dst = jnp.arange(2 * LANES, dtype=jnp.int32)[None, :]
        args.append((src == 2 * (dst % LANES) + dst // LANES).astype(BF16))
        in_specs.append(pl.BlockSpec((2 * LANES, 2 * LANES), lambda e, j, *_: (0, 0)))
    in_specs += [pl.BlockSpec(memory_space=pl.ANY)] * 2
    args += [w, x]
    return pl.pallas_call(
        functools.partial(_gmm_kernel, swiglu=swiglu, tm=tm, n_blocks=n_slots // tm),
        grid_spec=pltpu.PrefetchScalarGridSpec(
            num_scalar_prefetch=3,
            grid=(n_exp, n // tn),
            in_specs=in_specs,
            out_specs=pl.BlockSpec(memory_space=pl.ANY),
            scratch_shapes=[pltpu.VMEM((kdim, tn), F32), pltpu.VMEM((kdim, tn), BF16),
                            pltpu.VMEM((2, tm, kdim), x.dtype), pltpu.VMEM((2, tm, tn_out), out_dtype),
                            pltpu.VMEM((tm, tn_out), out_dtype),
                            pltpu.SemaphoreType.DMA((1,)), pltpu.SemaphoreType.DMA((2,)),
                            pltpu.SemaphoreType.DMA((2,)), pltpu.SemaphoreType.DMA((1,))]),
        out_shape=jax.ShapeDtypeStruct((n_slots, n_out), out_dtype),
        compiler_params=_cparams("arbitrary", "arbitrary"),
        name="moe_gate_up" if swiglu else "moe_down",
    )(pstart, nblk, n_used, *args)


def _row_copy(src_hbm, row, dst, r, sem):
    return pltpu.make_async_copy(src_hbm.at[pl.ds(row, 1), :], dst.at[pl.ds(r, 1), :], sem)


def _combine_kernel(dest_ref, x_ref, gate_ref, g_ref, ys_hbm, o_ref, buf_ref, sem, *, norm):
    i = pl.program_id(0)
    tc = x_ref.shape[0]
    slot = i % 2

    def gather(step, sl):
        def start(r, c):
            for k in range(TOP_K):
                _row_copy(ys_hbm, dest_ref[k, step * tc + r], buf_ref.at[sl, k], r,
                          sem.at[sl]).start(priority=k % 2)
            return c
        lax.fori_loop(0, tc, start, 0)

    @pl.when(i == 0)
    def _():
        gather(i, slot)

    @pl.when(i + 1 < pl.num_programs(0))
    def _():
        gather(i + 1, 1 - slot)

    for k in range(TOP_K):
        pltpu.make_async_copy(ys_hbm.at[pl.ds(0, tc), :], buf_ref.at[slot, k], sem.at[slot]).wait()
    y = x_ref[...]
    for k in range(TOP_K):
        y = y + gate_ref[:, k:k + 1] * buf_ref[slot, k]
    o_ref[...] = _rms(y, g_ref[...]) if norm else y


def _combine(x1, gates_t, dest, ys, g_final, norm, tc=128):
    t, d = x1.shape
    tc = min(tc, t)
    return pl.pallas_call(
        functools.partial(_combine_kernel, norm=norm),
        grid_spec=pltpu.PrefetchScalarGridSpec(
            num_scalar_prefetch=1,
            grid=(t // tc,),
            in_specs=[pl.BlockSpec((tc, d), lambda i, dest: (i, 0)),
                      pl.BlockSpec((tc, TOP_K), lambda i, dest: (i, 0)),
                      pl.BlockSpec((1, d), lambda i, dest: (0, 0)),
                      pl.BlockSpec(memory_space=pl.ANY)],
            out_specs=pl.BlockSpec((tc, d), lambda i, dest: (i, 0)),
            scratch_shapes=[pltpu.VMEM((2, TOP_K, tc, d), F32), pltpu.SemaphoreType.DMA((2,))]),
        out_shape=jax.ShapeDtypeStruct((t, d), F32),
        compiler_params=_cparams("arbitrary"),
        name="moe_combine",
    )(dest, x1, gates_t, g_final, ys)


def _mixer(x, layer, g_mix, w_in, conv_w, conv_b, dt_bias, a_log, d_skip, g_ssm, w_out):
    w_bf, w_dt = _to_bf16(w_in, layer, ntail=SSM_HEADS)
    q, k, v, z, xbc, dt = _in_proj(x, g_mix.reshape(1, -1), w_bf, w_dt)
    attn = _attention(q, k, v)
    ssm = _ssd(z, xbc, dt, conv_w, conv_b, dt_bias, a_log, d_skip, g_ssm)
    return _out_proj(x, attn, ssm, _to_bf16(w_out, layer))


def _moe(x1, g_ffn, w_router, b_router, w_gate_up, b_gate_up, w_down, b_down, g_out, norm, tm=256):
    t, d = x1.shape
    g_ffn = g_ffn.reshape(1, -1)
    idx, gates, pos, cnt = _router(x1, g_ffn, w_router, b_router)
    counts = cnt[:, 0]
    padded = (counts + tm - 1) // tm * tm
    pend = jnp.cumsum(padded)
    pstart = pend - padded
    experts = jnp.arange(N_EXPERTS, dtype=jnp.int32)
    dest = pos + jnp.sum(jnp.where(idx[None] == experts[:, None, None],
                                   pstart[:, None, None], 0), axis=0)
    n_blocks = (t * TOP_K) // tm + N_EXPERTS
    n_used = (pend[-1] // tm).reshape(1)
    nblk = padded // tm

    xs = _dispatch(x1, g_ffn, dest, pstart + counts, pend, n_used, n_blocks, tm)
    act = _grouped_matmul(xs, w_gate_up, b_gate_up, pstart, nblk, n_used,
                          tm=tm, tn=2048, swiglu=True, out_dtype=BF16)
    ys = _grouped_matmul(act, w_down, b_down, pstart, nblk, n_used,
                         tm=tm, tn=2048, swiglu=False, out_dtype=F32)
    return _combine(x1, gates.T, dest, ys, g_out.reshape(1, -1), norm)


def kernel(x, g_mix, w_in, conv_w, conv_b, dt_bias, a_log, d_skip, g_ssm, w_out, g_ffn, w_router,
           b_router, w_gate_up, b_gate_up, w_down, b_down, g_final):
    b, s, d = x.shape
    depth = g_mix.shape[0]
    outs = []
    for bi in range(b):
        xb = x[bi]
        for l in range(depth):
            x1 = _mixer(xb, l, g_mix[l], w_in, conv_w[l], conv_b[l], dt_bias[l], a_log[l],
                        d_skip[l], g_ssm[l], w_out)
            xb = _moe(x1, g_ffn[l], w_router[l], b_router[l], w_gate_up[l], b_gate_up[l],
                      w_down[l], b_down[l], g_final, norm=(l == depth - 1))
        outs.append(xb)
    return outs[0][None] if b == 1 else jnp.stack(outs)
```

```python
import functools

import jax
import jax.numpy as jnp
from jax import lax
from jax.experimental import pallas as pl
from jax.experimental.pallas import tpu as pltpu

F32 = jnp.float32
BF16 = jnp.bfloat16
HIGHEST = lax.Precision.HIGHEST

LANES = 128
HEAD_DIM = 64
SB_WIDTH = 1024
SSM_INNER = 1024
SSM_GROUPS = 2
SSM_STATE = 128
SSM_CONV = 4
SSM_CONV_DIM = SSM_INNER + 2 * SSM_GROUPS * SSM_STATE
SSM_HEADS = SSM_INNER // HEAD_DIM
N_EXPERTS = 32
TOP_K = 4
SWIGLU_LIMIT = 7.0
SWIGLU_ALPHA = 1.702
RMS_EPS = 1e-5

VMEM_LIMIT = 56 * 1024 * 1024

ATTN_DEAD = -110.0

W_CHUNK_ROWS = 256


def _cparams(*sem):
    return pltpu.CompilerParams(dimension_semantics=sem, vmem_limit_bytes=VMEM_LIMIT)


def _rms(x, g):
    return x * lax.rsqrt(jnp.mean(x * x, axis=-1, keepdims=True) + RMS_EPS) * g


def _sigmoid(x):
    return 1.0 / (1.0 + jnp.exp(-x))


def _softplus(x):
    return jnp.maximum(x, 0.0) + jnp.log(1.0 + jnp.exp(-jnp.abs(x)))


def _cast_kernel(w_ref, o_ref):
    o_ref[...] = w_ref[...].astype(o_ref.dtype)


def _to_bf16(w, layer, rows=256):
    _, k, n = w.shape
    return pl.pallas_call(
        _cast_kernel,
        grid=(k // rows,),
        in_specs=[pl.BlockSpec((None, rows, n), lambda i: (layer, i, 0))],
        out_specs=pl.BlockSpec((rows, n), lambda i: (i, 0)),
        out_shape=jax.ShapeDtypeStruct((k, n), BF16),
        compiler_params=_cparams("arbitrary"),
        name="cast_bf16",
    )(w)


def _cast_t_kernel(w_ref, o_ref, tail_ref, *, ntail):
    w = w_ref[...]
    o_ref[...] = w.astype(o_ref.dtype)
    tail_ref[...] = jnp.concatenate(
        [w[w.shape[0] - ntail:, :], jnp.zeros((LANES - ntail, w.shape[1]), F32)], axis=0)


def _to_bf16_t(w_t, layer, ntail, cols=256):
    _, n, k = w_t.shape
    return pl.pallas_call(
        functools.partial(_cast_t_kernel, ntail=ntail),
        grid=(k // cols,),
        in_specs=[pl.BlockSpec((None, n, cols), lambda i: (layer, 0, i))],
        out_specs=[pl.BlockSpec((n, cols), lambda i: (0, i)),
                   pl.BlockSpec((LANES, cols), lambda i: (0, i))],
        out_shape=[jax.ShapeDtypeStruct((n, k), BF16), jax.ShapeDtypeStruct((LANES, k), F32)],
        compiler_params=_cparams("arbitrary"),
        name="cast_bf16_t",
    )(w_t)


def _in_proj_kernel(x_ref, g_ref, w_ref, wdt_ref, q_ref, k_ref, v_ref, z_ref, xbc_ref, dt_ref):
    hb = _rms(x_ref[...], g_ref[...]).astype(BF16)
    nt = (((1,), (1,)), ((), ()))

    def mm(a, b):
        return lax.dot_general(hb, w_ref[a:b, :], nt, preferred_element_type=F32)

    o = 0
    q_ref[...] = (mm(o, o + SB_WIDTH) * (HEAD_DIM ** -0.5)).astype(BF16)
    o += SB_WIDTH
    k_ref[...] = mm(o, o + SB_WIDTH).astype(BF16)
    o += SB_WIDTH
    v_ref[...] = mm(o, o + SB_WIDTH).astype(BF16)
    o += SB_WIDTH
    z_ref[...] = mm(o, o + SSM_INNER)
    o += SSM_INNER
    xbc_ref[...] = mm(o, o + SSM_CONV_DIM)
    dt_ref[...] = lax.dot_general(hb, wdt_ref[...].astype(BF16), nt, preferred_element_type=F32)


def _in_proj(x, g, w_bf_t, w_dt_t, tm=256):
    t, d = x.shape
    n = w_bf_t.shape[0]
    row = lambda w: pl.BlockSpec((tm, w), lambda i: (i, 0))
    outs = [(SB_WIDTH, BF16)] * 3 + [(SSM_INNER, F32), (SSM_CONV_DIM, F32), (LANES, F32)]
    return pl.pallas_call(
        _in_proj_kernel,
        grid=(t // tm,),
        in_specs=[row(d),
                  pl.BlockSpec((1, d), lambda i: (0, 0)),
                  pl.BlockSpec((n, d), lambda i: (0, 0), pipeline_mode=pl.Buffered(1)),
                  pl.BlockSpec((LANES, d), lambda i: (0, 0))],
        out_specs=[row(w) for w, _ in outs],
        out_shape=[jax.ShapeDtypeStruct((t, w), dt) for w, dt in outs],
        compiler_params=_cparams("arbitrary"),
        name="in_proj",
    )(x, g, w_bf_t, w_dt_t)


def _attn_kernel(q_ref, k_ref, v_ref, o_ref, acc_ref, carry_ref, *, tq):
    i = pl.program_id(1)
    nsub = tq // LANES
    qb0 = i * nsub
    row = lax.broadcasted_iota(jnp.int32, (LANES, LANES), 0)
    col = lax.broadcasted_iota(jnp.int32, (LANES, LANES), 1)
    past = col < row
    lo = col < HEAD_DIM
    jj = lax.broadcasted_iota(jnp.int32, (LANES, 2 * LANES), 0)
    ss = lax.broadcasted_iota(jnp.int32, (LANES, 2 * LANES), 1)
    neg_later = jnp.where((ss >= LANES) | (jj > ss), -1.0, 0.0).astype(BF16)
    nt = (((1,), (1,)), ((), ()))

    def step(n, diag):
        k0s = [pl.multiple_of(jnp.maximum(qb0 + s - n, 0) * LANES, LANES) for s in range(nsub)]
        chains = [(s, hd) for s in range(nsub) for hd in range(2)]
        logits = []
        for s, hd in chains:
            q = q_ref[s * LANES:(s + 1) * LANES, :]
            qh = jnp.where(lo if hd == 0 else jnp.logical_not(lo), q, jnp.zeros(q.shape, q.dtype))
            kt = k_ref[pl.ds(k0s[s], LANES), :]
            logits.append(lax.dot_general(qh, kt, nt, preferred_element_type=F32))
        logsig, sums = [], []
        for l in logits:
            sp = _softplus(l)
            lk = jnp.where(past, sp, 0.0) if diag else sp
            sums.append(jnp.dot(lk.astype(BF16), neg_later, preferred_element_type=F32))
            logsig.append(l - sp)
        top = None
        for c, (s, hd) in enumerate(chains):
            r = sums[c]
            if diag:
                w = jnp.where(past, jnp.exp(logsig[c] + r[:, :LANES]), 0.0)
                carry = r[:, LANES:]
            else:
                before = jnp.where(qb0 + s - n >= 0, carry_ref[c], -1e30)
                w = jnp.exp(logsig[c] + r[:, :LANES] + before)
                carry = before + r[:, LANES:]
            vt = v_ref[pl.ds(k0s[s], LANES), :]
            pv = jnp.dot(w.astype(BF16), vt, preferred_element_type=F32)
            if diag:
                acc_ref[c] = pv
            else:
                acc_ref[c] += pv
            carry_ref[c] = carry
            top = carry if top is None else jnp.maximum(top, carry)
        return jnp.max(top)

    top0 = step(0, True)

    def cond(st):
        n, top = st
        return jnp.logical_and(n <= qb0 + nsub - 1, top > ATTN_DEAD)

    def body(st):
        n, _ = st
        return n + 1, step(n, False)

    lax.while_loop(cond, body, (jnp.int32(1), top0))
    for s in range(nsub):
        o_ref[s * LANES:(s + 1) * LANES, :] = jnp.where(
            lo, acc_ref[2 * s], acc_ref[2 * s + 1]).astype(o_ref.dtype)


def _attention(q, k, v, tq=1024):
    t, w = q.shape
    tq = min(tq, t)
    chains = 2 * (tq // LANES)
    return pl.pallas_call(
        functools.partial(_attn_kernel, tq=tq),
        grid=(w // LANES, t // tq),
        in_specs=[pl.BlockSpec((tq, LANES), lambda p, i: (i, p)),
                  pl.BlockSpec((t, LANES), lambda p, i: (0, p)),
                  pl.BlockSpec((t, LANES), lambda p, i: (0, p))],
        out_specs=pl.BlockSpec((tq, LANES), lambda p, i: (i, p)),
        out_shape=jax.ShapeDtypeStruct((t, w), BF16),
        scratch_shapes=[pltpu.VMEM((chains, LANES, LANES), F32),
                        pltpu.VMEM((chains, LANES, LANES), F32)],
        compiler_params=_cparams("arbitrary", "arbitrary"),
        name="sb_attention",
    )(q, k, v)


def _ssd_kernel(z_ref, xbc_ref, dt_ref, cw_ref, cb_ref, dtb_ref, alog_ref, dsk_ref, g_ref, e_ref,
                o_ref, state_ref, xcat_ref):
    c = pl.program_id(0)
    L = z_ref.shape[0]
    half = SSM_INNER // SSM_GROUPS
    pad = 8

    @pl.when(c == 0)
    def _():
        state_ref[...] = jnp.zeros(state_ref.shape, F32)
        xcat_ref[0:pad, :] = jnp.zeros((pad, SSM_CONV_DIM), F32)

    xcat_ref[pad:pad + L, :] = xbc_ref[...]
    acc = jnp.broadcast_to(cb_ref[...], (L, SSM_CONV_DIM))
    for kk in range(SSM_CONV):
        s = pad - (SSM_CONV - 1) + kk
        acc = acc + cw_ref[kk:kk + 1, :] * xcat_ref[s:s + L, :]
    xcat_ref[0:pad, :] = xcat_ref[L:L + pad, :]
    xc = acc * _sigmoid(acc)
    xs = xc[:, :SSM_INNER]
    bm = xc[:, SSM_INNER:SSM_INNER + SSM_GROUPS * SSM_STATE]
    cm = xc[:, SSM_INNER + SSM_GROUPS * SSM_STATE:]

    dtv = _softplus(dt_ref[...] + dtb_ref[...])
    adt = dtv * (-jnp.exp(alog_ref[...]))
    ri = lax.broadcasted_iota(jnp.int32, (L, L), 0)
    ci = lax.broadcasted_iota(jnp.int32, (L, L), 1)
    causal = ri >= ci
    tri = jnp.where(causal, 1.0, 0.0).astype(F32)
    acum = jnp.dot(tri, adt, precision=HIGHEST, preferred_element_type=F32)
    expand = e_ref[...]
    acx = jnp.dot(acum, expand, precision=HIGHEST, preferred_element_type=F32)
    dtx = jnp.dot(dtv, expand, precision=HIGHEST, preferred_element_type=F32)
    alx = acx[L - 1:L, :]
    xdt = xs * dtx
    acum_t = acum.T
    lo = lax.broadcasted_iota(jnp.int32, (L, LANES), 1) < HEAD_DIM
    nt = (((1,), (1,)), ((), ()))

    xdt_b = xdt.astype(BF16)
    y_parts = []
    for g in range(SSM_GROUPS):
        cg = cm[:, g * SSM_STATE:(g + 1) * SSM_STATE].astype(BF16)
        bg = bm[:, g * SSM_STATE:(g + 1) * SSM_STATE]
        y_off = jnp.dot(cg, state_ref[:, g * half:(g + 1) * half].astype(BF16),
                        preferred_element_type=F32)
        cb = lax.dot_general(cg, bg.astype(BF16), nt, preferred_element_type=F32)
        for p in range(half // LANES):
            pair = g * (half // LANES) + p
            xp = xdt_b[:, pair * LANES:(pair + 1) * LANES]
            yh = []
            for hh in range(2):
                h = 2 * pair + hh
                seg = acum[:, h:h + 1] - acum_t[h:h + 1, :]
                dec = jnp.where(causal, jnp.exp(jnp.minimum(seg, 0.0)), 0.0)
                yh.append(jnp.dot((cb * dec).astype(BF16), xp, preferred_element_type=F32))
            y_parts.append(jnp.where(lo, yh[0], yh[1])
                           + y_off[:, p * LANES:(p + 1) * LANES]
                           * jnp.exp(acx[:, pair * LANES:(pair + 1) * LANES]))
        xd = xdt[:, g * half:(g + 1) * half] * jnp.exp(alx[:, g * half:(g + 1) * half]
                                                       - acx[:, g * half:(g + 1) * half])
        upd = jnp.dot(bg.T.astype(BF16), xd.astype(BF16), preferred_element_type=F32)
        state_ref[:, g * half:(g + 1) * half] = (
            jnp.exp(alx[:, g * half:(g + 1) * half]) * state_ref[:, g * half:(g + 1) * half] + upd)

    y = jnp.concatenate(y_parts, axis=1) + dsk_ref[...] * xs
    zz = z_ref[...]
    y = y * (zz * _sigmoid(zz))
    outs = []
    for g in range(SSM_GROUPS):
        yg = y[:, g * half:(g + 1) * half]
        outs.append(yg * lax.rsqrt(jnp.mean(yg * yg, axis=-1, keepdims=True) + RMS_EPS))
    o_ref[...] = (jnp.concatenate(outs, axis=1) * g_ref[...]).astype(o_ref.dtype)


def _ssd(z, xbc, dt, conv_w, conv_b, dt_bias, a_log, d_skip, g_ssm, chunk=128):
    t = z.shape[0]
    chunk = min(chunk, t)
    padh = lambda a: jnp.pad(a.reshape(1, -1), ((0, 0), (0, LANES - a.shape[-1])))
    heads = jnp.arange(SSM_INNER, dtype=jnp.int32) // HEAD_DIM
    expand = (jnp.arange(LANES, dtype=jnp.int32)[:, None] == heads[None, :]).astype(F32)
    row = lambda w: pl.BlockSpec((chunk, w), lambda c: (c, 0))
    full = lambda a: pl.BlockSpec(a.shape, lambda c: (0, 0))
    params = [conv_w, conv_b.reshape(1, -1), padh(dt_bias), padh(a_log),
              jnp.repeat(d_skip, HEAD_DIM).reshape(1, -1), g_ssm.reshape(1, -1), expand]
    return pl.pallas_call(
        _ssd_kernel,
        grid=(t // chunk,),
        in_specs=[row(SSM_INNER), row(SSM_CONV_DIM), row(LANES)] + [full(a) for a in params],
        out_specs=row(SSM_INNER),
        out_shape=jax.ShapeDtypeStruct((t, SSM_INNER), BF16),
        scratch_shapes=[pltpu.VMEM((SSM_STATE, SSM_INNER), F32),
                        pltpu.VMEM((chunk + 8, SSM_CONV_DIM), F32)],
        compiler_params=_cparams("arbitrary"),
        name="ssd_mixer",
    )(z, xbc, dt, *params)


def _out_proj_kernel(x_ref, a_ref, s_ref, w_ref, o_ref):
    o_ref[...] = (x_ref[...]
                  + jnp.dot(a_ref[...], w_ref[0:SB_WIDTH, :], preferred_element_type=F32)
                  + jnp.dot(s_ref[...], w_ref[SB_WIDTH:, :], preferred_element_type=F32))


def _out_proj(x, attn, ssm, w_bf, tm=512):
    t, d = x.shape
    tm = min(tm, t)
    return pl.pallas_call(
        _out_proj_kernel,
        grid=(t // tm,),
        in_specs=[pl.BlockSpec((tm, d), lambda i: (i, 0)),
                  pl.BlockSpec((tm, SB_WIDTH), lambda i: (i, 0)),
                  pl.BlockSpec((tm, SSM_INNER), lambda i: (i, 0)),
                  pl.BlockSpec(w_bf.shape, lambda i: (0, 0), pipeline_mode=pl.Buffered(1))],
        out_specs=pl.BlockSpec((tm, d), lambda i: (i, 0)),
        out_shape=jax.ShapeDtypeStruct((t, d), F32),
        compiler_params=_cparams("arbitrary"),
        name="out_proj",
    )(x, attn, ssm, w_bf)


def _router_kernel(x_ref, g_ref, wt_ref, b_ref, idx_ref, gate_ref, pos_ref, cnt_ref, base_ref):
    i = pl.program_id(0)
    tm = x_ref.shape[0]

    @pl.when(i == 0)
    def _():
        base_ref[...] = jnp.zeros(base_ref.shape, F32)

    h = _rms(x_ref[...], g_ref[...])
    logits = lax.dot_general(wt_ref[...], h, (((1,), (1,)), ((), ())), precision=HIGHEST,
                             preferred_element_type=F32) + b_ref[:, 0:1]
    eio = lax.broadcasted_iota(jnp.int32, (N_EXPERTS, tm), 0).astype(F32)
    work = logits
    vals, hots = [], []
    for k in range(TOP_K):
        m = jnp.max(work, axis=0, keepdims=True)
        idx = jnp.min(jnp.where(work == m, eio, float(N_EXPERTS)), axis=0, keepdims=True)
        hot = eio == idx
        work = jnp.where(hot, -jnp.inf, work)
        vals.append(m)
        hots.append(hot)
        idx_ref[k:k + 1, :] = idx.astype(jnp.int32)
    ex = [jnp.exp(v - vals[0]) for v in vals]
    den = ex[0] + ex[1] + ex[2] + ex[3]
    picked = jnp.zeros((N_EXPERTS, tm), F32)
    for k in range(TOP_K):
        gate_ref[k:k + 1, :] = ex[k] / den
        picked = picked + jnp.where(hots[k], 1.0, 0.0)
    ti = lax.broadcasted_iota(jnp.int32, (tm, tm), 0)
    tj = lax.broadcasted_iota(jnp.int32, (tm, tm), 1)
    before = jnp.where(ti < tj, 1.0, 0.0).astype(BF16)
    rank = jnp.dot(picked.astype(BF16), before, preferred_element_type=F32) + base_ref[:, 0:1]
    for k in range(TOP_K):
        pos = jnp.sum(jnp.where(hots[k], rank, 0.0), axis=0, keepdims=True)
        pos_ref[k:k + 1, :] = pos.astype(jnp.int32)
    base_ref[...] = base_ref[...] + jnp.sum(picked, axis=1, keepdims=True)
    cnt_ref[...] = base_ref[...].astype(jnp.int32)


def _router(x1, g, w_router, b_router, tm=512):
    t, d = x1.shape
    tm = min(tm, t)
    wt = w_router.T
    b = jnp.broadcast_to(b_router.reshape(-1, 1), (N_EXPERTS, LANES))
    tok = lambda dt: (pl.BlockSpec((TOP_K, tm), lambda i: (0, i)), jax.ShapeDtypeStruct((TOP_K, t), dt))
    specs = [tok(jnp.int32), tok(F32), tok(jnp.int32),
             (pl.BlockSpec((N_EXPERTS, LANES), lambda i: (0, 0)),
              jax.ShapeDtypeStruct((N_EXPERTS, LANES), jnp.int32))]
    return pl.pallas_call(
        _router_kernel,
        grid=(t // tm,),
        in_specs=[pl.BlockSpec((tm, d), lambda i: (i, 0)),
                  pl.BlockSpec((1, d), lambda i: (0, 0)),
                  pl.BlockSpec((N_EXPERTS, d), lambda i: (0, 0)),
                  pl.BlockSpec((N_EXPERTS, LANES), lambda i: (0, 0))],
        out_specs=[s for s, _ in specs],
        out_shape=[s for _, s in specs],
        scratch_shapes=[pltpu.VMEM((N_EXPERTS, LANES), F32)],
        compiler_params=_cparams("arbitrary"),
        name="router",
    )(x1, g, wt, b)


def _dispatch_kernel(dest_ref, lo_ref, hi_ref, nu_ref, x_ref, g_ref, xs_hbm, hbuf, zbuf, sem, zsem,
                     *, n_blocks):
    i = pl.program_id(0)
    last = pl.num_programs(0) - 1
    tm = x_ref.shape[0]
    slot = i % 2

    def row_copies(step, sl, r):
        return [pltpu.make_async_copy(hbuf.at[sl, pl.ds(r, 1), :],
                                      xs_hbm.at[pl.ds(dest_ref[k, step * tm + r], 1), :], sem.at[sl])
                for k in range(TOP_K)]

    def wait_step(sl):
        for _ in range(TOP_K):
            pltpu.make_async_copy(hbuf.at[sl], xs_hbm.at[pl.ds(0, tm), :], sem.at[sl]).wait()

    @pl.when(i >= 2)
    def _():
        wait_step(slot)

    hbuf[slot] = _rms(x_ref[...], g_ref[...])

    def start(r, c):
        for k, cp in enumerate(row_copies(i, slot, r)):
            cp.start(priority=k % 2)
        return c
    lax.fori_loop(0, tm, start, 0)

    @pl.when(i == 0)
    def _():
        zbuf[...] = jnp.zeros(zbuf.shape, F32)

        def pad_row(r):
            return pltpu.make_async_copy(zbuf.at[pl.ds(0, 1), :], xs_hbm.at[pl.ds(r, 1), :], zsem.at[0])

        def tail_block(b):
            return pltpu.make_async_copy(zbuf, xs_hbm.at[pl.ds(pl.multiple_of(b * tm, tm), tm), :],
                                         zsem.at[0])

        def each(fn):
            def per_expert(e, c):
                return lax.fori_loop(lo_ref[e], hi_ref[e], lambda r, c2: (fn(pad_row(r)), c2)[1], c)
            lax.fori_loop(0, N_EXPERTS, per_expert, 0)
            lax.fori_loop(nu_ref[0], n_blocks, lambda b, c: (fn(tail_block(b)), c)[1], 0)

        each(lambda cp: cp.start())
        each(lambda cp: cp.wait())

    @pl.when(i == last)
    def _():
        @pl.when(i >= 1)
        def _():
            wait_step(1 - slot)
        wait_step(slot)


def _dispatch(x1, g, dest, pad_lo, pad_hi, n_used, n_blocks, tm):
    t, d = x1.shape
    return pl.pallas_call(
        functools.partial(_dispatch_kernel, n_blocks=n_blocks),
        grid_spec=pltpu.PrefetchScalarGridSpec(
            num_scalar_prefetch=4,
            grid=(t // tm,),
            in_specs=[pl.BlockSpec((tm, d), lambda i, *_: (i, 0)),
                      pl.BlockSpec((1, d), lambda i, *_: (0, 0))],
            out_specs=pl.BlockSpec(memory_space=pl.ANY),
            scratch_shapes=[pltpu.VMEM((2, tm, d), F32), pltpu.VMEM((tm, d), F32),
                            pltpu.SemaphoreType.DMA((2,)), pltpu.SemaphoreType.DMA((1,))]),
        out_shape=jax.ShapeDtypeStruct((n_blocks * tm, d), F32),
        compiler_params=_cparams("arbitrary"),
        name="moe_dispatch",
    )(dest, pad_lo, pad_hi, n_used, x1, g)


def _swiglu_pick(y, pick):
    tm, n2 = y.shape
    even = lax.broadcasted_iota(jnp.int32, (tm, 2 * LANES), 1) % 2 == 0
    outs = []
    for cblk in range(n2 // (2 * LANES)):
        yc = y[:, cblk * 2 * LANES:(cblk + 1) * 2 * LANES]
        capped = jnp.minimum(yc, SWIGLU_LIMIT)
        gate = capped * _sigmoid(SWIGLU_ALPHA * capped)
        up = jnp.maximum(capped, -SWIGLU_LIMIT) + 1.0
        factors = jnp.where(even, gate, up).astype(BF16)
        split = jnp.dot(factors, pick, preferred_element_type=F32)
        outs.append(split[:, :LANES] * split[:, LANES:])
    return jnp.concatenate(outs, axis=1)


def _gmm_kernel(ps_ref, nb_ref, nu_ref, b_ref, *rest, swiglu, tm, n_blocks):
    if swiglu:
        pick_ref, w_hbm, x_hbm, o_hbm, wbuf, wbf, xbuf, obuf, zbuf, wsem, sin, sout, zsem = rest
    else:
        w_hbm, x_hbm, o_hbm, wbuf, wbf, xbuf, obuf, zbuf, wsem, sin, sout, zsem = rest
    e = pl.program_id(0)
    j = pl.program_id(1)
    n_exp = pl.num_programs(0)
    nj = pl.num_programs(1)
    step = e * nj + j
    have_next = step < n_exp * nj - 1
    nb = nb_ref[e]
    g0 = nj * (ps_ref[e] // tm) + j * nb
    e_next = jnp.minimum(jnp.where(j + 1 < nj, e, e + 1), n_exp - 1)
    e_prev = jnp.maximum(jnp.where(j > 0, e, e - 1), 0)
    next_has = jnp.logical_and(have_next, nb_ref[e_next] > 0)
    first_here = jnp.logical_or(step == 0, nb_ref[e_prev] == 0)
    kdim, tn = wbf.shape
    tn_out = obuf.shape[2]
    col = pl.multiple_of(j * tn_out, tn_out)
    nch = kdim // W_CHUNK_ROWS

    def w_chunk(st, c):
        r = pl.ds(pl.multiple_of(c * W_CHUNK_ROWS, W_CHUNK_ROWS), W_CHUNK_ROWS)
        cols = pl.ds(pl.multiple_of((st % nj) * tn, tn), tn)
        return pltpu.make_async_copy(w_hbm.at[st // nj, r, cols], wbuf.at[r, :], wsem.at[0])

    def start_chunks(st, lo, hi):
        lax.fori_loop(lo, hi, lambda c, x: (w_chunk(st, c).start(), x)[1], 0)

    def x_copy(row0, sl):
        return pltpu.make_async_copy(x_hbm.at[pl.ds(pl.multiple_of(row0, tm), tm), :], xbuf.at[sl],
                                     sin.at[sl])

    def o_copy(b, sl):
        r = pl.ds(pl.multiple_of(ps_ref[e] + b * tm, tm), tm)
        return pltpu.make_async_copy(obuf.at[sl], o_hbm.at[r, pl.ds(col, tn_out)], sout.at[sl])

    @pl.when(step == 0)
    def _():
        start_chunks(step, 0, nch)

    @pl.when(jnp.logical_and(nb > 0, first_here))
    def _():
        x_copy(ps_ref[e], g0 % 2).start()

    lax.fori_loop(0, nch, lambda c, x: (w_chunk(step, c).wait(), x)[1], 0)

    @pl.when(nb > 0)
    def _():
        wbf[...] = wbuf[...].astype(BF16)

    per_block = (nch + jnp.maximum(nb, 1) - 1) // jnp.maximum(nb, 1)

    def block(b, c):
        sl = (g0 + b) % 2
        more = b + 1 < nb

        @pl.when(jnp.logical_or(more, next_has))
        def _():
            x_copy(jnp.where(more, ps_ref[e] + (b + 1) * tm, ps_ref[e_next]), 1 - sl).start()

        @pl.when(have_next)
        def _():
            start_chunks(step + 1, jnp.minimum(b * per_block, nch),
                         jnp.minimum((b + 1) * per_block, nch))

        x_copy(ps_ref[e] + b * tm, sl).wait()

        @pl.when(g0 + b >= 2)
        def _():
            o_copy(b, sl).wait()

        y = jnp.dot(xbuf[sl].astype(BF16), wbf[...], preferred_element_type=F32) + b_ref[...]
        if swiglu:
            y = _swiglu_pick(y, pick_ref[...])
        obuf[sl] = y.astype(obuf.dtype)
        o_copy(b, sl).start()
        return c

    lax.fori_loop(0, nb, block, 0)

    @pl.when(jnp.logical_and(have_next, nb == 0))
    def _():
        start_chunks(step + 1, 0, nch)

    @pl.when(e == n_exp - 1)
    def _():
        zbuf[...] = jnp.zeros(zbuf.shape, zbuf.dtype)

        def tail(b):
            return pltpu.make_async_copy(
                zbuf, o_hbm.at[pl.ds(pl.multiple_of(b * tm, tm), tm), pl.ds(col, tn_out)], zsem.at[0])

        lax.fori_loop(nu_ref[0], n_blocks, lambda b, c: (tail(b).start(), c)[1], 0)
        lax.fori_loop(nu_ref[0], n_blocks, lambda b, c: (tail(b).wait(), c)[1], 0)

    @pl.when(jnp.logical_not(have_next))
    def _():
        done = g0 + nb

        @pl.when(done >= 2)
        def _():
            o_copy(0, done % 2).wait()

        @pl.when(done >= 1)
        def _():
            o_copy(0, (done + 1) % 2).wait()


def _grouped_matmul(x, w, b, pstart, nblk, n_used, *, tm, tn, swiglu, out_dtype):
    n_slots, kdim = x.shape
    n_exp, _, n = w.shape
    n_out = n // 2 if swiglu else n
    tn_out = tn // 2 if swiglu else tn
    in_specs = [pl.BlockSpec((None, 1, tn), lambda e, j, *_: (e, 0, j))]
    args = [b.reshape(n_exp, 1, n)]
    if swiglu:
        src = jnp.arange(2 * LANES, dtype=jnp.int32)[:, None]
        dst = jnp.arange(2 * LANES, dtype=jnp.int32)[None, :]
        args.append((src == 2 * (dst % LANES) + dst // LANES).astype(BF16))
        in_specs.append(pl.BlockSpec((2 * LANES, 2 * LANES), lambda e, j, *_: (0, 0)))
    in_specs += [pl.BlockSpec(memory_space=pl.ANY)] * 2
    args += [w, x]
    return pl.pallas_call(
        functools.partial(_gmm_kernel, swiglu=swiglu, tm=tm, n_blocks=n_slots // tm),
        grid_spec=pltpu.PrefetchScalarGridSpec(
            num_scalar_prefetch=3,
            grid=(n_exp, n // tn),
            in_specs=in_specs,
            out_specs=pl.BlockSpec(memory_space=pl.ANY),
            scratch_shapes=[pltpu.VMEM((kdim, tn), F32), pltpu.VMEM((kdim, tn), BF16),
                            pltpu.VMEM((2, tm, kdim), x.dtype), pltpu.VMEM((2, tm, tn_out), out_dtype),
                            pltpu.VMEM((tm, tn_out), out_dtype),
                            pltpu.SemaphoreType.DMA((1,)), pltpu.SemaphoreType.DMA((2,)),
                            pltpu.SemaphoreType.DMA((2,)), pltpu.SemaphoreType.DMA((1,))]),
        out_shape=jax.ShapeDtypeStruct((n_slots, n_out), out_dtype),
        compiler_params=_cparams("arbitrary", "arbitrary"),
        name="moe_gate_up" if swiglu else "moe_down",
    )(pstart, nblk, n_used, *args)


def _row_copy(src_hbm, row, dst, r, sem):
    return pltpu.make_async_copy(src_hbm.at[pl.ds(row, 1), :], dst.at[pl.ds(r, 1), :], sem)


def _combine_kernel(dest_ref, x_ref, gate_ref, g_ref, ys_hbm, o_ref, buf_ref, sem, *, norm):
    i = pl.program_id(0)
    tc = x_ref.shape[0]
    slot = i % 2

    def gather(step, sl):
        def start(r, c):
            for k in range(TOP_K):
                _row_copy(ys_hbm, dest_ref[k, step * tc + r], buf_ref.at[sl, k], r,
                          sem.at[sl]).start(priority=k % 2)
            return c
        lax.fori_loop(0, tc, start, 0)

    @pl.when(i == 0)
    def _():
        gather(i, slot)

    @pl.when(i + 1 < pl.num_programs(0))
    def _():
        gather(i + 1, 1 - slot)

    for k in range(TOP_K):
        pltpu.make_async_copy(ys_hbm.at[pl.ds(0, tc), :], buf_ref.at[slot, k], sem.at[slot]).wait()
    y = x_ref[...]
    for k in range(TOP_K):
        y = y + gate_ref[:, k:k + 1] * buf_ref[slot, k]
    o_ref[...] = _rms(y, g_ref[...]) if norm else y


def _combine(x1, gates_t, dest, ys, g_final, norm, tc=128):
    t, d = x1.shape
    tc = min(tc, t)
    return pl.pallas_call(
        functools.partial(_combine_kernel, norm=norm),
        grid_spec=pltpu.PrefetchScalarGridSpec(
            num_scalar_prefetch=1,
            grid=(t // tc,),
            in_specs=[pl.BlockSpec((tc, d), lambda i, dest: (i, 0)),
                      pl.BlockSpec((tc, TOP_K), lambda i, dest: (i, 0)),
                      pl.BlockSpec((1, d), lambda i, dest: (0, 0)),
                      pl.BlockSpec(memory_space=pl.ANY)],
            out_specs=pl.BlockSpec((tc, d), lambda i, dest: (i, 0)),
            scratch_shapes=[pltpu.VMEM((2, TOP_K, tc, d), F32), pltpu.SemaphoreType.DMA((2,))]),
        out_shape=jax.ShapeDtypeStruct((t, d), F32),
        compiler_params=_cparams("arbitrary"),
        name="moe_combine",
    )(dest, x1, gates_t, g_final, ys)


def _mixer(x, layer, g_mix, w_in, conv_w, conv_b, dt_bias, a_log, d_skip, g_ssm, w_out):
    w_bf_t, w_dt_t = _to_bf16_t(jnp.swapaxes(w_in, 1, 2), layer, ntail=SSM_HEADS)
    q, k, v, z, xbc, dt = _in_proj(x, g_mix.reshape(1, -1), w_bf_t, w_dt_t)
    attn = _attention(q, k, v)
    ssm = _ssd(z, xbc, dt, conv_w, conv_b, dt_bias, a_log, d_skip, g_ssm)
    return _out_proj(x, attn, ssm, _to_bf16(w_out, layer))


def _moe(x1, g_ffn, w_router, b_router, w_gate_up, b_gate_up, w_down, b_down, g_out, norm, tm=256):
    t, d = x1.shape
    g_ffn = g_ffn.reshape(1, -1)
    idx, gates, pos, cnt = _router(x1, g_ffn, w_router, b_router)
    counts = cnt[:, 0]
    padded = (counts + tm - 1) // tm * tm
    pend = jnp.cumsum(padded)
    pstart = pend - padded
    experts = jnp.arange(N_EXPERTS, dtype=jnp.int32)
    dest = pos + jnp.sum(jnp.where(idx[None] == experts[:, None, None],
                                   pstart[:, None, None], 0), axis=0)
    n_blocks = (t * TOP_K) // tm + N_EXPERTS
    n_used = (pend[-1] // tm).reshape(1)
    nblk = padded // tm

    xs = _dispatch(x1, g_ffn, dest, pstart + counts, pend, n_used, n_blocks, tm)
    act = _grouped_matmul(xs, w_gate_up, b_gate_up, pstart, nblk, n_used,
                          tm=tm, tn=2048, swiglu=True, out_dtype=BF16)
    ys = _grouped_matmul(act, w_down, b_down, pstart, nblk, n_used,
                         tm=tm, tn=2048, swiglu=False, out_dtype=F32)
    return _combine(x1, gates.T, dest, ys, g_out.reshape(1, -1), norm)


def kernel(x, g_mix, w_in, conv_w, conv_b, dt_bias, a_log, d_skip, g_ssm, w_out, g_ffn, w_router,
           b_router, w_gate_up, b_gate_up, w_down, b_down, g_final):
    b, s, d = x.shape
    depth = g_mix.shape[0]
    outs = []
    for bi in range(b):
        xb = x[bi]
        for l in range(depth):
            x1 = _mixer(xb, l, g_mix[l], w_in, conv_w[l], conv_b[l], dt_bias[l], a_log[l],
                        d_skip[l], g_ssm[l], w_out)
            xb = _moe(x1, g_ffn[l], w_router[l], b_router[l], w_gate_up[l], b_gate_up[l],
                      w_down[l], b_down[l], g_final, norm=(l == depth - 1))
        outs.append(xb)
    return outs[0][None] if b == 1 else jnp.stack(outs)
```

```python
import functools

import jax
import jax.numpy as jnp
from jax import lax
from jax.experimental import pallas as pl
from jax.experimental.pallas import tpu as pltpu

F32 = jnp.float32
BF16 = jnp.bfloat16
HIGHEST = lax.Precision.HIGHEST

LANES = 128
HEAD_DIM = 64
SB_WIDTH = 1024
SSM_INNER = 1024
SSM_GROUPS = 2
SSM_STATE = 128
SSM_CONV = 4
SSM_CONV_DIM = SSM_INNER + 2 * SSM_GROUPS * SSM_STATE
SSM_HEADS = SSM_INNER // HEAD_DIM
N_EXPERTS = 32
TOP_K = 4
SWIGLU_LIMIT = 7.0
SWIGLU_ALPHA = 1.702
RMS_EPS = 1e-5

VMEM_LIMIT = 56 * 1024 * 1024

ATTN_DEAD = -110.0

W_CHUNK_ROWS = 256


def _cparams(*sem):
    return pltpu.CompilerParams(dimension_semantics=sem, vmem_limit_bytes=VMEM_LIMIT)


def _rms(x, g):
    return x * lax.rsqrt(jnp.mean(x * x, axis=-1, keepdims=True) + RMS_EPS) * g


def _sigmoid(x):
    return 1.0 / (1.0 + jnp.exp(-x))


def _softplus(x):
    return jnp.maximum(x, 0.0) + jnp.log(1.0 + jnp.exp(-jnp.abs(x)))


def _cast_kernel(w_ref, o_ref):
    o_ref[...] = w_ref[...].astype(o_ref.dtype)


def _to_bf16(w, layer, rows=256):
    _, k, n = w.shape
    return pl.pallas_call(
        _cast_kernel,
        grid=(k // rows,),
        in_specs=[pl.BlockSpec((None, rows, n), lambda i: (layer, i, 0))],
        out_specs=pl.BlockSpec((rows, n), lambda i: (i, 0)),
        out_shape=jax.ShapeDtypeStruct((k, n), BF16),
        compiler_params=_cparams("arbitrary"),
        name="cast_bf16",
    )(w)


def _cast_t_kernel(w_ref, o_ref, tail_ref, *, ntail):
    w = w_ref[...]
    o_ref[...] = w.astype(o_ref.dtype)
    tail_ref[...] = jnp.concatenate(
        [w[w.shape[0] - ntail:, :], jnp.zeros((LANES - ntail, w.shape[1]), F32)], axis=0)


def _to_bf16_t(w_t, layer, ntail, cols=256):
    _, n, k = w_t.shape
    return pl.pallas_call(
        functools.partial(_cast_t_kernel, ntail=ntail),
        grid=(k // cols,),
        in_specs=[pl.BlockSpec((None, n, cols), lambda i: (layer, 0, i))],
        out_specs=[pl.BlockSpec((n, cols), lambda i: (0, i)),
                   pl.BlockSpec((LANES, cols), lambda i: (0, i))],
        out_shape=[jax.ShapeDtypeStruct((n, k), BF16), jax.ShapeDtypeStruct((LANES, k), F32)],
        compiler_params=_cparams("arbitrary"),
        name="cast_bf16_t",
    )(w_t)


def _in_proj_kernel(x_ref, g_ref, w_ref, wdt_ref, q_ref, k_ref, v_ref, z_ref, xbc_ref, dt_ref):
    hb = _rms(x_ref[...], g_ref[...]).astype(BF16)
    nt = (((1,), (1,)), ((), ()))

    def mm(a, b):
        return lax.dot_general(hb, w_ref[a:b, :], nt, preferred_element_type=F32)

    o = 0
    q_ref[...] = (mm(o, o + SB_WIDTH) * (HEAD_DIM ** -0.5)).astype(BF16)
    o += SB_WIDTH
    k_ref[...] = mm(o, o + SB_WIDTH).astype(BF16)
    o += SB_WIDTH
    v_ref[...] = mm(o, o + SB_WIDTH).astype(BF16)
    o += SB_WIDTH
    z_ref[...] = mm(o, o + SSM_INNER)
    o += SSM_INNER
    xbc_ref[...] = mm(o, o + SSM_CONV_DIM)
    dt_ref[...] = lax.dot_general(hb, wdt_ref[...].astype(BF16), nt, preferred_element_type=F32)


def _in_proj(x, g, w_bf_t, w_dt_t, tm=256):
    t, d = x.shape
    n = w_bf_t.shape[0]
    row = lambda w: pl.BlockSpec((tm, w), lambda i: (i, 0))
    outs = [(SB_WIDTH, BF16)] * 3 + [(SSM_INNER, F32), (SSM_CONV_DIM, F32), (LANES, F32)]
    return pl.pallas_call(
        _in_proj_kernel,
        grid=(t // tm,),
        in_specs=[row(d),
                  pl.BlockSpec((1, d), lambda i: (0, 0)),
                  pl.BlockSpec((n, d), lambda i: (0, 0), pipeline_mode=pl.Buffered(1)),
                  pl.BlockSpec((LANES, d), lambda i: (0, 0))],
        out_specs=[row(w) for w, _ in outs],
        out_shape=[jax.ShapeDtypeStruct((t, w), dt) for w, dt in outs],
        compiler_params=_cparams("arbitrary"),
        name="in_proj",
    )(x, g, w_bf_t, w_dt_t)


def _attn_kernel(q_ref, k_ref, v_ref, o_ref, acc_ref, carry_ref, *, tq):
    i = pl.program_id(1)
    nsub = tq // LANES
    qb0 = i * nsub
    row = lax.broadcasted_iota(jnp.int32, (LANES, LANES), 0)
    col = lax.broadcasted_iota(jnp.int32, (LANES, LANES), 1)
    past = col < row
    lo = col < HEAD_DIM
    jj = lax.broadcasted_iota(jnp.int32, (LANES, 2 * LANES), 0)
    ss = lax.broadcasted_iota(jnp.int32, (LANES, 2 * LANES), 1)
    neg_later = jnp.where((ss >= LANES) | (jj > ss), -1.0, 0.0).astype(BF16)
    nt = (((1,), (1,)), ((), ()))

    def step(n, diag):
        k0s = [pl.multiple_of(jnp.maximum(qb0 + s - n, 0) * LANES, LANES) for s in range(nsub)]
        chains = [(s, hd) for s in range(nsub) for hd in range(2)]
        logits = []
        for s, hd in chains:
            q = q_ref[s * LANES:(s + 1) * LANES, :]
            qh = jnp.where(lo if hd == 0 else jnp.logical_not(lo), q, jnp.zeros(q.shape, q.dtype))
            kt = k_ref[pl.ds(k0s[s], LANES), :]
            logits.append(lax.dot_general(qh, kt, nt, preferred_element_type=F32))
        logsig, sums = [], []
        for l in logits:
            sp = _softplus(l)
            lk = jnp.where(past, sp, 0.0) if diag else sp
            sums.append(jnp.dot(lk.astype(BF16), neg_later, preferred_element_type=F32))
            logsig.append(l - sp)
        top = None
        for c, (s, hd) in enumerate(chains):
            r = sums[c]
            if diag:
                w = jnp.where(past, jnp.exp(logsig[c] + r[:, :LANES]), 0.0)
                carry = r[:, LANES:]
            else:
                before = jnp.where(qb0 + s - n >= 0, carry_ref[c], -1e30)
                w = jnp.exp(logsig[c] + r[:, :LANES] + before)
                carry = before + r[:, LANES:]
            vt = v_ref[pl.ds(k0s[s], LANES), :]
            pv = jnp.dot(w.astype(BF16), vt, preferred_element_type=F32)
            if diag:
                acc_ref[c] = pv
            else:
                acc_ref[c] += pv
            carry_ref[c] = carry
            top = carry if top is None else jnp.maximum(top, carry)
        return jnp.max(top)

    top0 = step(0, True)

    def cond(st):
        n, top = st
        return jnp.logical_and(n <= qb0 + nsub - 1, top > ATTN_DEAD)

    def body(st):
        n, _ = st
        return n + 1, step(n, False)

    lax.while_loop(cond, body, (jnp.int32(1), top0))
    for s in range(nsub):
        o_ref[s * LANES:(s + 1) * LANES, :] = jnp.where(
            lo, acc_ref[2 * s], acc_ref[2 * s + 1]).astype(o_ref.dtype)


def _attention(q, k, v, tq=1024):
    t, w = q.shape
    tq = min(tq, t)
    chains = 2 * (tq // LANES)
    return pl.pallas_call(
        functools.partial(_attn_kernel, tq=tq),
        grid=(w // LANES, t // tq),
        in_specs=[pl.BlockSpec((tq, LANES), lambda p, i: (i, p)),
                  pl.BlockSpec((t, LANES), lambda p, i: (0, p)),
                  pl.BlockSpec((t, LANES), lambda p, i: (0, p))],
        out_specs=pl.BlockSpec((tq, LANES), lambda p, i: (i, p)),
        out_shape=jax.ShapeDtypeStruct((t, w), BF16),
        scratch_shapes=[pltpu.VMEM((chains, LANES, LANES), F32),
                        pltpu.VMEM((chains, LANES, LANES), F32)],
        compiler_params=_cparams("arbitrary", "arbitrary"),
        name="sb_attention",
    )(q, k, v)


def _ssd_kernel(z_ref, xbc_ref, dt_ref, cw_ref, cb_ref, dtb_ref, alog_ref, dsk_ref, g_ref, e_ref,
                o_ref, state_ref, xcat_ref):
    c = pl.program_id(0)
    L = z_ref.shape[0]
    half = SSM_INNER // SSM_GROUPS
    pad = 8

    @pl.when(c == 0)
    def _():
        state_ref[...] = jnp.zeros(state_ref.shape, F32)
        xcat_ref[0:pad, :] = jnp.zeros((pad, SSM_CONV_DIM), F32)

    xcat_ref[pad:pad + L, :] = xbc_ref[...]
    acc = jnp.broadcast_to(cb_ref[...], (L, SSM_CONV_DIM))
    for kk in range(SSM_CONV):
        s = pad - (SSM_CONV - 1) + kk
        acc = acc + cw_ref[kk:kk + 1, :] * xcat_ref[s:s + L, :]
    xcat_ref[0:pad, :] = xcat_ref[L:L + pad, :]
    xc = acc * _sigmoid(acc)
    xs = xc[:, :SSM_INNER]
    bm = xc[:, SSM_INNER:SSM_INNER + SSM_GROUPS * SSM_STATE]
    cm = xc[:, SSM_INNER + SSM_GROUPS * SSM_STATE:]

    dtv = _softplus(dt_ref[...] + dtb_ref[...])
    adt = dtv * (-jnp.exp(alog_ref[...]))
    ri = lax.broadcasted_iota(jnp.int32, (L, L), 0)
    ci = lax.broadcasted_iota(jnp.int32, (L, L), 1)
    causal = ri >= ci
    tri = jnp.where(causal, 1.0, 0.0).astype(F32)
    acum = jnp.dot(tri, adt, precision=HIGHEST, preferred_element_type=F32)
    expand = e_ref[...]
    acx = jnp.dot(acum, expand, precision=HIGHEST, preferred_element_type=F32)
    dtx = jnp.dot(dtv, expand, precision=HIGHEST, preferred_element_type=F32)
    alx = acx[L - 1:L, :]
    xdt = xs * dtx
    acum_t = acum.T
    lo = lax.broadcasted_iota(jnp.int32, (L, LANES), 1) < HEAD_DIM
    nt = (((1,), (1,)), ((), ()))

    xdt_b = xdt.astype(BF16)
    y_parts = []
    for g in range(SSM_GROUPS):
        cg = cm[:, g * SSM_STATE:(g + 1) * SSM_STATE].astype(BF16)
        bg = bm[:, g * SSM_STATE:(g + 1) * SSM_STATE]
        y_off = jnp.dot(cg, state_ref[:, g * half:(g + 1) * half].astype(BF16),
                        preferred_element_type=F32)
        cb = lax.dot_general(cg, bg.astype(BF16), nt, preferred_element_type=F32)
        for p in range(half // LANES):
            pair = g * (half // LANES) + p
            xp = xdt_b[:, pair * LANES:(pair + 1) * LANES]
            yh = []
            for hh in range(2):
                h = 2 * pair + hh
                seg = acum[:, h:h + 1] - acum_t[h:h + 1, :]
                dec = jnp.where(causal, jnp.exp(jnp.minimum(seg, 0.0)), 0.0)
                yh.append(jnp.dot((cb * dec).astype(BF16), xp, preferred_element_type=F32))
            y_parts.append(jnp.where(lo, yh[0], yh[1])
                           + y_off[:, p * LANES:(p + 1) * LANES]
                           * jnp.exp(acx[:, pair * LANES:(pair + 1) * LANES]))
        xd = xdt[:, g * half:(g + 1) * half] * jnp.exp(alx[:, g * half:(g + 1) * half]
                                                       - acx[:, g * half:(g + 1) * half])
        upd = jnp.dot(bg.T.astype(BF16), xd.astype(BF16), preferred_element_type=F32)
        state_ref[:, g * half:(g + 1) * half] = (
            jnp.exp(alx[:, g * half:(g + 1) * half]) * state_ref[:, g * half:(g + 1) * half] + upd)

    y = jnp.concatenate(y_parts, axis=1) + dsk_ref[...] * xs
    zz = z_ref[...]
    y = y * (zz * _sigmoid(zz))
    outs = []
    for g in range(SSM_GROUPS):
        yg = y[:, g * half:(g + 1) * half]
        outs.append(yg * lax.rsqrt(jnp.mean(yg * yg, axis=-1, keepdims=True) + RMS_EPS))
    o_ref[...] = (jnp.concatenate(outs, axis=1) * g_ref[...]).astype(o_ref.dtype)


def _ssd(z, xbc, dt, conv_w, conv_b, dt_bias, a_log, d_skip, g_ssm, chunk=128):
    t = z.shape[0]
    chunk = min(chunk, t)
    padh = lambda a: jnp.pad(a.reshape(1, -1), ((0, 0), (0, LANES - a.shape[-1])))
    heads = jnp.arange(SSM_INNER, dtype=jnp.int32) // HEAD_DIM
    expand = (jnp.arange(LANES, dtype=jnp.int32)[:, None] == heads[None, :]).astype(F32)
    row = lambda w: pl.BlockSpec((chunk, w), lambda c: (c, 0))
    full = lambda a: pl.BlockSpec(a.shape, lambda c: (0, 0))
    params = [conv_w, conv_b.reshape(1, -1), padh(dt_bias), padh(a_log),
              jnp.repeat(d_skip, HEAD_DIM).reshape(1, -1), g_ssm.reshape(1, -1), expand]
    return pl.pallas_call(
        _ssd_kernel,
        grid=(t // chunk,),
        in_specs=[row(SSM_INNER), row(SSM_CONV_DIM), row(LANES)] + [full(a) for a in params],
        out_specs=row(SSM_INNER),
        out_shape=jax.ShapeDtypeStruct((t, SSM_INNER), BF16),
        scratch_shapes=[pltpu.VMEM((SSM_STATE, SSM_INNER), F32),
                        pltpu.VMEM((chunk + 8, SSM_CONV_DIM), F32)],
        compiler_params=_cparams("arbitrary"),
        name="ssd_mixer",
    )(z, xbc, dt, *params)


def _out_proj_kernel(x_ref, a_ref, s_ref, w_ref, o_ref):
    o_ref[...] = (x_ref[...]
                  + jnp.dot(a_ref[...], w_ref[0:SB_WIDTH, :], preferred_element_type=F32)
                  + jnp.dot(s_ref[...], w_ref[SB_WIDTH:, :], preferred_element_type=F32))


def _out_proj(x, attn, ssm, w_bf, tm=512):
    t, d = x.shape
    tm = min(tm, t)
    return pl.pallas_call(
        _out_proj_kernel,
        grid=(t // tm,),
        in_specs=[pl.BlockSpec((tm, d), lambda i: (i, 0)),
                  pl.BlockSpec((tm, SB_WIDTH), lambda i: (i, 0)),
                  pl.BlockSpec((tm, SSM_INNER), lambda i: (i, 0)),
                  pl.BlockSpec(w_bf.shape, lambda i: (0, 0), pipeline_mode=pl.Buffered(1))],
        out_specs=pl.BlockSpec((tm, d), lambda i: (i, 0)),
        out_shape=jax.ShapeDtypeStruct((t, d), F32),
        compiler_params=_cparams("arbitrary"),
        name="out_proj",
    )(x, attn, ssm, w_bf)


def _router_kernel(x_ref, g_ref, wt_ref, b_ref, idx_ref, gate_ref, pos_ref, cnt_ref, base_ref):
    i = pl.program_id(0)
    tm = x_ref.shape[0]

    @pl.when(i == 0)
    def _():
        base_ref[...] = jnp.zeros(base_ref.shape, F32)

    h = _rms(x_ref[...], g_ref[...])
    logits = lax.dot_general(wt_ref[...], h, (((1,), (1,)), ((), ())), precision=HIGHEST,
                             preferred_element_type=F32) + b_ref[:, 0:1]
    eio = lax.broadcasted_iota(jnp.int32, (N_EXPERTS, tm), 0).astype(F32)
    work = logits
    vals, hots = [], []
    for k in range(TOP_K):
        m = jnp.max(work, axis=0, keepdims=True)
        idx = jnp.min(jnp.where(work == m, eio, float(N_EXPERTS)), axis=0, keepdims=True)
        hot = eio == idx
        work = jnp.where(hot, -jnp.inf, work)
        vals.append(m)
        hots.append(hot)
        idx_ref[k:k + 1, :] = idx.astype(jnp.int32)
    ex = [jnp.exp(v - vals[0]) for v in vals]
    den = ex[0] + ex[1] + ex[2] + ex[3]
    picked = jnp.zeros((N_EXPERTS, tm), F32)
    for k in range(TOP_K):
        gate_ref[k:k + 1, :] = ex[k] / den
        picked = picked + jnp.where(hots[k], 1.0, 0.0)
    ti = lax.broadcasted_iota(jnp.int32, (tm, tm), 0)
    tj = lax.broadcasted_iota(jnp.int32, (tm, tm), 1)
    before = jnp.where(ti < tj, 1.0, 0.0).astype(BF16)
    rank = jnp.dot(picked.astype(BF16), before, preferred_element_type=F32) + base_ref[:, 0:1]
    for k in range(TOP_K):
        pos = jnp.sum(jnp.where(hots[k], rank, 0.0), axis=0, keepdims=True)
        pos_ref[k:k + 1, :] = pos.astype(jnp.int32)
    base_ref[...] = base_ref[...] + jnp.sum(picked, axis=1, keepdims=True)
    cnt_ref[...] = base_ref[...].astype(jnp.int32)


def _router(x1, g, w_router, b_router, tm=512):
    t, d = x1.shape
    tm = min(tm, t)
    wt = w_router.T
    b = jnp.broadcast_to(b_router.reshape(-1, 1), (N_EXPERTS, LANES))
    tok = lambda dt: (pl.BlockSpec((TOP_K, tm), lambda i: (0, i)), jax.ShapeDtypeStruct((TOP_K, t), dt))
    specs = [tok(jnp.int32), tok(F32), tok(jnp.int32),
             (pl.BlockSpec((N_EXPERTS, LANES), lambda i: (0, 0)),
              jax.ShapeDtypeStruct((N_EXPERTS, LANES), jnp.int32))]
    return pl.pallas_call(
        _router_kernel,
        grid=(t // tm,),
        in_specs=[pl.BlockSpec((tm, d), lambda i: (i, 0)),
                  pl.BlockSpec((1, d), lambda i: (0, 0)),
                  pl.BlockSpec((N_EXPERTS, d), lambda i: (0, 0)),
                  pl.BlockSpec((N_EXPERTS, LANES), lambda i: (0, 0))],
        out_specs=[s for s, _ in specs],
        out_shape=[s for _, s in specs],
        scratch_shapes=[pltpu.VMEM((N_EXPERTS, LANES), F32)],
        compiler_params=_cparams("arbitrary"),
        name="router",
    )(x1, g, wt, b)


def _dispatch_kernel(dest_ref, lo_ref, hi_ref, nu_ref, x_ref, g_ref, xs_hbm, hbuf, zbuf, sem, zsem,
                     *, n_blocks):
    i = pl.program_id(0)
    last = pl.num_programs(0) - 1
    tm = x_ref.shape[0]
    slot = i % 2

    def row_copies(step, sl, r):
        return [pltpu.make_async_copy(hbuf.at[sl, pl.ds(r, 1), :],
                                      xs_hbm.at[pl.ds(dest_ref[k, step * tm + r], 1), :], sem.at[sl])
                for k in range(TOP_K)]

    def wait_step(sl):
        for _ in range(TOP_K):
            pltpu.make_async_copy(hbuf.at[sl], xs_hbm.at[pl.ds(0, tm), :], sem.at[sl]).wait()

    @pl.when(i >= 2)
    def _():
        wait_step(slot)

    hbuf[slot] = _rms(x_ref[...], g_ref[...])

    def start(r, c):
        for k, cp in enumerate(row_copies(i, slot, r)):
            cp.start(priority=k % 2)
        return c
    lax.fori_loop(0, tm, start, 0, unroll=8)

    @pl.when(i == 0)
    def _():
        zbuf[...] = jnp.zeros(zbuf.shape, F32)

        def pad_row(r):
            return pltpu.make_async_copy(zbuf.at[pl.ds(0, 1), :], xs_hbm.at[pl.ds(r, 1), :], zsem.at[0])

        def tail_block(b):
            return pltpu.make_async_copy(zbuf, xs_hbm.at[pl.ds(pl.multiple_of(b * tm, tm), tm), :],
                                         zsem.at[0])

        def each(fn):
            def per_expert(e, c):
                return lax.fori_loop(lo_ref[e], hi_ref[e], lambda r, c2: (fn(pad_row(r)), c2)[1], c)
            lax.fori_loop(0, N_EXPERTS, per_expert, 0)
            lax.fori_loop(nu_ref[0], n_blocks, lambda b, c: (fn(tail_block(b)), c)[1], 0)

        each(lambda cp: cp.start())
        each(lambda cp: cp.wait())

    @pl.when(i == last)
    def _():
        @pl.when(i >= 1)
        def _():
            wait_step(1 - slot)
        wait_step(slot)


def _dispatch(x1, g, dest, pad_lo, pad_hi, n_used, n_blocks, tm):
    t, d = x1.shape
    return pl.pallas_call(
        functools.partial(_dispatch_kernel, n_blocks=n_blocks),
        grid_spec=pltpu.PrefetchScalarGridSpec(
            num_scalar_prefetch=4,
            grid=(t // tm,),
            in_specs=[pl.BlockSpec((tm, d), lambda i, *_: (i, 0)),
                      pl.BlockSpec((1, d), lambda i, *_: (0, 0))],
            out_specs=pl.BlockSpec(memory_space=pl.ANY),
            scratch_shapes=[pltpu.VMEM((2, tm, d), F32), pltpu.VMEM((tm, d), F32),
                            pltpu.SemaphoreType.DMA((2,)), pltpu.SemaphoreType.DMA((1,))]),
        out_shape=jax.ShapeDtypeStruct((n_blocks * tm, d), F32),
        compiler_params=_cparams("arbitrary"),
        name="moe_dispatch",
    )(dest, pad_lo, pad_hi, n_used, x1, g)


def _swiglu_pick(y, pick):
    tm, n2 = y.shape
    even = lax.broadcasted_iota(jnp.int32, (tm, 2 * LANES), 1) % 2 == 0
    outs = []
    for cblk in range(n2 // (2 * LANES)):
        yc = y[:, cblk * 2 * LANES:(cblk + 1) * 2 * LANES]
        capped = jnp.minimum(yc, SWIGLU_LIMIT)
        gate = capped * _sigmoid(SWIGLU_ALPHA * capped)
        up = jnp.maximum(capped, -SWIGLU_LIMIT) + 1.0
        factors = jnp.where(even, gate, up).astype(BF16)
        split = jnp.dot(factors, pick, preferred_element_type=F32)
        outs.append(split[:, :LANES] * split[:, LANES:])
    return jnp.concatenate(outs, axis=1)


def _gmm_kernel(ps_ref, nb_ref, nu_ref, b_ref, *rest, swiglu, tm, n_blocks):
    if swiglu:
        pick_ref, w_hbm, x_hbm, o_hbm, wbuf, wbf, xbuf, obuf, zbuf, wsem, sin, sout, zsem = rest
    else:
        w_hbm, x_hbm, o_hbm, wbuf, wbf, xbuf, obuf, zbuf, wsem, sin, sout, zsem = rest
    e = pl.program_id(0)
    j = pl.program_id(1)
    n_exp = pl.num_programs(0)
    nj = pl.num_programs(1)
    step = e * nj + j
    have_next = step < n_exp * nj - 1
    nb = nb_ref[e]
    g0 = nj * (ps_ref[e] // tm) + j * nb
    e_next = jnp.minimum(jnp.where(j + 1 < nj, e, e + 1), n_exp - 1)
    e_prev = jnp.maximum(jnp.where(j > 0, e, e - 1), 0)
    next_has = jnp.logical_and(have_next, nb_ref[e_next] > 0)
    first_here = jnp.logical_or(step == 0, nb_ref[e_prev] == 0)
    kdim, tn = wbf.shape
    tn_out = obuf.shape[2]
    col = pl.multiple_of(j * tn_out, tn_out)
    nch = kdim // W_CHUNK_ROWS

    def w_chunk(st, c):
        r = pl.ds(pl.multiple_of(c * W_CHUNK_ROWS, W_CHUNK_ROWS), W_CHUNK_ROWS)
        cols = pl.ds(pl.multiple_of((st % nj) * tn, tn), tn)
        return pltpu.make_async_copy(w_hbm.at[st // nj, r, cols], wbuf.at[r, :], wsem.at[0])

    def start_chunks(st, lo, hi):
        lax.fori_loop(lo, hi, lambda c, x: (w_chunk(st, c).start(), x)[1], 0)

    def x_copy(row0, sl):
        return pltpu.make_async_copy(x_hbm.at[pl.ds(pl.multiple_of(row0, tm), tm), :], xbuf.at[sl],
                                     sin.at[sl])

    def o_copy(b, sl):
        r = pl.ds(pl.multiple_of(ps_ref[e] + b * tm, tm), tm)
        return pltpu.make_async_copy(obuf.at[sl], o_hbm.at[r, pl.ds(col, tn_out)], sout.at[sl])

    @pl.when(step == 0)
    def _():
        start_chunks(step, 0, nch)

    @pl.when(jnp.logical_and(nb > 0, first_here))
    def _():
        x_copy(ps_ref[e], g0 % 2).start()

    lax.fori_loop(0, nch, lambda c, x: (w_chunk(step, c).wait(), x)[1], 0)

    @pl.when(nb > 0)
    def _():
        wbf[...] = wbuf[...].astype(BF16)

    per_block = (nch + jnp.maximum(nb, 1) - 1) // jnp.maximum(nb, 1)

    def block(b, c):
        sl = (g0 + b) % 2
        more = b + 1 < nb

        @pl.when(jnp.logical_or(more, next_has))
        def _():
            x_copy(jnp.where(more, ps_ref[e] + (b + 1) * tm, ps_ref[e_next]), 1 - sl).start()

        @pl.when(have_next)
        def _():
            start_chunks(step + 1, jnp.minimum(b * per_block, nch),
                         jnp.minimum((b + 1) * per_block, nch))

        x_copy(ps_ref[e] + b * tm, sl).wait()

        @pl.when(g0 + b >= 2)
        def _():
            o_copy(b, sl).wait()

        y = jnp.dot(xbuf[sl].astype(BF16), wbf[...], preferred_element_type=F32) + b_ref[...]
        if swiglu:
            y = _swiglu_pick(y, pick_ref[...])
        obuf[sl] = y.astype(obuf.dtype)
        o_copy(b, sl).start()
        return c

    lax.fori_loop(0, nb, block, 0)

    @pl.when(jnp.logical_and(have_next, nb == 0))
    def _():
        start_chunks(step + 1, 0, nch)

    @pl.when(e == n_exp - 1)
    def _():
        zbuf[...] = jnp.zeros(zbuf.shape, zbuf.dtype)

        def tail(b):
            return pltpu.make_async_copy(
                zbuf, o_hbm.at[pl.ds(pl.multiple_of(b * tm, tm), tm), pl.ds(col, tn_out)], zsem.at[0])

        lax.fori_loop(nu_ref[0], n_blocks, lambda b, c: (tail(b).start(), c)[1], 0)
        lax.fori_loop(nu_ref[0], n_blocks, lambda b, c: (tail(b).wait(), c)[1], 0)

    @pl.when(jnp.logical_not(have_next))
    def _():
        done = g0 + nb

        @pl.when(done >= 2)
        def _():
            o_copy(0, done % 2).wait()

        @pl.when(done >= 1)
        def _():
            o_copy(0, (done + 1) % 2).wait()


def _grouped_matmul(x, w, b, pstart, nblk, n_used, *, tm, tn, swiglu, out_dtype):
    n_slots, kdim = x.shape
    n_exp, _, n = w.shape
    n_out = n // 2 if swiglu else n
    tn_out = tn // 2 if swiglu else tn
    in_specs = [pl.BlockSpec((None, 1, tn), lambda e, j, *_: (e, 0, j))]
    args = [b.reshape(n_exp, 1, n)]
    if swiglu:
        src = jnp.arange(2 * LANES, dtype=jnp.int32)[:, None]
        dst = jnp.arange(2 * LANES, dtype=jnp.int32)[None, :]
        args.append((src == 2 * (dst % LANES) + dst // LANES).astype(BF16))
        in_specs.append(pl.BlockSpec((2 * LANES, 2 * LANES), lambda e, j, *_: (0, 0)))
    in_specs += [pl.BlockSpec(memory_space=pl.ANY)] * 2
    args += [w, x]
    return pl.pallas_call(
        functools.partial(_gmm_kernel, swiglu=swiglu, tm=tm, n_blocks=n_slots // tm),
        grid_spec=pltpu.PrefetchScalarGridSpec(
            num_scalar_prefetch=3,
            grid=(n_exp, n // tn),
            in_specs=in_specs,
            out_specs=pl.BlockSpec(memory_space=pl.ANY),
            scratch_shapes=[pltpu.VMEM((kdim, tn), F32), pltpu.VMEM((kdim, tn), BF16),
                            pltpu.VMEM((2, tm, kdim), x.dtype), pltpu.VMEM((2, tm, tn_out), out_dtype),
                            pltpu.VMEM((tm, tn_out), out_dtype),
                            pltpu.SemaphoreType.DMA((1,)), pltpu.SemaphoreType.DMA((2,)),
                            pltpu.SemaphoreType.DMA((2,)), pltpu.SemaphoreType.DMA((1,))]),
        out_shape=jax.ShapeDtypeStruct((n_slots, n_out), out_dtype),
        compiler_params=_cparams("arbitrary", "arbitrary"),
        name="moe_gate_up" if swiglu else "moe_down",
    )(pstart, nblk, n_used, *args)


def _row_copy(src_hbm, row, dst, r, sem):
    return pltpu.make_async_copy(src_hbm.at[pl.ds(row, 1), :], dst.at[pl.ds(r, 1), :], sem)


def _combine_kernel(dest_ref, x_ref, gate_ref, g_ref, ys_hbm, o_ref, buf_ref, sem, *, norm):
    i = pl.program_id(0)
    tc = x_ref.shape[0]
    slot = i % 2

    def gather(step, sl):
        def start(r, c):
            for k in range(TOP_K):
                _row_copy(ys_hbm, dest_ref[k, step * tc + r], buf_ref.at[sl, k], r,
                          sem.at[sl]).start(priority=k % 2)
            return c
        lax.fori_loop(0, tc, start, 0, unroll=8)

    @pl.when(i == 0)
    def _():
        gather(i, slot)

    @pl.when(i + 1 < pl.num_programs(0))
    def _():
        gather(i + 1, 1 - slot)

    for k in range(TOP_K):
        pltpu.make_async_copy(ys_hbm.at[pl.ds(0, tc), :], buf_ref.at[slot, k], sem.at[slot]).wait()
    y = x_ref[...]
    for k in range(TOP_K):
        y = y + gate_ref[:, k:k + 1] * buf_ref[slot, k]
    o_ref[...] = _rms(y, g_ref[...]) if norm else y


def _combine(x1, gates_t, dest, ys, g_final, norm, tc=128):
    t, d = x1.shape
    tc = min(tc, t)
    return pl.pallas_call(
        functools.partial(_combine_kernel, norm=norm),
        grid_spec=pltpu.PrefetchScalarGridSpec(
            num_scalar_prefetch=1,
            grid=(t // tc,),
            in_specs=[pl.BlockSpec((tc, d), lambda i, dest: (i, 0)),
                      pl.BlockSpec((tc, TOP_K), lambda i, dest: (i, 0)),
                      pl.BlockSpec((1, d), lambda i, dest: (0, 0)),
                      pl.BlockSpec(memory_space=pl.ANY)],
            out_specs=pl.BlockSpec((tc, d), lambda i, dest: (i, 0)),
            scratch_shapes=[pltpu.VMEM((2, TOP_K, tc, d), F32), pltpu.SemaphoreType.DMA((2,))]),
        out_shape=jax.ShapeDtypeStruct((t, d), F32),
        compiler_params=_cparams("arbitrary"),
        name="moe_combine",
    )(dest, x1, gates_t, g_final, ys)


def _mixer(x, layer, g_mix, w_in, conv_w, conv_b, dt_bias, a_log, d_skip, g_ssm, w_out):
    w_bf_t, w_dt_t = _to_bf16_t(jnp.swapaxes(w_in, 1, 2), layer, ntail=SSM_HEADS)
    q, k, v, z, xbc, dt = _in_proj(x, g_mix.reshape(1, -1), w_bf_t, w_dt_t)
    attn = _attention(q, k, v)
    ssm = _ssd(z, xbc, dt, conv_w, conv_b, dt_bias, a_log, d_skip, g_ssm)
    return _out_proj(x, attn, ssm, _to_bf16(w_out, layer))


def _moe(x1, g_ffn, w_router, b_router, w_gate_up, b_gate_up, w_down, b_down, g_out, norm, tm=256):
    t, d = x1.shape
    g_ffn = g_ffn.reshape(1, -1)
    idx, gates, pos, cnt = _router(x1, g_ffn, w_router, b_router)
    counts = cnt[:, 0]
    padded = (counts + tm - 1) // tm * tm
    pend = jnp.cumsum(padded)
    pstart = pend - padded
    experts = jnp.arange(N_EXPERTS, dtype=jnp.int32)
    dest = pos + jnp.sum(jnp.where(idx[None] == experts[:, None, None],
                                   pstart[:, None, None], 0), axis=0)
    n_blocks = (t * TOP_K) // tm + N_EXPERTS
    n_used = (pend[-1] // tm).reshape(1)
    nblk = padded // tm

    xs = _dispatch(x1, g_ffn, dest, pstart + counts, pend, n_used, n_blocks, tm)
    act = _grouped_matmul(xs, w_gate_up, b_gate_up, pstart, nblk, n_used,
                          tm=tm, tn=2048, swiglu=True, out_dtype=BF16)
    ys = _grouped_matmul(act, w_down, b_down, pstart, nblk, n_used,
                         tm=tm, tn=2048, swiglu=False, out_dtype=F32)
    return _combine(x1, gates.T, dest, ys, g_out.reshape(1, -1), norm)


def kernel(x, g_mix, w_in, conv_w, conv_b, dt_bias, a_log, d_skip, g_ssm, w_out, g_ffn, w_router,
           b_router, w_gate_up, b_gate_up, w_down, b_down, g_final):
    b, s, d = x.shape
    depth = g_mix.shape[0]
    outs = []
    for bi in range(b):
        xb = x[bi]
        for l in range(depth):
            x1 = _mixer(xb, l, g_mix[l], w_in, conv_w[l], conv_b[l], dt_bias[l], a_log[l],
                        d_skip[l], g_ssm[l], w_out)
            xb = _moe(x1, g_ffn[l], w_router[l], b_router[l], w_gate_up[l], b_gate_up[l],
                      w_down[l], b_down[l], g_final, norm=(l == depth - 1))
        outs.append(xb)
    return outs[0][None] if b == 1 else jnp.stack(outs)
```

```python
import functools

import jax
import jax.numpy as jnp
from jax import lax
from jax.experimental import pallas as pl
from jax.experimental.pallas import tpu as pltpu

F32 = jnp.float32
BF16 = jnp.bfloat16
HIGHEST = lax.Precision.HIGHEST

LANES = 128
HEAD_DIM = 64
SB_WIDTH = 1024
SSM_INNER = 1024
SSM_GROUPS = 2
SSM_STATE = 128
SSM_CONV = 4
SSM_CONV_DIM = SSM_INNER + 2 * SSM_GROUPS * SSM_STATE
SSM_HEADS = SSM_INNER // HEAD_DIM
N_EXPERTS = 32
TOP_K = 4
SWIGLU_LIMIT = 7.0
SWIGLU_ALPHA = 1.702
RMS_EPS = 1e-5

VMEM_LIMIT = 56 * 1024 * 1024

ATTN_DEAD = -110.0

W_CHUNK_ROWS = 256


def _cparams(*sem):
    return pltpu.CompilerParams(dimension_semantics=sem, vmem_limit_bytes=VMEM_LIMIT)


def _rms(x, g):
    return x * lax.rsqrt(jnp.mean(x * x, axis=-1, keepdims=True) + RMS_EPS) * g


def _sigmoid(x):
    return 1.0 / (1.0 + jnp.exp(-x))


def _softplus(x):
    return jnp.maximum(x, 0.0) + jnp.log(1.0 + jnp.exp(-jnp.abs(x)))


def _cast_kernel(w_ref, o_ref):
    o_ref[...] = w_ref[...].astype(o_ref.dtype)


def _to_bf16(w, layer, rows=256):
    _, k, n = w.shape
    return pl.pallas_call(
        _cast_kernel,
        grid=(k // rows,),
        in_specs=[pl.BlockSpec((None, rows, n), lambda i: (layer, i, 0))],
        out_specs=pl.BlockSpec((rows, n), lambda i: (i, 0)),
        out_shape=jax.ShapeDtypeStruct((k, n), BF16),
        compiler_params=_cparams("arbitrary"),
        name="cast_bf16",
    )(w)


def _cast_t_kernel(w_ref, o_ref, tail_ref, *, ntail):
    w = w_ref[...]
    o_ref[...] = w.astype(o_ref.dtype)
    tail_ref[...] = jnp.concatenate(
        [w[w.shape[0] - ntail:, :], jnp.zeros((LANES - ntail, w.shape[1]), F32)], axis=0)


def _to_bf16_t(w_t, layer, ntail, cols=256):
    _, n, k = w_t.shape
    return pl.pallas_call(
        functools.partial(_cast_t_kernel, ntail=ntail),
        grid=(k // cols,),
        in_specs=[pl.BlockSpec((None, n, cols), lambda i: (layer, 0, i))],
        out_specs=[pl.BlockSpec((n, cols), lambda i: (0, i)),
                   pl.BlockSpec((LANES, cols), lambda i: (0, i))],
        out_shape=[jax.ShapeDtypeStruct((n, k), BF16), jax.ShapeDtypeStruct((LANES, k), F32)],
        compiler_params=_cparams("arbitrary"),
        name="cast_bf16_t",
    )(w_t)


def _in_proj_kernel(x_ref, g_ref, w_ref, wdt_ref, q_ref, k_ref, v_ref, z_ref, xbc_ref, dt_ref):
    hb = _rms(x_ref[...], g_ref[...]).astype(BF16)
    nt = (((1,), (1,)), ((), ()))

    def mm(a, b):
        return lax.dot_general(hb, w_ref[a:b, :], nt, preferred_element_type=F32)

    o = 0
    q_ref[...] = (mm(o, o + SB_WIDTH) * (HEAD_DIM ** -0.5)).astype(BF16)
    o += SB_WIDTH
    k_ref[...] = mm(o, o + SB_WIDTH).astype(BF16)
    o += SB_WIDTH
    v_ref[...] = mm(o, o + SB_WIDTH).astype(BF16)
    o += SB_WIDTH
    z_ref[...] = mm(o, o + SSM_INNER)
    o += SSM_INNER
    xbc_ref[...] = mm(o, o + SSM_CONV_DIM)
    dt_ref[...] = lax.dot_general(hb, wdt_ref[...].astype(BF16), nt, preferred_element_type=F32)


def _in_proj(x, g, w_bf_t, w_dt_t, tm=256):
    t, d = x.shape
    n = w_bf_t.shape[0]
    row = lambda w: pl.BlockSpec((tm, w), lambda i: (i, 0))
    outs = [(SB_WIDTH, BF16)] * 3 + [(SSM_INNER, F32), (SSM_CONV_DIM, F32), (LANES, F32)]
    return pl.pallas_call(
        _in_proj_kernel,
        grid=(t // tm,),
        in_specs=[row(d),
                  pl.BlockSpec((1, d), lambda i: (0, 0)),
                  pl.BlockSpec((n, d), lambda i: (0, 0), pipeline_mode=pl.Buffered(1)),
                  pl.BlockSpec((LANES, d), lambda i: (0, 0))],
        out_specs=[row(w) for w, _ in outs],
        out_shape=[jax.ShapeDtypeStruct((t, w), dt) for w, dt in outs],
        compiler_params=_cparams("arbitrary"),
        name="in_proj",
    )(x, g, w_bf_t, w_dt_t)


def _attn_kernel(q_ref, k_ref, v_ref, o_ref, acc_ref, carry_ref, *, tq):
    i = pl.program_id(1)
    nsub = tq // LANES
    qb0 = i * nsub
    row = lax.broadcasted_iota(jnp.int32, (LANES, LANES), 0)
    col = lax.broadcasted_iota(jnp.int32, (LANES, LANES), 1)
    past = col < row
    lo = col < HEAD_DIM
    jj = lax.broadcasted_iota(jnp.int32, (LANES, 2 * LANES), 0)
    ss = lax.broadcasted_iota(jnp.int32, (LANES, 2 * LANES), 1)
    neg_later = jnp.where((ss >= LANES) | (jj > ss), -1.0, 0.0).astype(BF16)
    nt = (((1,), (1,)), ((), ()))

    def step(n, diag):
        k0s = [pl.multiple_of(jnp.maximum(qb0 + s - n, 0) * LANES, LANES) for s in range(nsub)]
        chains = [(s, hd) for s in range(nsub) for hd in range(2)]
        logits = []
        for s, hd in chains:
            q = q_ref[s * LANES:(s + 1) * LANES, :]
            qh = jnp.where(lo if hd == 0 else jnp.logical_not(lo), q, jnp.zeros(q.shape, q.dtype))
            kt = k_ref[pl.ds(k0s[s], LANES), :]
            logits.append(lax.dot_general(qh, kt, nt, preferred_element_type=F32))
        logsig, sums = [], []
        for l in logits:
            sp = _softplus(l)
            lk = jnp.where(past, sp, 0.0) if diag else sp
            sums.append(jnp.dot(lk.astype(BF16), neg_later, preferred_element_type=F32))
            logsig.append(l - sp)
        top = None
        for c, (s, hd) in enumerate(chains):
            r = sums[c]
            if diag:
                w = jnp.where(past, jnp.exp(logsig[c] + r[:, :LANES]), 0.0)
                carry = r[:, LANES:]
            else:
                before = jnp.where(qb0 + s - n >= 0, carry_ref[c], -1e30)
                w = jnp.exp(logsig[c] + r[:, :LANES] + before)
                carry = before + r[:, LANES:]
            vt = v_ref[pl.ds(k0s[s], LANES), :]
            pv = jnp.dot(w.astype(BF16), vt, preferred_element_type=F32)
            if diag:
                acc_ref[c] = pv
            else:
                acc_ref[c] += pv
            carry_ref[c] = carry
            top = carry if top is None else jnp.maximum(top, carry)
        return jnp.max(top)

    top0 = step(0, True)

    def cond(st):
        n, top = st
        return jnp.logical_and(n <= qb0 + nsub - 1, top > ATTN_DEAD)

    def body(st):
        n, _ = st
        return n + 1, step(n, False)

    lax.while_loop(cond, body, (jnp.int32(1), top0))
    for s in range(nsub):
        o_ref[s * LANES:(s + 1) * LANES, :] = jnp.where(
            lo, acc_ref[2 * s], acc_ref[2 * s + 1]).astype(o_ref.dtype)


def _attention(q, k, v, tq=1024):
    t, w = q.shape
    tq = min(tq, t)
    chains = 2 * (tq // LANES)
    return pl.pallas_call(
        functools.partial(_attn_kernel, tq=tq),
        grid=(w // LANES, t // tq),
        in_specs=[pl.BlockSpec((tq, LANES), lambda p, i: (i, p)),
                  pl.BlockSpec((t, LANES), lambda p, i: (0, p)),
                  pl.BlockSpec((t, LANES), lambda p, i: (0, p))],
        out_specs=pl.BlockSpec((tq, LANES), lambda p, i: (i, p)),
        out_shape=jax.ShapeDtypeStruct((t, w), BF16),
        scratch_shapes=[pltpu.VMEM((chains, LANES, LANES), F32),
                        pltpu.VMEM((chains, LANES, LANES), F32)],
        compiler_params=_cparams("arbitrary", "arbitrary"),
        name="sb_attention",
    )(q, k, v)


def _ssd_kernel(z_ref, xbc_ref, dt_ref, cw_ref, cb_ref, dtb_ref, alog_ref, dsk_ref, g_ref, e_ref,
                o_ref, state_ref, xcat_ref):
    c = pl.program_id(0)
    L = z_ref.shape[0]
    half = SSM_INNER // SSM_GROUPS
    pad = 8

    @pl.when(c == 0)
    def _():
        state_ref[...] = jnp.zeros(state_ref.shape, F32)
        xcat_ref[0:pad, :] = jnp.zeros((pad, SSM_CONV_DIM), F32)

    xcat_ref[pad:pad + L, :] = xbc_ref[...]
    acc = jnp.broadcast_to(cb_ref[...], (L, SSM_CONV_DIM))
    for kk in range(SSM_CONV):
        s = pad - (SSM_CONV - 1) + kk
        acc = acc + cw_ref[kk:kk + 1, :] * xcat_ref[s:s + L, :]
    xcat_ref[0:pad, :] = xcat_ref[L:L + pad, :]
    xc = acc * _sigmoid(acc)
    xs = xc[:, :SSM_INNER]
    bm = xc[:, SSM_INNER:SSM_INNER + SSM_GROUPS * SSM_STATE]
    cm = xc[:, SSM_INNER + SSM_GROUPS * SSM_STATE:]

    dtv = _softplus(dt_ref[...] + dtb_ref[...])
    adt = dtv * (-jnp.exp(alog_ref[...]))
    ri = lax.broadcasted_iota(jnp.int32, (L, L), 0)
    ci = lax.broadcasted_iota(jnp.int32, (L, L), 1)
    causal = ri >= ci
    tri = jnp.where(causal, 1.0, 0.0).astype(F32)
    acum = jnp.dot(tri, adt, precision=HIGHEST, preferred_element_type=F32)
    expand = e_ref[...]
    acx = jnp.dot(acum, expand, precision=HIGHEST, preferred_element_type=F32)
    dtx = jnp.dot(dtv, expand, precision=HIGHEST, preferred_element_type=F32)
    alx = acx[L - 1:L, :]
    xdt = xs * dtx
    acum_t = acum.T
    lo = lax.broadcasted_iota(jnp.int32, (L, LANES), 1) < HEAD_DIM
    nt = (((1,), (1,)), ((), ()))

    xdt_b = xdt.astype(BF16)
    y_parts = []
    for g in range(SSM_GROUPS):
        cg = cm[:, g * SSM_STATE:(g + 1) * SSM_STATE].astype(BF16)
        bg = bm[:, g * SSM_STATE:(g + 1) * SSM_STATE]
        y_off = jnp.dot(cg, state_ref[:, g * half:(g + 1) * half].astype(BF16),
                        preferred_element_type=F32)
        cb = lax.dot_general(cg, bg.astype(BF16), nt, preferred_element_type=F32)
        for p in range(half // LANES):
            pair = g * (half // LANES) + p
            xp = xdt_b[:, pair * LANES:(pair + 1) * LANES]
            yh = []
            for hh in range(2):
                h = 2 * pair + hh
                seg = acum[:, h:h + 1] - acum_t[h:h + 1, :]
                dec = jnp.where(causal, jnp.exp(jnp.minimum(seg, 0.0)), 0.0)
                yh.append(jnp.dot((cb * dec).astype(BF16), xp, preferred_element_type=F32))
            y_parts.append(jnp.where(lo, yh[0], yh[1])
                           + y_off[:, p * LANES:(p + 1) * LANES]
                           * jnp.exp(acx[:, pair * LANES:(pair + 1) * LANES]))
        xd = xdt[:, g * half:(g + 1) * half] * jnp.exp(alx[:, g * half:(g + 1) * half]
                                                       - acx[:, g * half:(g + 1) * half])
        upd = jnp.dot(bg.T.astype(BF16), xd.astype(BF16), preferred_element_type=F32)
        state_ref[:, g * half:(g + 1) * half] = (
            jnp.exp(alx[:, g * half:(g + 1) * half]) * state_ref[:, g * half:(g + 1) * half] + upd)

    y = jnp.concatenate(y_parts, axis=1) + dsk_ref[...] * xs
    zz = z_ref[...]
    y = y * (zz * _sigmoid(zz))
    outs = []
    for g in range(SSM_GROUPS):
        yg = y[:, g * half:(g + 1) * half]
        outs.append(yg * lax.rsqrt(jnp.mean(yg * yg, axis=-1, keepdims=True) + RMS_EPS))
    o_ref[...] = (jnp.concatenate(outs, axis=1) * g_ref[...]).astype(o_ref.dtype)


def _ssd(z, xbc, dt, conv_w, conv_b, dt_bias, a_log, d_skip, g_ssm, chunk=128):
    t = z.shape[0]
    chunk = min(chunk, t)
    padh = lambda a: jnp.pad(a.reshape(1, -1), ((0, 0), (0, LANES - a.shape[-1])))
    heads = jnp.arange(SSM_INNER, dtype=jnp.int32) // HEAD_DIM
    expand = (jnp.arange(LANES, dtype=jnp.int32)[:, None] == heads[None, :]).astype(F32)
    row = lambda w: pl.BlockSpec((chunk, w), lambda c: (c, 0))
    full = lambda a: pl.BlockSpec(a.shape, lambda c: (0, 0))
    params = [conv_w, conv_b.reshape(1, -1), padh(dt_bias), padh(a_log),
              jnp.repeat(d_skip, HEAD_DIM).reshape(1, -1), g_ssm.reshape(1, -1), expand]
    return pl.pallas_call(
        _ssd_kernel,
        grid=(t // chunk,),
        in_specs=[row(SSM_INNER), row(SSM_CONV_DIM), row(LANES)] + [full(a) for a in params],
        out_specs=row(SSM_INNER),
        out_shape=jax.ShapeDtypeStruct((t, SSM_INNER), BF16),
        scratch_shapes=[pltpu.VMEM((SSM_STATE, SSM_INNER), F32),
                        pltpu.VMEM((chunk + 8, SSM_CONV_DIM), F32)],
        compiler_params=_cparams("arbitrary"),
        name="ssd_mixer",
    )(z, xbc, dt, *params)


def _out_proj_kernel(x_ref, a_ref, s_ref, w_ref, o_ref):
    o_ref[...] = (x_ref[...]
                  + jnp.dot(a_ref[...], w_ref[0:SB_WIDTH, :], preferred_element_type=F32)
                  + jnp.dot(s_ref[...], w_ref[SB_WIDTH:, :], preferred_element_type=F32))


def _out_proj(x, attn, ssm, w_bf, tm=512):
    t, d = x.shape
    tm = min(tm, t)
    return pl.pallas_call(
        _out_proj_kernel,
        grid=(t // tm,),
        in_specs=[pl.BlockSpec((tm, d), lambda i: (i, 0)),
                  pl.BlockSpec((tm, SB_WIDTH), lambda i: (i, 0)),
                  pl.BlockSpec((tm, SSM_INNER), lambda i: (i, 0)),
                  pl.BlockSpec(w_bf.shape, lambda i: (0, 0), pipeline_mode=pl.Buffered(1))],
        out_specs=pl.BlockSpec((tm, d), lambda i: (i, 0)),
        out_shape=jax.ShapeDtypeStruct((t, d), F32),
        compiler_params=_cparams("arbitrary"),
        name="out_proj",
    )(x, attn, ssm, w_bf)


def _router_kernel(x_ref, g_ref, wt_ref, b_ref, idx_ref, gate_ref, pos_ref, cnt_ref, base_ref):
    i = pl.program_id(0)
    tm = x_ref.shape[0]

    @pl.when(i == 0)
    def _():
        base_ref[...] = jnp.zeros(base_ref.shape, F32)

    h = _rms(x_ref[...], g_ref[...])
    logits = lax.dot_general(wt_ref[...], h, (((1,), (1,)), ((), ())), precision=HIGHEST,
                             preferred_element_type=F32) + b_ref[:, 0:1]
    eio = lax.broadcasted_iota(jnp.int32, (N_EXPERTS, tm), 0).astype(F32)
    work = logits
    vals, hots = [], []
    for k in range(TOP_K):
        m = jnp.max(work, axis=0, keepdims=True)
        idx = jnp.min(jnp.where(work == m, eio, float(N_EXPERTS)), axis=0, keepdims=True)
        hot = eio == idx
        work = jnp.where(hot, -jnp.inf, work)
        vals.append(m)
        hots.append(hot)
        idx_ref[k:k + 1, :] = idx.astype(jnp.int32)
    ex = [jnp.exp(v - vals[0]) for v in vals]
    den = ex[0] + ex[1] + ex[2] + ex[3]
    picked = jnp.zeros((N_EXPERTS, tm), F32)
    for k in range(TOP_K):
        gate_ref[k:k + 1, :] = ex[k] / den
        picked = picked + jnp.where(hots[k], 1.0, 0.0)
    ti = lax.broadcasted_iota(jnp.int32, (tm, tm), 0)
    tj = lax.broadcasted_iota(jnp.int32, (tm, tm), 1)
    before = jnp.where(ti < tj, 1.0, 0.0).astype(BF16)
    rank = jnp.dot(picked.astype(BF16), before, preferred_element_type=F32) + base_ref[:, 0:1]
    for k in range(TOP_K):
        pos = jnp.sum(jnp.where(hots[k], rank, 0.0), axis=0, keepdims=True)
        pos_ref[k:k + 1, :] = pos.astype(jnp.int32)
    base_ref[...] = base_ref[...] + jnp.sum(picked, axis=1, keepdims=True)
    cnt_ref[...] = base_ref[...].astype(jnp.int32)


def _router(x1, g, w_router, b_router, tm=512):
    t, d = x1.shape
    tm = min(tm, t)
    wt = w_router.T
    b = jnp.broadcast_to(b_router.reshape(-1, 1), (N_EXPERTS, LANES))
    tok = lambda dt: (pl.BlockSpec((TOP_K, tm), lambda i: (0, i)), jax.ShapeDtypeStruct((TOP_K, t), dt))
    specs = [tok(jnp.int32), tok(F32), tok(jnp.int32),
             (pl.BlockSpec((N_EXPERTS, LANES), lambda i: (0, 0)),
              jax.ShapeDtypeStruct((N_EXPERTS, LANES), jnp.int32))]
    return pl.pallas_call(
        _router_kernel,
        grid=(t // tm,),
        in_specs=[pl.BlockSpec((tm, d), lambda i: (i, 0)),
                  pl.BlockSpec((1, d), lambda i: (0, 0)),
                  pl.BlockSpec((N_EXPERTS, d), lambda i: (0, 0)),
                  pl.BlockSpec((N_EXPERTS, LANES), lambda i: (0, 0))],
        out_specs=[s for s, _ in specs],
        out_shape=[s for _, s in specs],
        scratch_shapes=[pltpu.VMEM((N_EXPERTS, LANES), F32)],
        compiler_params=_cparams("arbitrary"),
        name="router",
    )(x1, g, wt, b)


def _dispatch_kernel(dest_ref, lo_ref, hi_ref, nu_ref, x_ref, g_ref, xs_hbm, hbuf, zbuf, sem, zsem,
                     *, n_blocks):
    i = pl.program_id(0)
    last = pl.num_programs(0) - 1
    tm = x_ref.shape[0]
    slot = i % 2

    def row_copies(step, sl, r):
        return [pltpu.make_async_copy(hbuf.at[sl, pl.ds(r, 1), :],
                                      xs_hbm.at[pl.ds(dest_ref[k, step * tm + r], 1), :], sem.at[sl])
                for k in range(TOP_K)]

    def wait_step(sl):
        for _ in range(TOP_K):
            pltpu.make_async_copy(hbuf.at[sl], xs_hbm.at[pl.ds(0, tm), :], sem.at[sl]).wait()

    @pl.when(i >= 2)
    def _():
        wait_step(slot)

    hbuf[slot] = _rms(x_ref[...], g_ref[...])

    def start(r, c):
        for k, cp in enumerate(row_copies(i, slot, r)):
            cp.start(priority=k % 2)
        return c
    lax.fori_loop(0, tm, start, 0, unroll=8)

    @pl.when(i == 0)
    def _():
        zbuf[...] = jnp.zeros(zbuf.shape, F32)

        def pad_row(r):
            return pltpu.make_async_copy(zbuf.at[pl.ds(0, 1), :], xs_hbm.at[pl.ds(r, 1), :], zsem.at[0])

        def tail_block(b):
            return pltpu.make_async_copy(zbuf, xs_hbm.at[pl.ds(pl.multiple_of(b * tm, tm), tm), :],
                                         zsem.at[0])

        def each(fn):
            def per_expert(e, c):
                return lax.fori_loop(lo_ref[e], hi_ref[e], lambda r, c2: (fn(pad_row(r)), c2)[1], c)
            lax.fori_loop(0, N_EXPERTS, per_expert, 0)
            lax.fori_loop(nu_ref[0], n_blocks, lambda b, c: (fn(tail_block(b)), c)[1], 0)

        each(lambda cp: cp.start())
        each(lambda cp: cp.wait())

    @pl.when(i == last)
    def _():
        @pl.when(i >= 1)
        def _():
            wait_step(1 - slot)
        wait_step(slot)


def _dispatch(x1, g, dest, pad_lo, pad_hi, n_used, n_blocks, tm):
    t, d = x1.shape
    return pl.pallas_call(
        functools.partial(_dispatch_kernel, n_blocks=n_blocks),
        grid_spec=pltpu.PrefetchScalarGridSpec(
            num_scalar_prefetch=4,
            grid=(t // tm,),
            in_specs=[pl.BlockSpec((tm, d), lambda i, *_: (i, 0)),
                      pl.BlockSpec((1, d), lambda i, *_: (0, 0))],
            out_specs=pl.BlockSpec(memory_space=pl.ANY),
            scratch_shapes=[pltpu.VMEM((2, tm, d), F32), pltpu.VMEM((tm, d), F32),
                            pltpu.SemaphoreType.DMA((2,)), pltpu.SemaphoreType.DMA((1,))]),
        out_shape=jax.ShapeDtypeStruct((n_blocks * tm, d), F32),
        compiler_params=_cparams("arbitrary"),
        name="moe_dispatch",
    )(dest, pad_lo, pad_hi, n_used, x1, g)


def _swiglu_pick(y, pick):
    tm, n2 = y.shape
    even = lax.broadcasted_iota(jnp.int32, (tm, 2 * LANES), 1) % 2 == 0
    outs = []
    for cblk in range(n2 // (2 * LANES)):
        yc = y[:, cblk * 2 * LANES:(cblk + 1) * 2 * LANES]
        capped = jnp.minimum(yc, SWIGLU_LIMIT)
        gate = capped * _sigmoid(SWIGLU_ALPHA * capped)
        up = jnp.maximum(capped, -SWIGLU_LIMIT) + 1.0
        factors = jnp.where(even, gate, up).astype(BF16)
        split = jnp.dot(factors, pick, preferred_element_type=F32)
        outs.append(split[:, :LANES] * split[:, LANES:])
    return jnp.concatenate(outs, axis=1)


def _gmm_kernel(ps_ref, nb_ref, nu_ref, b_ref, *rest, swiglu, tm, n_blocks):
    if swiglu:
        pick_ref, w_hbm, x_hbm, o_hbm, wbuf, wbf, xbuf, obuf, zbuf, wsem, sin, sout, zsem = rest
    else:
        w_hbm, x_hbm, o_hbm, wbuf, wbf, xbuf, obuf, zbuf, wsem, sin, sout, zsem = rest
    e = pl.program_id(0)
    j = pl.program_id(1)
    n_exp = pl.num_programs(0)
    nj = pl.num_programs(1)
    step = e * nj + j
    have_next = step < n_exp * nj - 1
    nb = nb_ref[e]
    g0 = nj * (ps_ref[e] // tm) + j * nb
    e_next = jnp.minimum(jnp.where(j + 1 < nj, e, e + 1), n_exp - 1)
    e_prev = jnp.maximum(jnp.where(j > 0, e, e - 1), 0)
    next_has = jnp.logical_and(have_next, nb_ref[e_next] > 0)
    first_here = jnp.logical_or(step == 0, nb_ref[e_prev] == 0)
    kdim, tn = wbf.shape
    tn_out = obuf.shape[2]
    col = pl.multiple_of(j * tn_out, tn_out)
    nch = kdim // W_CHUNK_ROWS

    def w_chunk(st, c):
        r = pl.ds(pl.multiple_of(c * W_CHUNK_ROWS, W_CHUNK_ROWS), W_CHUNK_ROWS)
        cols = pl.ds(pl.multiple_of((st % nj) * tn, tn), tn)
        return pltpu.make_async_copy(w_hbm.at[st // nj, r, cols], wbuf.at[r, :], wsem.at[0])

    def start_chunks(st, lo, hi):
        lax.fori_loop(lo, hi, lambda c, x: (w_chunk(st, c).start(), x)[1], 0)

    def x_copy(row0, sl):
        return pltpu.make_async_copy(x_hbm.at[pl.ds(pl.multiple_of(row0, tm), tm), :], xbuf.at[sl],
                                     sin.at[sl])

    def o_copy(b, sl):
        r = pl.ds(pl.multiple_of(ps_ref[e] + b * tm, tm), tm)
        return pltpu.make_async_copy(obuf.at[sl], o_hbm.at[r, pl.ds(col, tn_out)], sout.at[sl])

    @pl.when(step == 0)
    def _():
        start_chunks(step, 0, nch)

    @pl.when(jnp.logical_and(nb > 0, first_here))
    def _():
        x_copy(ps_ref[e], g0 % 2).start()

    lax.fori_loop(0, nch, lambda c, x: (w_chunk(step, c).wait(), x)[1], 0)

    @pl.when(nb > 0)
    def _():
        wbf[...] = wbuf[...].astype(BF16)

    per_block = (nch + jnp.maximum(nb, 1) - 1) // jnp.maximum(nb, 1)

    def block(b, c):
        sl = (g0 + b) % 2
        more = b + 1 < nb

        @pl.when(jnp.logical_or(more, next_has))
        def _():
            x_copy(jnp.where(more, ps_ref[e] + (b + 1) * tm, ps_ref[e_next]), 1 - sl).start()

        @pl.when(have_next)
        def _():
            start_chunks(step + 1, jnp.minimum(b * per_block, nch),
                         jnp.minimum((b + 1) * per_block, nch))

        x_copy(ps_ref[e] + b * tm, sl).wait()

        @pl.when(g0 + b >= 2)
        def _():
            o_copy(b, sl).wait()

        y = jnp.dot(xbuf[sl].astype(BF16), wbf[...], preferred_element_type=F32) + b_ref[...]
        if swiglu:
            y = _swiglu_pick(y, pick_ref[...])
        obuf[sl] = y.astype(obuf.dtype)
        o_copy(b, sl).start()
        return c

    lax.fori_loop(0, nb, block, 0)

    @pl.when(jnp.logical_and(have_next, nb == 0))
    def _():
        start_chunks(step + 1, 0, nch)

    @pl.when(e == n_exp - 1)
    def _():
        zbuf[...] = jnp.zeros(zbuf.shape, zbuf.dtype)

        def tail(b):
            return pltpu.make_async_copy(
                zbuf, o_hbm.at[pl.ds(pl.multiple_of(b * tm, tm), tm), pl.ds(col, tn_out)], zsem.at[0])

        lax.fori_loop(nu_ref[0], n_blocks, lambda b, c: (tail(b).start(), c)[1], 0)
        lax.fori_loop(nu_ref[0], n_blocks, lambda b, c: (tail(b).wait(), c)[1], 0)

    @pl.when(jnp.logical_not(have_next))
    def _():
        done = g0 + nb

        @pl.when(done >= 2)
        def _():
            o_copy(0, done % 2).wait()

        @pl.when(done >= 1)
        def _():
            o_copy(0, (done + 1) % 2).wait()


def _grouped_matmul(x, w, b, pstart, nblk, n_used, *, tm, tn, swiglu, out_dtype):
    n_slots, kdim = x.shape
    n_exp, _, n = w.shape
    n_out = n // 2 if swiglu else n
    tn_out = tn // 2 if swiglu else tn
    in_specs = [pl.BlockSpec((None, 1, tn), lambda e, j, *_: (e, 0, j))]
    args = [b.reshape(n_exp, 1, n)]
    if swiglu:
        src = jnp.arange(2 * LANES, dtype=jnp.int32)[:, None]
        dst = jnp.arange(2 * LANES, dtype=jnp.int32)[None, :]
        args.append((src == 2 * (dst % LANES) + dst // LANES).astype(BF16))
        in_specs.append(pl.BlockSpec((2 * LANES, 2 * LANES), lambda e, j, *_: (0, 0)))
    in_specs += [pl.BlockSpec(memory_space=pl.ANY)] * 2
    args += [w, x]
    return pl.pallas_call(
        functools.partial(_gmm_kernel, swiglu=swiglu, tm=tm, n_blocks=n_slots // tm),
        grid_spec=pltpu.PrefetchScalarGridSpec(
            num_scalar_prefetch=3,
            grid=(n_exp, n // tn),
            in_specs=in_specs,
            out_specs=pl.BlockSpec(memory_space=pl.ANY),
            scratch_shapes=[pltpu.VMEM((kdim, tn), F32), pltpu.VMEM((kdim, tn), BF16),
                            pltpu.VMEM((2, tm, kdim), x.dtype), pltpu.VMEM((2, tm, tn_out), out_dtype),
                            pltpu.VMEM((tm, tn_out), out_dtype),
                            pltpu.SemaphoreType.DMA((1,)), pltpu.SemaphoreType.DMA((2,)),
                            pltpu.SemaphoreType.DMA((2,)), pltpu.SemaphoreType.DMA((1,))]),
        out_shape=jax.ShapeDtypeStruct((n_slots, n_out), out_dtype),
        compiler_params=_cparams("arbitrary", "arbitrary"),
        name="moe_gate_up" if swiglu else "moe_down",
    )(pstart, nblk, n_used, *args)


def _row_copy(src_hbm, row, dst, r, sem):
    return pltpu.make_async_copy(src_hbm.at[pl.ds(row, 1), :], dst.at[pl.ds(r, 1), :], sem)


def _combine_kernel(dest_ref, x_ref, gate_ref, g_ref, ys_hbm, o_ref, buf_ref, sem, *, norm):
    i = pl.program_id(0)
    tc = x_ref.shape[0]
    slot = i % 2

    def gather(step, sl):
        def start(r, c):
            for k in range(TOP_K):
                _row_copy(ys_hbm, dest_ref[k, step * tc + r], buf_ref.at[sl, k], r,
                          sem.at[sl]).start(priority=k % 2)
            return c
        lax.fori_loop(0, tc, start, 0, unroll=8)

    @pl.when(i == 0)
    def _():
        gather(i, slot)

    @pl.when(i + 1 < pl.num_programs(0))
    def _():
        gather(i + 1, 1 - slot)

    for k in range(TOP_K):
        pltpu.make_async_copy(ys_hbm.at[pl.ds(0, tc), :], buf_ref.at[slot, k], sem.at[slot]).wait()
    y = x_ref[...]
    for k in range(TOP_K):
        y = y + gate_ref[:, k:k + 1] * buf_ref[slot, k]
    o_ref[...] = _rms(y, g_ref[...]) if norm else y


def _combine(x1, gates_t, dest, ys, g_final, norm, tc=256):
    t, d = x1.shape
    tc = min(tc, t)
    return pl.pallas_call(
        functools.partial(_combine_kernel, norm=norm),
        grid_spec=pltpu.PrefetchScalarGridSpec(
            num_scalar_prefetch=1,
            grid=(t // tc,),
            in_specs=[pl.BlockSpec((tc, d), lambda i, dest: (i, 0)),
                      pl.BlockSpec((tc, TOP_K), lambda i, dest: (i, 0)),
                      pl.BlockSpec((1, d), lambda i, dest: (0, 0)),
                      pl.BlockSpec(memory_space=pl.ANY)],
            out_specs=pl.BlockSpec((tc, d), lambda i, dest: (i, 0)),
            scratch_shapes=[pltpu.VMEM((2, TOP_K, tc, d), F32), pltpu.SemaphoreType.DMA((2,))]),
        out_shape=jax.ShapeDtypeStruct((t, d), F32),
        compiler_params=_cparams("arbitrary"),
        name="moe_combine",
    )(dest, x1, gates_t, g_final, ys)


def _mixer(x, layer, g_mix, w_in, conv_w, conv_b, dt_bias, a_log, d_skip, g_ssm, w_out):
    w_bf_t, w_dt_t = _to_bf16_t(jnp.swapaxes(w_in, 1, 2), layer, ntail=SSM_HEADS)
    q, k, v, z, xbc, dt = _in_proj(x, g_mix.reshape(1, -1), w_bf_t, w_dt_t)
    attn = _attention(q, k, v)
    ssm = _ssd(z, xbc, dt, conv_w, conv_b, dt_bias, a_log, d_skip, g_ssm)
    return _out_proj(x, attn, ssm, _to_bf16(w_out, layer))


def _moe(x1, g_ffn, w_router, b_router, w_gate_up, b_gate_up, w_down, b_down, g_out, norm, tm=256):
    t, d = x1.shape
    g_ffn = g_ffn.reshape(1, -1)
    idx, gates, pos, cnt = _router(x1, g_ffn, w_router, b_router)
    counts = cnt[:, 0]
    padded = (counts + tm - 1) // tm * tm
    pend = jnp.cumsum(padded)
    pstart = pend - padded
    experts = jnp.arange(N_EXPERTS, dtype=jnp.int32)
    dest = pos + jnp.sum(jnp.where(idx[None] == experts[:, None, None],
                                   pstart[:, None, None], 0), axis=0)
    n_blocks = (t * TOP_K) // tm + N_EXPERTS
    n_used = (pend[-1] // tm).reshape(1)
    nblk = padded // tm

    xs = _dispatch(x1, g_ffn, dest, pstart + counts, pend, n_used, n_blocks, tm)
    act = _grouped_matmul(xs, w_gate_up, b_gate_up, pstart, nblk, n_used,
                          tm=tm, tn=2048, swiglu=True, out_dtype=BF16)
    ys = _grouped_matmul(act, w_down, b_down, pstart, nblk, n_used,
                         tm=tm, tn=2048, swiglu=False, out_dtype=F32)
    return _combine(x1, gates.T, dest, ys, g_out.reshape(1, -1), norm)


def kernel(x, g_mix, w_in, conv_w, conv_b, dt_bias, a_log, d_skip, g_ssm, w_out, g_ffn, w_router,
           b_router, w_gate_up, b_gate_up, w_down, b_down, g_final):
    b, s, d = x.shape
    depth = g_mix.shape[0]
    outs = []
    for bi in range(b):
        xb = x[bi]
        for l in range(depth):
            x1 = _mixer(xb, l, g_mix[l], w_in, conv_w[l], conv_b[l], dt_bias[l], a_log[l],
                        d_skip[l], g_ssm[l], w_out)
            xb = _moe(x1, g_ffn[l], w_router[l], b_router[l], w_gate_up[l], b_gate_up[l],
                      w_down[l], b_down[l], g_final, norm=(l == depth - 1))
        outs.append(xb)
    return outs[0][None] if b == 1 else jnp.stack(outs)
```

```python
import functools

import jax
import jax.numpy as jnp
from jax import lax
from jax.experimental import pallas as pl
from jax.experimental.pallas import tpu as pltpu

F32 = jnp.float32
BF16 = jnp.bfloat16
HIGHEST = lax.Precision.HIGHEST

LANES = 128
HEAD_DIM = 64
SB_WIDTH = 1024
SSM_INNER = 1024
SSM_GROUPS = 2
SSM_STATE = 128
SSM_CONV = 4
SSM_CONV_DIM = SSM_INNER + 2 * SSM_GROUPS * SSM_STATE
SSM_HEADS = SSM_INNER // HEAD_DIM
N_EXPERTS = 32
TOP_K = 4
SWIGLU_LIMIT = 7.0
SWIGLU_ALPHA = 1.702
RMS_EPS = 1e-5

VMEM_LIMIT = 56 * 1024 * 1024

ATTN_DEAD = -110.0

W_CHUNK_ROWS = 256


def _cparams(*sem):
    return pltpu.CompilerParams(dimension_semantics=sem, vmem_limit_bytes=VMEM_LIMIT)


def _rms(x, g):
    return x * lax.rsqrt(jnp.mean(x * x, axis=-1, keepdims=True) + RMS_EPS) * g


def _sigmoid(x):
    return 1.0 / (1.0 + jnp.exp(-x))


def _softplus(x):
    return jnp.maximum(x, 0.0) + jnp.log(1.0 + jnp.exp(-jnp.abs(x)))


def _cast_kernel(w_ref, o_ref):
    o_ref[...] = w_ref[...].astype(o_ref.dtype)


def _to_bf16(w, layer, rows=256):
    _, k, n = w.shape
    return pl.pallas_call(
        _cast_kernel,
        grid=(k // rows,),
        in_specs=[pl.BlockSpec((None, rows, n), lambda i: (layer, i, 0))],
        out_specs=pl.BlockSpec((rows, n), lambda i: (i, 0)),
        out_shape=jax.ShapeDtypeStruct((k, n), BF16),
        compiler_params=_cparams("arbitrary"),
        name="cast_bf16",
    )(w)


def _cast_t_kernel(w_ref, o_ref, tail_ref, *, ntail):
    w = w_ref[...]
    o_ref[...] = w.astype(o_ref.dtype)
    tail_ref[...] = jnp.concatenate(
        [w[w.shape[0] - ntail:, :], jnp.zeros((LANES - ntail, w.shape[1]), F32)], axis=0)


def _to_bf16_t(w_t, layer, ntail, cols=256):
    _, n, k = w_t.shape
    return pl.pallas_call(
        functools.partial(_cast_t_kernel, ntail=ntail),
        grid=(k // cols,),
        in_specs=[pl.BlockSpec((None, n, cols), lambda i: (layer, 0, i))],
        out_specs=[pl.BlockSpec((n, cols), lambda i: (0, i)),
                   pl.BlockSpec((LANES, cols), lambda i: (0, i))],
        out_shape=[jax.ShapeDtypeStruct((n, k), BF16), jax.ShapeDtypeStruct((LANES, k), F32)],
        compiler_params=_cparams("arbitrary"),
        name="cast_bf16_t",
    )(w_t)


def _in_proj_kernel(x_ref, g_ref, w_ref, wdt_ref, q_ref, k_ref, v_ref, z_ref, xbc_ref, dt_ref):
    hb = _rms(x_ref[...], g_ref[...]).astype(BF16)
    nt = (((1,), (1,)), ((), ()))

    def mm(a, b):
        return lax.dot_general(hb, w_ref[a:b, :], nt, preferred_element_type=F32)

    o = 0
    q_ref[...] = (mm(o, o + SB_WIDTH) * (HEAD_DIM ** -0.5)).astype(BF16)
    o += SB_WIDTH
    k_ref[...] = mm(o, o + SB_WIDTH).astype(BF16)
    o += SB_WIDTH
    v_ref[...] = mm(o, o + SB_WIDTH).astype(BF16)
    o += SB_WIDTH
    z_ref[...] = mm(o, o + SSM_INNER)
    o += SSM_INNER
    xbc_ref[...] = mm(o, o + SSM_CONV_DIM)
    dt_ref[...] = lax.dot_general(hb, wdt_ref[...].astype(BF16), nt, preferred_element_type=F32)


def _in_proj(x, g, w_bf_t, w_dt_t, tm=256):
    t, d = x.shape
    n = w_bf_t.shape[0]
    row = lambda w: pl.BlockSpec((tm, w), lambda i: (i, 0))
    outs = [(SB_WIDTH, BF16)] * 3 + [(SSM_INNER, F32), (SSM_CONV_DIM, F32), (LANES, F32)]
    return pl.pallas_call(
        _in_proj_kernel,
        grid=(t // tm,),
        in_specs=[row(d),
                  pl.BlockSpec((1, d), lambda i: (0, 0)),
                  pl.BlockSpec((n, d), lambda i: (0, 0), pipeline_mode=pl.Buffered(1)),
                  pl.BlockSpec((LANES, d), lambda i: (0, 0))],
        out_specs=[row(w) for w, _ in outs],
        out_shape=[jax.ShapeDtypeStruct((t, w), dt) for w, dt in outs],
        compiler_params=_cparams("arbitrary"),
        name="in_proj",
    )(x, g, w_bf_t, w_dt_t)


def _attn_kernel(q_ref, k_ref, v_ref, o_ref, acc_ref, carry_ref, *, tq):
    i = pl.program_id(1)
    nsub = tq // LANES
    qb0 = i * nsub
    row = lax.broadcasted_iota(jnp.int32, (LANES, LANES), 0)
    col = lax.broadcasted_iota(jnp.int32, (LANES, LANES), 1)
    past = col < row
    lo = col < HEAD_DIM
    jj = lax.broadcasted_iota(jnp.int32, (LANES, 2 * LANES), 0)
    ss = lax.broadcasted_iota(jnp.int32, (LANES, 2 * LANES), 1)
    neg_later = jnp.where((ss >= LANES) | (jj > ss), -1.0, 0.0).astype(BF16)
    nt = (((1,), (1,)), ((), ()))

    def step(n, diag):
        k0s = [pl.multiple_of(jnp.maximum(qb0 + s - n, 0) * LANES, LANES) for s in range(nsub)]
        chains = [(s, hd) for s in range(nsub) for hd in range(2)]
        logits = []
        for s, hd in chains:
            q = q_ref[s * LANES:(s + 1) * LANES, :]
            qh = jnp.where(lo if hd == 0 else jnp.logical_not(lo), q, jnp.zeros(q.shape, q.dtype))
            kt = k_ref[pl.ds(k0s[s], LANES), :]
            logits.append(lax.dot_general(qh, kt, nt, preferred_element_type=F32))
        logsig, sums = [], []
        for l in logits:
            sp = _softplus(l)
            lk = jnp.where(past, sp, 0.0) if diag else sp
            sums.append(jnp.dot(lk.astype(BF16), neg_later, preferred_element_type=F32))
            logsig.append(l - sp)
        top = None
        for c, (s, hd) in enumerate(chains):
            r = sums[c]
            if diag:
                w = jnp.where(past, jnp.exp(logsig[c] + r[:, :LANES]), 0.0)
                carry = r[:, LANES:]
            else:
                before = jnp.where(qb0 + s - n >= 0, carry_ref[c], -1e30)
                w = jnp.exp(logsig[c] + r[:, :LANES] + before)
                carry = before + r[:, LANES:]
            vt = v_ref[pl.ds(k0s[s], LANES), :]
            pv = jnp.dot(w.astype(BF16), vt, preferred_element_type=F32)
            if diag:
                acc_ref[c] = pv
            else:
                acc_ref[c] += pv
            carry_ref[c] = carry
            top = carry if top is None else jnp.maximum(top, carry)
        return jnp.max(top)

    top0 = step(0, True)

    def cond(st):
        n, top = st
        return jnp.logical_and(n <= qb0 + nsub - 1, top > ATTN_DEAD)

    def body(st):
        n, _ = st
        return n + 1, step(n, False)

    lax.while_loop(cond, body, (jnp.int32(1), top0))
    for s in range(nsub):
        o_ref[s * LANES:(s + 1) * LANES, :] = jnp.where(
            lo, acc_ref[2 * s], acc_ref[2 * s + 1]).astype(o_ref.dtype)


def _attention(q, k, v, tq=1024):
    t, w = q.shape
    tq = min(tq, t)
    chains = 2 * (tq // LANES)
    return pl.pallas_call(
        functools.partial(_attn_kernel, tq=tq),
        grid=(w // LANES, t // tq),
        in_specs=[pl.BlockSpec((tq, LANES), lambda p, i: (i, p)),
                  pl.BlockSpec((t, LANES), lambda p, i: (0, p)),
                  pl.BlockSpec((t, LANES), lambda p, i: (0, p))],
        out_specs=pl.BlockSpec((tq, LANES), lambda p, i: (i, p)),
        out_shape=jax.ShapeDtypeStruct((t, w), BF16),
        scratch_shapes=[pltpu.VMEM((chains, LANES, LANES), F32),
                        pltpu.VMEM((chains, LANES, LANES), F32)],
        compiler_params=_cparams("arbitrary", "arbitrary"),
        name="sb_attention",
    )(q, k, v)


def _ssd_kernel(z_ref, xbc_ref, dt_ref, cw_ref, cb_ref, dtb_ref, alog_ref, dsk_ref, g_ref, e_ref,
                o_ref, state_ref, xcat_ref):
    c = pl.program_id(0)
    L = z_ref.shape[0]
    half = SSM_INNER // SSM_GROUPS
    pad = 8

    @pl.when(c == 0)
    def _():
        state_ref[...] = jnp.zeros(state_ref.shape, F32)
        xcat_ref[0:pad, :] = jnp.zeros((pad, SSM_CONV_DIM), F32)

    xcat_ref[pad:pad + L, :] = xbc_ref[...]
    acc = jnp.broadcast_to(cb_ref[...], (L, SSM_CONV_DIM))
    for kk in range(SSM_CONV):
        s = pad - (SSM_CONV - 1) + kk
        acc = acc + cw_ref[kk:kk + 1, :] * xcat_ref[s:s + L, :]
    xcat_ref[0:pad, :] = xcat_ref[L:L + pad, :]
    xc = acc * _sigmoid(acc)
    xs = xc[:, :SSM_INNER]
    bm = xc[:, SSM_INNER:SSM_INNER + SSM_GROUPS * SSM_STATE]
    cm = xc[:, SSM_INNER + SSM_GROUPS * SSM_STATE:]

    dtv = _softplus(dt_ref[...] + dtb_ref[...])
    adt = dtv * (-jnp.exp(alog_ref[...]))
    ri = lax.broadcasted_iota(jnp.int32, (L, L), 0)
    ci = lax.broadcasted_iota(jnp.int32, (L, L), 1)
    causal = ri >= ci
    tri = jnp.where(causal, 1.0, 0.0).astype(F32)
    acum = jnp.dot(tri, adt, precision=HIGHEST, preferred_element_type=F32)
    expand = e_ref[...]
    acx = jnp.dot(acum, expand, precision=HIGHEST, preferred_element_type=F32)
    dtx = jnp.dot(dtv, expand, precision=HIGHEST, preferred_element_type=F32)
    alx = acx[L - 1:L, :]
    xdt = xs * dtx
    acum_t = acum.T
    lo = lax.broadcasted_iota(jnp.int32, (L, LANES), 1) < HEAD_DIM
    nt = (((1,), (1,)), ((), ()))

    xdt_b = xdt.astype(BF16)
    y_parts = []
    for g in range(SSM_GROUPS):
        cg = cm[:, g * SSM_STATE:(g + 1) * SSM_STATE].astype(BF16)
        bg = bm[:, g * SSM_STATE:(g + 1) * SSM_STATE]
        y_off = jnp.dot(cg, state_ref[:, g * half:(g + 1) * half].astype(BF16),
                        preferred_element_type=F32)
        cb = lax.dot_general(cg, bg.astype(BF16), nt, preferred_element_type=F32)
        for p in range(half // LANES):
            pair = g * (half // LANES) + p
            xp = xdt_b[:, pair * LANES:(pair + 1) * LANES]
            yh = []
            for hh in range(2):
                h = 2 * pair + hh
                seg = acum[:, h:h + 1] - acum_t[h:h + 1, :]
                dec = jnp.where(causal, jnp.exp(jnp.minimum(seg, 0.0)), 0.0)
                yh.append(jnp.dot((cb * dec).astype(BF16), xp, preferred_element_type=F32))
            y_parts.append(jnp.where(lo, yh[0], yh[1])
                           + y_off[:, p * LANES:(p + 1) * LANES]
                           * jnp.exp(acx[:, pair * LANES:(pair + 1) * LANES]))
        xd = xdt[:, g * half:(g + 1) * half] * jnp.exp(alx[:, g * half:(g + 1) * half]
                                                       - acx[:, g * half:(g + 1) * half])
        upd = jnp.dot(bg.T.astype(BF16), xd.astype(BF16), preferred_element_type=F32)
        state_ref[:, g * half:(g + 1) * half] = (
            jnp.exp(alx[:, g * half:(g + 1) * half]) * state_ref[:, g * half:(g + 1) * half] + upd)

    y = jnp.concatenate(y_parts, axis=1) + dsk_ref[...] * xs
    zz = z_ref[...]
    y = y * (zz * _sigmoid(zz))
    outs = []
    for g in range(SSM_GROUPS):
        yg = y[:, g * half:(g + 1) * half]
        outs.append(yg * lax.rsqrt(jnp.mean(yg * yg, axis=-1, keepdims=True) + RMS_EPS))
    o_ref[...] = (jnp.concatenate(outs, axis=1) * g_ref[...]).astype(o_ref.dtype)


def _ssd(z, xbc, dt, conv_w, conv_b, dt_bias, a_log, d_skip, g_ssm, chunk=128):
    t = z.shape[0]
    chunk = min(chunk, t)
    padh = lambda a: jnp.pad(a.reshape(1, -1), ((0, 0), (0, LANES - a.shape[-1])))
    heads = jnp.arange(SSM_INNER, dtype=jnp.int32) // HEAD_DIM
    expand = (jnp.arange(LANES, dtype=jnp.int32)[:, None] == heads[None, :]).astype(F32)
    row = lambda w: pl.BlockSpec((chunk, w), lambda c: (c, 0))
    full = lambda a: pl.BlockSpec(a.shape, lambda c: (0, 0))
    params = [conv_w, conv_b.reshape(1, -1), padh(dt_bias), padh(a_log),
              jnp.repeat(d_skip, HEAD_DIM).reshape(1, -1), g_ssm.reshape(1, -1), expand]
    return pl.pallas_call(
        _ssd_kernel,
        grid=(t // chunk,),
        in_specs=[row(SSM_INNER), row(SSM_CONV_DIM), row(LANES)] + [full(a) for a in params],
        out_specs=row(SSM_INNER),
        out_shape=jax.ShapeDtypeStruct((t, SSM_INNER), BF16),
        scratch_shapes=[pltpu.VMEM((SSM_STATE, SSM_INNER), F32),
                        pltpu.VMEM((chunk + 8, SSM_CONV_DIM), F32)],
        compiler_params=_cparams("arbitrary"),
        name="ssd_mixer",
    )(z, xbc, dt, *params)


def _out_proj_kernel(x_ref, a_ref, s_ref, w_ref, o_ref):
    o_ref[...] = (x_ref[...]
                  + jnp.dot(a_ref[...], w_ref[0:SB_WIDTH, :], preferred_element_type=F32)
                  + jnp.dot(s_ref[...], w_ref[SB_WIDTH:, :], preferred_element_type=F32))


def _out_proj(x, attn, ssm, w_bf, tm=512):
    t, d = x.shape
    tm = min(tm, t)
    return pl.pallas_call(
        _out_proj_kernel,
        grid=(t // tm,),
        in_specs=[pl.BlockSpec((tm, d), lambda i: (i, 0)),
                  pl.BlockSpec((tm, SB_WIDTH), lambda i: (i, 0)),
                  pl.BlockSpec((tm, SSM_INNER), lambda i: (i, 0)),
                  pl.BlockSpec(w_bf.shape, lambda i: (0, 0), pipeline_mode=pl.Buffered(1))],
        out_specs=pl.BlockSpec((tm, d), lambda i: (i, 0)),
        out_shape=jax.ShapeDtypeStruct((t, d), F32),
        compiler_params=_cparams("arbitrary"),
        name="out_proj",
    )(x, attn, ssm, w_bf)


def _router_kernel(x_ref, g_ref, wt_ref, b_ref, idx_ref, gate_ref, pos_ref, cnt_ref, base_ref):
    i = pl.program_id(0)
    tm = x_ref.shape[0]

    @pl.when(i == 0)
    def _():
        base_ref[...] = jnp.zeros(base_ref.shape, F32)

    h = _rms(x_ref[...], g_ref[...])
    nt = (((1,), (1,)), ((), ()))
    w = wt_ref[...]
    w_hi = w.astype(BF16)
    w_lo = (w - w_hi.astype(F32)).astype(BF16)
    h_hi = h.astype(BF16)
    h_lo = (h - h_hi.astype(F32)).astype(BF16)
    logits = (lax.dot_general(w_hi, h_hi, nt, preferred_element_type=F32)
              + lax.dot_general(w_hi, h_lo, nt, preferred_element_type=F32)
              + lax.dot_general(w_lo, h_hi, nt, preferred_element_type=F32)) + b_ref[:, 0:1]
    eio = lax.broadcasted_iota(jnp.int32, (N_EXPERTS, tm), 0).astype(F32)
    work = logits
    vals, hots = [], []
    for k in range(TOP_K):
        m = jnp.max(work, axis=0, keepdims=True)
        idx = jnp.min(jnp.where(work == m, eio, float(N_EXPERTS)), axis=0, keepdims=True)
        hot = eio == idx
        work = jnp.where(hot, -jnp.inf, work)
        vals.append(m)
        hots.append(hot)
        idx_ref[k:k + 1, :] = idx.astype(jnp.int32)
    ex = [jnp.exp(v - vals[0]) for v in vals]
    den = ex[0] + ex[1] + ex[2] + ex[3]
    picked = jnp.zeros((N_EXPERTS, tm), F32)
    for k in range(TOP_K):
        gate_ref[k:k + 1, :] = ex[k] / den
        picked = picked + jnp.where(hots[k], 1.0, 0.0)
    ti = lax.broadcasted_iota(jnp.int32, (tm, tm), 0)
    tj = lax.broadcasted_iota(jnp.int32, (tm, tm), 1)
    before = jnp.where(ti < tj, 1.0, 0.0).astype(BF16)
    rank = jnp.dot(picked.astype(BF16), before, preferred_element_type=F32) + base_ref[:, 0:1]
    for k in range(TOP_K):
        pos = jnp.sum(jnp.where(hots[k], rank, 0.0), axis=0, keepdims=True)
        pos_ref[k:k + 1, :] = pos.astype(jnp.int32)
    base_ref[...] = base_ref[...] + jnp.sum(picked, axis=1, keepdims=True)
    cnt_ref[...] = base_ref[...].astype(jnp.int32)


def _router(x1, g, w_router, b_router, tm=512):
    t, d = x1.shape
    tm = min(tm, t)
    wt = w_router.T
    b = jnp.broadcast_to(b_router.reshape(-1, 1), (N_EXPERTS, LANES))
    tok = lambda dt: (pl.BlockSpec((TOP_K, tm), lambda i: (0, i)), jax.ShapeDtypeStruct((TOP_K, t), dt))
    specs = [tok(jnp.int32), tok(F32), tok(jnp.int32),
             (pl.BlockSpec((N_EXPERTS, LANES), lambda i: (0, 0)),
              jax.ShapeDtypeStruct((N_EXPERTS, LANES), jnp.int32))]
    return pl.pallas_call(
        _router_kernel,
        grid=(t // tm,),
        in_specs=[pl.BlockSpec((tm, d), lambda i: (i, 0)),
                  pl.BlockSpec((1, d), lambda i: (0, 0)),
                  pl.BlockSpec((N_EXPERTS, d), lambda i: (0, 0)),
                  pl.BlockSpec((N_EXPERTS, LANES), lambda i: (0, 0))],
        out_specs=[s for s, _ in specs],
        out_shape=[s for _, s in specs],
        scratch_shapes=[pltpu.VMEM((N_EXPERTS, LANES), F32)],
        compiler_params=_cparams("arbitrary"),
        name="router",
    )(x1, g, wt, b)


def _dispatch_kernel(dest_ref, lo_ref, hi_ref, nu_ref, x_ref, g_ref, xs_hbm, hbuf, zbuf, sem, zsem,
                     *, n_blocks):
    i = pl.program_id(0)
    last = pl.num_programs(0) - 1
    tm = x_ref.shape[0]
    slot = i % 2

    def row_copies(step, sl, r):
        return [pltpu.make_async_copy(hbuf.at[sl, pl.ds(r, 1), :],
                                      xs_hbm.at[pl.ds(dest_ref[k, step * tm + r], 1), :], sem.at[sl])
                for k in range(TOP_K)]

    def wait_step(sl):
        for _ in range(TOP_K):
            pltpu.make_async_copy(hbuf.at[sl], xs_hbm.at[pl.ds(0, tm), :], sem.at[sl]).wait()

    @pl.when(i >= 2)
    def _():
        wait_step(slot)

    hbuf[slot] = _rms(x_ref[...], g_ref[...])

    def start(r, c):
        for k, cp in enumerate(row_copies(i, slot, r)):
            cp.start(priority=k % 2)
        return c
    lax.fori_loop(0, tm, start, 0, unroll=8)

    @pl.when(i == 0)
    def _():
        zbuf[...] = jnp.zeros(zbuf.shape, F32)

        def pad_row(r):
            return pltpu.make_async_copy(zbuf.at[pl.ds(0, 1), :], xs_hbm.at[pl.ds(r, 1), :], zsem.at[0])

        def tail_block(b):
            return pltpu.make_async_copy(zbuf, xs_hbm.at[pl.ds(pl.multiple_of(b * tm, tm), tm), :],
                                         zsem.at[0])

        def each(fn):
            def per_expert(e, c):
                return lax.fori_loop(lo_ref[e], hi_ref[e], lambda r, c2: (fn(pad_row(r)), c2)[1], c)
            lax.fori_loop(0, N_EXPERTS, per_expert, 0)
            lax.fori_loop(nu_ref[0], n_blocks, lambda b, c: (fn(tail_block(b)), c)[1], 0)

        each(lambda cp: cp.start())
        each(lambda cp: cp.wait())

    @pl.when(i == last)
    def _():
        @pl.when(i >= 1)
        def _():
            wait_step(1 - slot)
        wait_step(slot)


def _dispatch(x1, g, dest, pad_lo, pad_hi, n_used, n_blocks, tm):
    t, d = x1.shape
    return pl.pallas_call(
        functools.partial(_dispatch_kernel, n_blocks=n_blocks),
        grid_spec=pltpu.PrefetchScalarGridSpec(
            num_scalar_prefetch=4,
            grid=(t // tm,),
            in_specs=[pl.BlockSpec((tm, d), lambda i, *_: (i, 0)),
                      pl.BlockSpec((1, d), lambda i, *_: (0, 0))],
            out_specs=pl.BlockSpec(memory_space=pl.ANY),
            scratch_shapes=[pltpu.VMEM((2, tm, d), F32), pltpu.VMEM((tm, d), F32),
                            pltpu.SemaphoreType.DMA((2,)), pltpu.SemaphoreType.DMA((1,))]),
        out_shape=jax.ShapeDtypeStruct((n_blocks * tm, d), F32),
        compiler_params=_cparams("arbitrary"),
        name="moe_dispatch",
    )(dest, pad_lo, pad_hi, n_used, x1, g)


def _swiglu_pick(y, pick):
    tm, n2 = y.shape
    even = lax.broadcasted_iota(jnp.int32, (tm, 2 * LANES), 1) % 2 == 0
    outs = []
    for cblk in range(n2 // (2 * LANES)):
        yc = y[:, cblk * 2 * LANES:(cblk + 1) * 2 * LANES]
        capped = jnp.minimum(yc, SWIGLU_LIMIT)
        gate = capped * _sigmoid(SWIGLU_ALPHA * capped)
        up = jnp.maximum(capped, -SWIGLU_LIMIT) + 1.0
        factors = jnp.where(even, gate, up).astype(BF16)
        split = jnp.dot(factors, pick, preferred_element_type=F32)
        outs.append(split[:, :LANES] * split[:, LANES:])
    return jnp.concatenate(outs, axis=1)


def _gmm_kernel(ps_ref, nb_ref, nu_ref, b_ref, *rest, swiglu, tm, n_blocks):
    if swiglu:
        pick_ref, w_hbm, x_hbm, o_hbm, wbuf, wbf, xbuf, obuf, zbuf, wsem, sin, sout, zsem = rest
    else:
        w_hbm, x_hbm, o_hbm, wbuf, wbf, xbuf, obuf, zbuf, wsem, sin, sout, zsem = rest
    e = pl.program_id(0)
    j = pl.program_id(1)
    n_exp = pl.num_programs(0)
    nj = pl.num_programs(1)
    step = e * nj + j
    have_next = step < n_exp * nj - 1
    nb = nb_ref[e]
    g0 = nj * (ps_ref[e] // tm) + j * nb
    e_next = jnp.minimum(jnp.where(j + 1 < nj, e, e + 1), n_exp - 1)
    e_prev = jnp.maximum(jnp.where(j > 0, e, e - 1), 0)
    next_has = jnp.logical_and(have_next, nb_ref[e_next] > 0)
    first_here = jnp.logical_or(step == 0, nb_ref[e_prev] == 0)
    kdim, tn = wbf.shape
    tn_out = obuf.shape[2]
    col = pl.multiple_of(j * tn_out, tn_out)
    nch = kdim // W_CHUNK_ROWS

    def w_chunk(st, c):
        r = pl.ds(pl.multiple_of(c * W_CHUNK_ROWS, W_CHUNK_ROWS), W_CHUNK_ROWS)
        cols = pl.ds(pl.multiple_of((st % nj) * tn, tn), tn)
        return pltpu.make_async_copy(w_hbm.at[st // nj, r, cols], wbuf.at[r, :], wsem.at[0])

    def start_chunks(st, lo, hi):
        lax.fori_loop(lo, hi, lambda c, x: (w_chunk(st, c).start(), x)[1], 0)

    def x_copy(row0, sl):
        return pltpu.make_async_copy(x_hbm.at[pl.ds(pl.multiple_of(row0, tm), tm), :], xbuf.at[sl],
                                     sin.at[sl])

    def o_copy(b, sl):
        r = pl.ds(pl.multiple_of(ps_ref[e] + b * tm, tm), tm)
        return pltpu.make_async_copy(obuf.at[sl], o_hbm.at[r, pl.ds(col, tn_out)], sout.at[sl])

    @pl.when(step == 0)
    def _():
        start_chunks(step, 0, nch)

    @pl.when(jnp.logical_and(nb > 0, first_here))
    def _():
        x_copy(ps_ref[e], g0 % 2).start()

    lax.fori_loop(0, nch, lambda c, x: (w_chunk(step, c).wait(), x)[1], 0)

    @pl.when(nb > 0)
    def _():
        wbf[...] = wbuf[...].astype(BF16)

    per_block = (nch + jnp.maximum(nb, 1) - 1) // jnp.maximum(nb, 1)

    def block(b, c):
        sl = (g0 + b) % 2
        more = b + 1 < nb

        @pl.when(jnp.logical_or(more, next_has))
        def _():
            x_copy(jnp.where(more, ps_ref[e] + (b + 1) * tm, ps_ref[e_next]), 1 - sl).start()

        @pl.when(have_next)
        def _():
            start_chunks(step + 1, jnp.minimum(b * per_block, nch),
                         jnp.minimum((b + 1) * per_block, nch))

        x_copy(ps_ref[e] + b * tm, sl).wait()

        @pl.when(g0 + b >= 2)
        def _():
            o_copy(b, sl).wait()

        y = jnp.dot(xbuf[sl].astype(BF16), wbf[...], preferred_element_type=F32) + b_ref[...]
        if swiglu:
            y = _swiglu_pick(y, pick_ref[...])
        obuf[sl] = y.astype(obuf.dtype)
        o_copy(b, sl).start()
        return c

    lax.fori_loop(0, nb, block, 0)

    @pl.when(jnp.logical_and(have_next, nb == 0))
    def _():
        start_chunks(step + 1, 0, nch)

    @pl.when(e == n_exp - 1)
    def _():
        zbuf[...] = jnp.zeros(zbuf.shape, zbuf.dtype)

        def tail(b):
            return pltpu.make_async_copy(
                zbuf, o_hbm.at[pl.ds(pl.multiple_of(b * tm, tm), tm), pl.ds(col, tn_out)], zsem.at[0])

        lax.fori_loop(nu_ref[0], n_blocks, lambda b, c: (tail(b).start(), c)[1], 0)
        lax.fori_loop(nu_ref[0], n_blocks, lambda b, c: (tail(b).wait(), c)[1], 0)

    @pl.when(jnp.logical_not(have_next))
    def _():
        done = g0 + nb

        @pl.when(done >= 2)
        def _():
            o_copy(0, done % 2).wait()

        @pl.when(done >= 1)
        def _():
            o_copy(0, (done + 1) % 2).wait()


def _grouped_matmul(x, w, b, pstart, nblk, n_used, *, tm, tn, swiglu, out_dtype):
    n_slots, kdim = x.shape
    n_exp, _, n = w.shape
    n_out = n // 2 if swiglu else n
    tn_out = tn // 2 if swiglu else tn
    in_specs = [pl.BlockSpec((None, 1, tn), lambda e, j, *_: (e, 0, j))]
    args = [b.reshape(n_exp, 1, n)]
    if swiglu:
        src = jnp.arange(2 * LANES, dtype=jnp.int32)[:, None]
        dst = jnp.arange(2 * LANES, dtype=jnp.int32)[None, :]
        args.append((src == 2 * (dst % LANES) + dst // LANES).astype(BF16))
        in_specs.append(pl.BlockSpec((2 * LANES, 2 * LANES), lambda e, j, *_: (0, 0)))
    in_specs += [pl.BlockSpec(memory_space=pl.ANY)] * 2
    args += [w, x]
    return pl.pallas_call(
        functools.partial(_gmm_kernel, swiglu=swiglu, tm=tm, n_blocks=n_slots // tm),
        grid_spec=pltpu.PrefetchScalarGridSpec(
            num_scalar_prefetch=3,
            grid=(n_exp, n // tn),
            in_specs=in_specs,
            out_specs=pl.BlockSpec(memory_space=pl.ANY),
            scratch_shapes=[pltpu.VMEM((kdim, tn), F32), pltpu.VMEM((kdim, tn), BF16),
                            pltpu.VMEM((2, tm, kdim), x.dtype), pltpu.VMEM((2, tm, tn_out), out_dtype),
                            pltpu.VMEM((tm, tn_out), out_dtype),
                            pltpu.SemaphoreType.DMA((1,)), pltpu.SemaphoreType.DMA((2,)),
                            pltpu.SemaphoreType.DMA((2,)), pltpu.SemaphoreType.DMA((1,))]),
        out_shape=jax.ShapeDtypeStruct((n_slots, n_out), out_dtype),
        compiler_params=_cparams("arbitrary", "arbitrary"),
        name="moe_gate_up" if swiglu else "moe_down",
    )(pstart, nblk, n_used, *args)


def _row_copy(src_hbm, row, dst, r, sem):
    return pltpu.make_async_copy(src_hbm.at[pl.ds(row, 1), :], dst.at[pl.ds(r, 1), :], sem)


def _combine_kernel(dest_ref, x_ref, gate_ref, g_ref, ys_hbm, o_ref, buf_ref, sem, *, norm):
    i = pl.program_id(0)
    tc = x_ref.shape[0]
    slot = i % 2

    def gather(step, sl):
        def start(r, c):
            for k in range(TOP_K):
                _row_copy(ys_hbm, dest_ref[k, step * tc + r], buf_ref.at[sl, k], r,
                          sem.at[sl]).start(priority=k % 2)
            return c
        lax.fori_loop(0, tc, start, 0, unroll=8)

    @pl.when(i == 0)
    def _():
        gather(i, slot)

    @pl.when(i + 1 < pl.num_programs(0))
    def _():
        gather(i + 1, 1 - slot)

    for k in range(TOP_K):
        pltpu.make_async_copy(ys_hbm.at[pl.ds(0, tc), :], buf_ref.at[slot, k], sem.at[slot]).wait()
    y = x_ref[...]
    for k in range(TOP_K):
        y = y + gate_ref[:, k:k + 1] * buf_ref[slot, k]
    o_ref[...] = _rms(y, g_ref[...]) if norm else y


def _combine(x1, gates_t, dest, ys, g_final, norm, tc=256):
    t, d = x1.shape
    tc = min(tc, t)
    return pl.pallas_call(
        functools.partial(_combine_kernel, norm=norm),
        grid_spec=pltpu.PrefetchScalarGridSpec(
            num_scalar_prefetch=1,
            grid=(t // tc,),
            in_specs=[pl.BlockSpec((tc, d), lambda i, dest: (i, 0)),
                      pl.BlockSpec((tc, TOP_K), lambda i, dest: (i, 0)),
                      pl.BlockSpec((1, d), lambda i, dest: (0, 0)),
                      pl.BlockSpec(memory_space=pl.ANY)],
            out_specs=pl.BlockSpec((tc, d), lambda i, dest: (i, 0)),
            scratch_shapes=[pltpu.VMEM((2, TOP_K, tc, d), F32), pltpu.SemaphoreType.DMA((2,))]),
        out_shape=jax.ShapeDtypeStruct((t, d), F32),
        compiler_params=_cparams("arbitrary"),
        name="moe_combine",
    )(dest, x1, gates_t, g_final, ys)


def _mixer(x, layer, g_mix, w_in, conv_w, conv_b, dt_bias, a_log, d_skip, g_ssm, w_out):
    w_bf_t, w_dt_t = _to_bf16_t(jnp.swapaxes(w_in, 1, 2), layer, ntail=SSM_HEADS)
    q, k, v, z, xbc, dt = _in_proj(x, g_mix.reshape(1, -1), w_bf_t, w_dt_t)
    attn = _attention(q, k, v)
    ssm = _ssd(z, xbc, dt, conv_w, conv_b, dt_bias, a_log, d_skip, g_ssm)
    return _out_proj(x, attn, ssm, _to_bf16(w_out, layer))


def _moe(x1, g_ffn, w_router, b_router, w_gate_up, b_gate_up, w_down, b_down, g_out, norm, tm=256):
    t, d = x1.shape
    g_ffn = g_ffn.reshape(1, -1)
    idx, gates, pos, cnt = _router(x1, g_ffn, w_router, b_router)
    counts = cnt[:, 0]
    padded = (counts + tm - 1) // tm * tm
    pend = jnp.cumsum(padded)
    pstart = pend - padded
    experts = jnp.arange(N_EXPERTS, dtype=jnp.int32)
    dest = pos + jnp.sum(jnp.where(idx[None] == experts[:, None, None],
                                   pstart[:, None, None], 0), axis=0)
    n_blocks = (t * TOP_K) // tm + N_EXPERTS
    n_used = (pend[-1] // tm).reshape(1)
    nblk = padded // tm

    xs = _dispatch(x1, g_ffn, dest, pstart + counts, pend, n_used, n_blocks, tm)
    act = _grouped_matmul(xs, w_gate_up, b_gate_up, pstart, nblk, n_used,
                          tm=tm, tn=2048, swiglu=True, out_dtype=BF16)
    ys = _grouped_matmul(act, w_down, b_down, pstart, nblk, n_used,
                         tm=tm, tn=2048, swiglu=False, out_dtype=F32)
    return _combine(x1, gates.T, dest, ys, g_out.reshape(1, -1), norm)


def kernel(x, g_mix, w_in, conv_w, conv_b, dt_bias, a_log, d_skip, g_ssm, w_out, g_ffn, w_router,
           b_router, w_gate_up, b_gate_up, w_down, b_down, g_final):
    b, s, d = x.shape
    depth = g_mix.shape[0]
    outs = []
    for bi in range(b):
        xb = x[bi]
        for l in range(depth):
            x1 = _mixer(xb, l, g_mix[l], w_in, conv_w[l], conv_b[l], dt_bias[l], a_log[l],
                        d_skip[l], g_ssm[l], w_out)
            xb = _moe(x1, g_ffn[l], w_router[l], b_router[l], w_gate_up[l], b_gate_up[l],
                      w_down[l], b_down[l], g_final, norm=(l == depth - 1))
        outs.append(xb)
    return outs[0][None] if b == 1 else jnp.stack(outs)
```

```python
import functools

import jax
import jax.numpy as jnp
from jax import lax
from jax.experimental import pallas as pl
from jax.experimental.pallas import tpu as pltpu

F32 = jnp.float32
BF16 = jnp.bfloat16
HIGHEST = lax.Precision.HIGHEST

LANES = 128
HEAD_DIM = 64
SB_WIDTH = 1024
SSM_INNER = 1024
SSM_GROUPS = 2
SSM_STATE = 128
SSM_CONV = 4
SSM_CONV_DIM = SSM_INNER + 2 * SSM_GROUPS * SSM_STATE
SSM_HEADS = SSM_INNER // HEAD_DIM
N_EXPERTS = 32
TOP_K = 4
SWIGLU_LIMIT = 7.0
SWIGLU_ALPHA = 1.702
RMS_EPS = 1e-5

VMEM_LIMIT = 56 * 1024 * 1024

ATTN_DEAD = -110.0

W_CHUNK_ROWS = 256


def _cparams(*sem):
    return pltpu.CompilerParams(dimension_semantics=sem, vmem_limit_bytes=VMEM_LIMIT)


def _rms(x, g):
    return x * lax.rsqrt(jnp.mean(x * x, axis=-1, keepdims=True) + RMS_EPS) * g


def _sigmoid(x):
    return 1.0 / (1.0 + jnp.exp(-x))


def _softplus(x):
    return jnp.maximum(x, 0.0) + jnp.log(1.0 + jnp.exp(-jnp.abs(x)))


def _cast_kernel(w_ref, o_ref):
    o_ref[...] = w_ref[...].astype(o_ref.dtype)


def _to_bf16(w, layer, rows=256):
    _, k, n = w.shape
    return pl.pallas_call(
        _cast_kernel,
        grid=(k // rows,),
        in_specs=[pl.BlockSpec((None, rows, n), lambda i: (layer, i, 0))],
        out_specs=pl.BlockSpec((rows, n), lambda i: (i, 0)),
        out_shape=jax.ShapeDtypeStruct((k, n), BF16),
        compiler_params=_cparams("arbitrary"),
        name="cast_bf16",
    )(w)


def _cast_t_kernel(w_ref, o_ref, tail_ref, *, ntail):
    w = w_ref[...]
    o_ref[...] = w.astype(o_ref.dtype)
    tail_ref[...] = jnp.concatenate(
        [w[w.shape[0] - ntail:, :], jnp.zeros((LANES - ntail, w.shape[1]), F32)], axis=0)


def _to_bf16_t(w_t, layer, ntail, cols=256):
    _, n, k = w_t.shape
    return pl.pallas_call(
        functools.partial(_cast_t_kernel, ntail=ntail),
        grid=(k // cols,),
        in_specs=[pl.BlockSpec((None, n, cols), lambda i: (layer, 0, i))],
        out_specs=[pl.BlockSpec((n, cols), lambda i: (0, i)),
                   pl.BlockSpec((LANES, cols), lambda i: (0, i))],
        out_shape=[jax.ShapeDtypeStruct((n, k), BF16), jax.ShapeDtypeStruct((LANES, k), F32)],
        compiler_params=_cparams("arbitrary"),
        name="cast_bf16_t",
    )(w_t)


def _in_proj_kernel(x_ref, g_ref, w_ref, wdt_ref, q_ref, k_ref, v_ref, z_ref, xbc_ref, dt_ref):
    hb = _rms(x_ref[...], g_ref[...]).astype(BF16)
    nt = (((1,), (1,)), ((), ()))

    def mm(a, b):
        return lax.dot_general(hb, w_ref[a:b, :], nt, preferred_element_type=F32)

    o = 0
    q_ref[...] = (mm(o, o + SB_WIDTH) * (HEAD_DIM ** -0.5)).astype(BF16)
    o += SB_WIDTH
    k_ref[...] = mm(o, o + SB_WIDTH).astype(BF16)
    o += SB_WIDTH
    v_ref[...] = mm(o, o + SB_WIDTH).astype(BF16)
    o += SB_WIDTH
    z_ref[...] = mm(o, o + SSM_INNER)
    o += SSM_INNER
    xbc_ref[...] = mm(o, o + SSM_CONV_DIM)
    dt_ref[...] = lax.dot_general(hb, wdt_ref[...].astype(BF16), nt, preferred_element_type=F32)


def _in_proj(x, g, w_bf_t, w_dt_t, tm=256):
    t, d = x.shape
    n = w_bf_t.shape[0]
    row = lambda w: pl.BlockSpec((tm, w), lambda i: (i, 0))
    outs = [(SB_WIDTH, BF16)] * 3 + [(SSM_INNER, F32), (SSM_CONV_DIM, F32), (LANES, F32)]
    return pl.pallas_call(
        _in_proj_kernel,
        grid=(t // tm,),
        in_specs=[row(d),
                  pl.BlockSpec((1, d), lambda i: (0, 0)),
                  pl.BlockSpec((n, d), lambda i: (0, 0), pipeline_mode=pl.Buffered(1)),
                  pl.BlockSpec((LANES, d), lambda i: (0, 0))],
        out_specs=[row(w) for w, _ in outs],
        out_shape=[jax.ShapeDtypeStruct((t, w), dt) for w, dt in outs],
        compiler_params=_cparams("arbitrary"),
        name="in_proj",
    )(x, g, w_bf_t, w_dt_t)


def _attn_kernel(q_ref, k_ref, v_ref, o_ref, acc_ref, carry_ref, *, tq):
    i = pl.program_id(1)
    nsub = tq // LANES
    qb0 = i * nsub
    row = lax.broadcasted_iota(jnp.int32, (LANES, LANES), 0)
    col = lax.broadcasted_iota(jnp.int32, (LANES, LANES), 1)
    past = col < row
    lo = col < HEAD_DIM
    jj = lax.broadcasted_iota(jnp.int32, (LANES, 2 * LANES), 0)
    ss = lax.broadcasted_iota(jnp.int32, (LANES, 2 * LANES), 1)
    neg_later = jnp.where((ss >= LANES) | (jj > ss), -1.0, 0.0).astype(BF16)
    nt = (((1,), (1,)), ((), ()))

    def step(n, diag):
        k0s = [pl.multiple_of(jnp.maximum(qb0 + s - n, 0) * LANES, LANES) for s in range(nsub)]
        chains = [(s, hd) for s in range(nsub) for hd in range(2)]
        logits = []
        for s, hd in chains:
            q = q_ref[s * LANES:(s + 1) * LANES, :]
            qh = jnp.where(lo if hd == 0 else jnp.logical_not(lo), q, jnp.zeros(q.shape, q.dtype))
            kt = k_ref[pl.ds(k0s[s], LANES), :]
            logits.append(lax.dot_general(qh, kt, nt, preferred_element_type=F32))
        logsig, sums = [], []
        for l in logits:
            sp = _softplus(l)
            lk = jnp.where(past, sp, 0.0) if diag else sp
            sums.append(jnp.dot(lk.astype(BF16), neg_later, preferred_element_type=F32))
            logsig.append(l - sp)
        top = None
        for c, (s, hd) in enumerate(chains):
            r = sums[c]
            if diag:
                w = jnp.where(past, jnp.exp(logsig[c] + r[:, :LANES]), 0.0)
                carry = r[:, LANES:]
            else:
                before = jnp.where(qb0 + s - n >= 0, carry_ref[c], -1e30)
                w = jnp.exp(logsig[c] + r[:, :LANES] + before)
                carry = before + r[:, LANES:]
            vt = v_ref[pl.ds(k0s[s], LANES), :]
            pv = jnp.dot(w.astype(BF16), vt, preferred_element_type=F32)
            if diag:
                acc_ref[c] = pv
            else:
                acc_ref[c] += pv
            carry_ref[c] = carry
            top = carry if top is None else jnp.maximum(top, carry)
        return jnp.max(top)

    top0 = step(0, True)

    def cond(st):
        n, top = st
        return jnp.logical_and(n <= qb0 + nsub - 1, top > ATTN_DEAD)

    def body(st):
        n, _ = st
        return n + 1, step(n, False)

    lax.while_loop(cond, body, (jnp.int32(1), top0))
    for s in range(nsub):
        o_ref[s * LANES:(s + 1) * LANES, :] = jnp.where(
            lo, acc_ref[2 * s], acc_ref[2 * s + 1]).astype(o_ref.dtype)


def _attention(q, k, v, tq=1024):
    t, w = q.shape
    tq = min(tq, t)
    chains = 2 * (tq // LANES)
    return pl.pallas_call(
        functools.partial(_attn_kernel, tq=tq),
        grid=(w // LANES, t // tq),
        in_specs=[pl.BlockSpec((tq, LANES), lambda p, i: (i, p)),
                  pl.BlockSpec((t, LANES), lambda p, i: (0, p)),
                  pl.BlockSpec((t, LANES), lambda p, i: (0, p))],
        out_specs=pl.BlockSpec((tq, LANES), lambda p, i: (i, p)),
        out_shape=jax.ShapeDtypeStruct((t, w), BF16),
        scratch_shapes=[pltpu.VMEM((chains, LANES, LANES), F32),
                        pltpu.VMEM((chains, LANES, LANES), F32)],
        compiler_params=_cparams("arbitrary", "arbitrary"),
        name="sb_attention",
    )(q, k, v)


def _ssd_kernel(z_ref, xbc_ref, dt_ref, cw_ref, cb_ref, dtb_ref, alog_ref, dsk_ref, g_ref, e_ref,
                o_ref, state_ref, xcat_ref):
    c = pl.program_id(0)
    L = z_ref.shape[0]
    half = SSM_INNER // SSM_GROUPS
    pad = 8

    @pl.when(c == 0)
    def _():
        state_ref[...] = jnp.zeros(state_ref.shape, F32)
        xcat_ref[0:pad, :] = jnp.zeros((pad, SSM_CONV_DIM), F32)

    xcat_ref[pad:pad + L, :] = xbc_ref[...]
    acc = jnp.broadcast_to(cb_ref[...], (L, SSM_CONV_DIM))
    for kk in range(SSM_CONV):
        s = pad - (SSM_CONV - 1) + kk
        acc = acc + cw_ref[kk:kk + 1, :] * xcat_ref[s:s + L, :]
    xcat_ref[0:pad, :] = xcat_ref[L:L + pad, :]
    xc = acc * _sigmoid(acc)
    xs = xc[:, :SSM_INNER]
    bm = xc[:, SSM_INNER:SSM_INNER + SSM_GROUPS * SSM_STATE]
    cm = xc[:, SSM_INNER + SSM_GROUPS * SSM_STATE:]

    dtv = _softplus(dt_ref[...] + dtb_ref[...])
    adt = dtv * (-jnp.exp(alog_ref[...]))
    ri = lax.broadcasted_iota(jnp.int32, (L, L), 0)
    ci = lax.broadcasted_iota(jnp.int32, (L, L), 1)
    causal = ri >= ci
    tri = jnp.where(causal, 1.0, 0.0).astype(F32)
    acum = jnp.dot(tri, adt, precision=HIGHEST, preferred_element_type=F32)
    expand = e_ref[...]
    acx = jnp.dot(acum, expand, precision=HIGHEST, preferred_element_type=F32)
    dtx = jnp.dot(dtv, expand, precision=HIGHEST, preferred_element_type=F32)
    alx = acx[L - 1:L, :]
    xdt = xs * dtx
    acum_t = acum.T
    lo = lax.broadcasted_iota(jnp.int32, (L, LANES), 1) < HEAD_DIM
    nt = (((1,), (1,)), ((), ()))

    xdt_b = xdt.astype(BF16)
    y_parts = []
    for g in range(SSM_GROUPS):
        cg = cm[:, g * SSM_STATE:(g + 1) * SSM_STATE].astype(BF16)
        bg = bm[:, g * SSM_STATE:(g + 1) * SSM_STATE]
        y_off = jnp.dot(cg, state_ref[:, g * half:(g + 1) * half].astype(BF16),
                        preferred_element_type=F32)
        cb = lax.dot_general(cg, bg.astype(BF16), nt, preferred_element_type=F32)
        for p in range(half // LANES):
            pair = g * (half // LANES) + p
            xp = xdt_b[:, pair * LANES:(pair + 1) * LANES]
            yh = []
            for hh in range(2):
                h = 2 * pair + hh
                seg = acum[:, h:h + 1] - acum_t[h:h + 1, :]
                dec = jnp.where(causal, jnp.exp(jnp.minimum(seg, 0.0)), 0.0)
                yh.append(jnp.dot((cb * dec).astype(BF16), xp, preferred_element_type=F32))
            y_parts.append(jnp.where(lo, yh[0], yh[1])
                           + y_off[:, p * LANES:(p + 1) * LANES]
                           * jnp.exp(acx[:, pair * LANES:(pair + 1) * LANES]))
        xd = xdt[:, g * half:(g + 1) * half] * jnp.exp(alx[:, g * half:(g + 1) * half]
                                                       - acx[:, g * half:(g + 1) * half])
        upd = jnp.dot(bg.T.astype(BF16), xd.astype(BF16), preferred_element_type=F32)
        state_ref[:, g * half:(g + 1) * half] = (
            jnp.exp(alx[:, g * half:(g + 1) * half]) * state_ref[:, g * half:(g + 1) * half] + upd)

    y = jnp.concatenate(y_parts, axis=1) + dsk_ref[...] * xs
    zz = z_ref[...]
    y = y * (zz * _sigmoid(zz))
    outs = []
    for g in range(SSM_GROUPS):
        yg = y[:, g * half:(g + 1) * half]
        outs.append(yg * lax.rsqrt(jnp.mean(yg * yg, axis=-1, keepdims=True) + RMS_EPS))
    o_ref[...] = (jnp.concatenate(outs, axis=1) * g_ref[...]).astype(o_ref.dtype)


def _ssd(z, xbc, dt, conv_w, conv_b, dt_bias, a_log, d_skip, g_ssm, chunk=128):
    t = z.shape[0]
    chunk = min(chunk, t)
    padh = lambda a: jnp.pad(a.reshape(1, -1), ((0, 0), (0, LANES - a.shape[-1])))
    heads = jnp.arange(SSM_INNER, dtype=jnp.int32) // HEAD_DIM
    expand = (jnp.arange(LANES, dtype=jnp.int32)[:, None] == heads[None, :]).astype(F32)
    row = lambda w: pl.BlockSpec((chunk, w), lambda c: (c, 0))
    full = lambda a: pl.BlockSpec(a.shape, lambda c: (0, 0))
    params = [conv_w, conv_b.reshape(1, -1), padh(dt_bias), padh(a_log),
              jnp.repeat(d_skip, HEAD_DIM).reshape(1, -1), g_ssm.reshape(1, -1), expand]
    return pl.pallas_call(
        _ssd_kernel,
        grid=(t // chunk,),
        in_specs=[row(SSM_INNER), row(SSM_CONV_DIM), row(LANES)] + [full(a) for a in params],
        out_specs=row(SSM_INNER),
        out_shape=jax.ShapeDtypeStruct((t, SSM_INNER), BF16),
        scratch_shapes=[pltpu.VMEM((SSM_STATE, SSM_INNER), F32),
                        pltpu.VMEM((chunk + 8, SSM_CONV_DIM), F32)],
        compiler_params=_cparams("arbitrary"),
        name="ssd_mixer",
    )(z, xbc, dt, *params)


def _route(x, g_ref, wt_ref, b_ref, idx_ref, gate_ref, pos_ref, cnt_ref, base_ref):
    i = pl.program_id(0)
    tm = x.shape[0]

    @pl.when(i == 0)
    def _():
        base_ref[...] = jnp.zeros(base_ref.shape, F32)

    h = _rms(x, g_ref[...])
    nt = (((1,), (1,)), ((), ()))
    w = wt_ref[...]
    w_hi = w.astype(BF16)
    w_lo = (w - w_hi.astype(F32)).astype(BF16)
    h_hi = h.astype(BF16)
    h_lo = (h - h_hi.astype(F32)).astype(BF16)
    logits = (lax.dot_general(w_hi, h_hi, nt, preferred_element_type=F32)
              + lax.dot_general(w_hi, h_lo, nt, preferred_element_type=F32)
              + lax.dot_general(w_lo, h_hi, nt, preferred_element_type=F32)) + b_ref[:, 0:1]
    eio = lax.broadcasted_iota(jnp.int32, (N_EXPERTS, tm), 0).astype(F32)
    work = logits
    vals, hots = [], []
    for k in range(TOP_K):
        m = jnp.max(work, axis=0, keepdims=True)
        idx = jnp.min(jnp.where(work == m, eio, float(N_EXPERTS)), axis=0, keepdims=True)
        hot = eio == idx
        work = jnp.where(hot, -jnp.inf, work)
        vals.append(m)
        hots.append(hot)
        idx_ref[k:k + 1, :] = idx.astype(jnp.int32)
    ex = [jnp.exp(v - vals[0]) for v in vals]
    den = ex[0] + ex[1] + ex[2] + ex[3]
    picked = jnp.zeros((N_EXPERTS, tm), F32)
    for k in range(TOP_K):
        gate_ref[k:k + 1, :] = ex[k] / den
        picked = picked + jnp.where(hots[k], 1.0, 0.0)
    ti = lax.broadcasted_iota(jnp.int32, (tm, tm), 0)
    tj = lax.broadcasted_iota(jnp.int32, (tm, tm), 1)
    before = jnp.where(ti < tj, 1.0, 0.0).astype(BF16)
    rank = jnp.dot(picked.astype(BF16), before, preferred_element_type=F32) + base_ref[:, 0:1]
    for k in range(TOP_K):
        pos = jnp.sum(jnp.where(hots[k], rank, 0.0), axis=0, keepdims=True)
        pos_ref[k:k + 1, :] = pos.astype(jnp.int32)
    base_ref[...] = base_ref[...] + jnp.sum(picked, axis=1, keepdims=True)
    cnt_ref[...] = base_ref[...].astype(jnp.int32)


def _out_proj_route_kernel(x_ref, a_ref, s_ref, w_ref, g_ref, wt_ref, b_ref,
                           o_ref, idx_ref, gate_ref, pos_ref, cnt_ref, base_ref):
    x1 = (x_ref[...]
          + jnp.dot(a_ref[...], w_ref[0:SB_WIDTH, :], preferred_element_type=F32)
          + jnp.dot(s_ref[...], w_ref[SB_WIDTH:, :], preferred_element_type=F32))
    o_ref[...] = x1
    _route(x1, g_ref, wt_ref, b_ref, idx_ref, gate_ref, pos_ref, cnt_ref, base_ref)


def _out_proj_route(x, attn, ssm, w_bf, g, w_router, b_router, tm=512):
    t, d = x.shape
    tm = min(tm, t)
    wt = w_router.T
    b = jnp.broadcast_to(b_router.reshape(-1, 1), (N_EXPERTS, LANES))
    tok = lambda dt: (pl.BlockSpec((TOP_K, tm), lambda i: (0, i)), jax.ShapeDtypeStruct((TOP_K, t), dt))
    specs = [(pl.BlockSpec((tm, d), lambda i: (i, 0)), jax.ShapeDtypeStruct((t, d), F32)),
             tok(jnp.int32), tok(F32), tok(jnp.int32),
             (pl.BlockSpec((N_EXPERTS, LANES), lambda i: (0, 0)),
              jax.ShapeDtypeStruct((N_EXPERTS, LANES), jnp.int32))]
    return pl.pallas_call(
        _out_proj_route_kernel,
        grid=(t // tm,),
        in_specs=[pl.BlockSpec((tm, d), lambda i: (i, 0)),
                  pl.BlockSpec((tm, SB_WIDTH), lambda i: (i, 0)),
                  pl.BlockSpec((tm, SSM_INNER), lambda i: (i, 0)),
                  pl.BlockSpec(w_bf.shape, lambda i: (0, 0), pipeline_mode=pl.Buffered(1)),
                  pl.BlockSpec((1, d), lambda i: (0, 0)),
                  pl.BlockSpec((N_EXPERTS, d), lambda i: (0, 0)),
                  pl.BlockSpec((N_EXPERTS, LANES), lambda i: (0, 0))],
        out_specs=[s for s, _ in specs],
        out_shape=[s for _, s in specs],
        scratch_shapes=[pltpu.VMEM((N_EXPERTS, LANES), F32)],
        compiler_params=_cparams("arbitrary"),
        name="out_proj_route",
    )(x, attn, ssm, w_bf, g, wt, b)


def _dispatch_kernel(dest_ref, lo_ref, hi_ref, nu_ref, x_ref, g_ref, xs_hbm, hbuf, zbuf, sem, zsem,
                     *, n_blocks):
    i = pl.program_id(0)
    last = pl.num_programs(0) - 1
    tm = x_ref.shape[0]
    slot = i % 2

    def row_copies(step, sl, r):
        return [pltpu.make_async_copy(hbuf.at[sl, pl.ds(r, 1), :],
                                      xs_hbm.at[pl.ds(dest_ref[k, step * tm + r], 1), :], sem.at[sl])
                for k in range(TOP_K)]

    def wait_step(sl):
        for _ in range(TOP_K):
            pltpu.make_async_copy(hbuf.at[sl], xs_hbm.at[pl.ds(0, tm), :], sem.at[sl]).wait()

    @pl.when(i >= 2)
    def _():
        wait_step(slot)

    hbuf[slot] = _rms(x_ref[...], g_ref[...])

    def start(r, c):
        for k, cp in enumerate(row_copies(i, slot, r)):
            cp.start(priority=k % 2)
        return c
    lax.fori_loop(0, tm, start, 0, unroll=8)

    @pl.when(i == 0)
    def _():
        zbuf[...] = jnp.zeros(zbuf.shape, F32)

        def pad_row(r):
            return pltpu.make_async_copy(zbuf.at[pl.ds(0, 1), :], xs_hbm.at[pl.ds(r, 1), :], zsem.at[0])

        def tail_block(b):
            return pltpu.make_async_copy(zbuf, xs_hbm.at[pl.ds(pl.multiple_of(b * tm, tm), tm), :],
                                         zsem.at[0])

        def each(fn):
            def per_expert(e, c):
                return lax.fori_loop(lo_ref[e], hi_ref[e], lambda r, c2: (fn(pad_row(r)), c2)[1], c)
            lax.fori_loop(0, N_EXPERTS, per_expert, 0)
            lax.fori_loop(nu_ref[0], n_blocks, lambda b, c: (fn(tail_block(b)), c)[1], 0)

        each(lambda cp: cp.start())
        each(lambda cp: cp.wait())

    @pl.when(i == last)
    def _():
        @pl.when(i >= 1)
        def _():
            wait_step(1 - slot)
        wait_step(slot)


def _dispatch(x1, g, dest, pad_lo, pad_hi, n_used, n_blocks, tm):
    t, d = x1.shape
    return pl.pallas_call(
        functools.partial(_dispatch_kernel, n_blocks=n_blocks),
        grid_spec=pltpu.PrefetchScalarGridSpec(
            num_scalar_prefetch=4,
            grid=(t // tm,),
            in_specs=[pl.BlockSpec((tm, d), lambda i, *_: (i, 0)),
                      pl.BlockSpec((1, d), lambda i, *_: (0, 0))],
            out_specs=pl.BlockSpec(memory_space=pl.ANY),
            scratch_shapes=[pltpu.VMEM((2, tm, d), F32), pltpu.VMEM((tm, d), F32),
                            pltpu.SemaphoreType.DMA((2,)), pltpu.SemaphoreType.DMA((1,))]),
        out_shape=jax.ShapeDtypeStruct((n_blocks * tm, d), F32),
        compiler_params=_cparams("arbitrary"),
        name="moe_dispatch",
    )(dest, pad_lo, pad_hi, n_used, x1, g)


def _swiglu_pick(y, pick):
    tm, n2 = y.shape
    even = lax.broadcasted_iota(jnp.int32, (tm, 2 * LANES), 1) % 2 == 0
    outs = []
    for cblk in range(n2 // (2 * LANES)):
        yc = y[:, cblk * 2 * LANES:(cblk + 1) * 2 * LANES]
        capped = jnp.minimum(yc, SWIGLU_LIMIT)
        gate = capped * _sigmoid(SWIGLU_ALPHA * capped)
        up = jnp.maximum(capped, -SWIGLU_LIMIT) + 1.0
        factors = jnp.where(even, gate, up).astype(BF16)
        split = jnp.dot(factors, pick, preferred_element_type=F32)
        outs.append(split[:, :LANES] * split[:, LANES:])
    return jnp.concatenate(outs, axis=1)


def _gmm_kernel(ps_ref, nb_ref, nu_ref, b_ref, *rest, swiglu, tm, n_blocks):
    if swiglu:
        pick_ref, w_hbm, x_hbm, o_hbm, wbuf, wbf, xbuf, obuf, zbuf, wsem, sin, sout, zsem = rest
    else:
        w_hbm, x_hbm, o_hbm, wbuf, wbf, xbuf, obuf, zbuf, wsem, sin, sout, zsem = rest
    e = pl.program_id(0)
    j = pl.program_id(1)
    n_exp = pl.num_programs(0)
    nj = pl.num_programs(1)
    step = e * nj + j
    have_next = step < n_exp * nj - 1
    nb = nb_ref[e]
    g0 = nj * (ps_ref[e] // tm) + j * nb
    e_next = jnp.minimum(jnp.where(j + 1 < nj, e, e + 1), n_exp - 1)
    e_prev = jnp.maximum(jnp.where(j > 0, e, e - 1), 0)
    next_has = jnp.logical_and(have_next, nb_ref[e_next] > 0)
    first_here = jnp.logical_or(step == 0, nb_ref[e_prev] == 0)
    kdim, tn = wbf.shape
    tn_out = obuf.shape[2]
    col = pl.multiple_of(j * tn_out, tn_out)
    nch = kdim // W_CHUNK_ROWS

    def w_chunk(st, c):
        r = pl.ds(pl.multiple_of(c * W_CHUNK_ROWS, W_CHUNK_ROWS), W_CHUNK_ROWS)
        cols = pl.ds(pl.multiple_of((st % nj) * tn, tn), tn)
        return pltpu.make_async_copy(w_hbm.at[st // nj, r, cols], wbuf.at[r, :], wsem.at[0])

    def start_chunks(st, lo, hi):
        lax.fori_loop(lo, hi, lambda c, x: (w_chunk(st, c).start(), x)[1], 0)

    def x_copy(row0, sl):
        return pltpu.make_async_copy(x_hbm.at[pl.ds(pl.multiple_of(row0, tm), tm), :], xbuf.at[sl],
                                     sin.at[sl])

    def o_copy(b, sl):
        r = pl.ds(pl.multiple_of(ps_ref[e] + b * tm, tm), tm)
        return pltpu.make_async_copy(obuf.at[sl], o_hbm.at[r, pl.ds(col, tn_out)], sout.at[sl])

    @pl.when(step == 0)
    def _():
        start_chunks(step, 0, nch)

    @pl.when(jnp.logical_and(nb > 0, first_here))
    def _():
        x_copy(ps_ref[e], g0 % 2).start()

    lax.fori_loop(0, nch, lambda c, x: (w_chunk(step, c).wait(), x)[1], 0)

    @pl.when(nb > 0)
    def _():
        wbf[...] = wbuf[...].astype(BF16)

    per_block = (nch + jnp.maximum(nb, 1) - 1) // jnp.maximum(nb, 1)

    def block(b, c):
        sl = (g0 + b) % 2
        more = b + 1 < nb

        @pl.when(jnp.logical_or(more, next_has))
        def _():
            x_copy(jnp.where(more, ps_ref[e] + (b + 1) * tm, ps_ref[e_next]), 1 - sl).start()

        @pl.when(have_next)
        def _():
            start_chunks(step + 1, jnp.minimum(b * per_block, nch),
                         jnp.minimum((b + 1) * per_block, nch))

        x_copy(ps_ref[e] + b * tm, sl).wait()

        @pl.when(g0 + b >= 2)
        def _():
            o_copy(b, sl).wait()

        y = jnp.dot(xbuf[sl].astype(BF16), wbf[...], preferred_element_type=F32) + b_ref[...]
        if swiglu:
            y = _swiglu_pick(y, pick_ref[...])
        obuf[sl] = y.astype(obuf.dtype)
        o_copy(b, sl).start()
        return c

    lax.fori_loop(0, nb, block, 0)

    @pl.when(jnp.logical_and(have_next, nb == 0))
    def _():
        start_chunks(step + 1, 0, nch)

    @pl.when(e == n_exp - 1)
    def _():
        zbuf[...] = jnp.zeros(zbuf.shape, zbuf.dtype)

        def tail(b):
            return pltpu.make_async_copy(
                zbuf, o_hbm.at[pl.ds(pl.multiple_of(b * tm, tm), tm), pl.ds(col, tn_out)], zsem.at[0])

        lax.fori_loop(nu_ref[0], n_blocks, lambda b, c: (tail(b).start(), c)[1], 0)
        lax.fori_loop(nu_ref[0], n_blocks, lambda b, c: (tail(b).wait(), c)[1], 0)

    @pl.when(jnp.logical_not(have_next))
    def _():
        done = g0 + nb

        @pl.when(done >= 2)
        def _():
            o_copy(0, done % 2).wait()

        @pl.when(done >= 1)
        def _():
            o_copy(0, (done + 1) % 2).wait()


def _grouped_matmul(x, w, b, pstart, nblk, n_used, *, tm, tn, swiglu, out_dtype):
    n_slots, kdim = x.shape
    n_exp, _, n = w.shape
    n_out = n // 2 if swiglu else n
    tn_out = tn // 2 if swiglu else tn
    in_specs = [pl.BlockSpec((None, 1, tn), lambda e, j, *_: (e, 0, j))]
    args = [b.reshape(n_exp, 1, n)]
    if swiglu:
        src = jnp.arange(2 * LANES, dtype=jnp.int32)[:, None]
        dst = jnp.arange(2 * LANES, dtype=jnp.int32)[None, :]
        args.append((src == 2 * (dst % LANES) + dst // LANES).astype(BF16))
        in_specs.append(pl.BlockSpec((2 * LANES, 2 * LANES), lambda e, j, *_: (0, 0)))
    in_specs += [pl.BlockSpec(memory_space=pl.ANY)] * 2
    args += [w, x]
    return pl.pallas_call(
        functools.partial(_gmm_kernel, swiglu=swiglu, tm=tm, n_blocks=n_slots // tm),
        grid_spec=pltpu.PrefetchScalarGridSpec(
            num_scalar_prefetch=3,
            grid=(n_exp, n // tn),
            in_specs=in_specs,
            out_specs=pl.BlockSpec(memory_space=pl.ANY),
            scratch_shapes=[pltpu.VMEM((kdim, tn), F32), pltpu.VMEM((kdim, tn), BF16),
                            pltpu.VMEM((2, tm, kdim), x.dtype), pltpu.VMEM((2, tm, tn_out), out_dtype),
                            pltpu.VMEM((tm, tn_out), out_dtype),
                            pltpu.SemaphoreType.DMA((1,)), pltpu.SemaphoreType.DMA((2,)),
                            pltpu.SemaphoreType.DMA((2,)), pltpu.SemaphoreType.DMA((1,))]),
        out_shape=jax.ShapeDtypeStruct((n_slots, n_out), out_dtype),
        compiler_params=_cparams("arbitrary", "arbitrary"),
        name="moe_gate_up" if swiglu else "moe_down",
    )(pstart, nblk, n_used, *args)


def _row_copy(src_hbm, row, dst, r, sem):
    return pltpu.make_async_copy(src_hbm.at[pl.ds(row, 1), :], dst.at[pl.ds(r, 1), :], sem)


def _combine_kernel(dest_ref, x_ref, gate_ref, g_ref, ys_hbm, o_ref, buf_ref, sem, *, norm):
    i = pl.program_id(0)
    tc = x_ref.shape[0]
    slot = i % 2

    def gather(step, sl):
        def start(r, c):
            for k in range(TOP_K):
                _row_copy(ys_hbm, dest_ref[k, step * tc + r], buf_ref.at[sl, k], r,
                          sem.at[sl]).start(priority=k % 2)
            return c
        lax.fori_loop(0, tc, start, 0, unroll=8)

    @pl.when(i == 0)
    def _():
        gather(i, slot)

    @pl.when(i + 1 < pl.num_programs(0))
    def _():
        gather(i + 1, 1 - slot)

    for k in range(TOP_K):
        pltpu.make_async_copy(ys_hbm.at[pl.ds(0, tc), :], buf_ref.at[slot, k], sem.at[slot]).wait()
    y = x_ref[...]
    for k in range(TOP_K):
        y = y + gate_ref[:, k:k + 1] * buf_ref[slot, k]
    o_ref[...] = _rms(y, g_ref[...]) if norm else y


def _combine(x1, gates_t, dest, ys, g_final, norm, tc=256):
    t, d = x1.shape
    tc = min(tc, t)
    return pl.pallas_call(
        functools.partial(_combine_kernel, norm=norm),
        grid_spec=pltpu.PrefetchScalarGridSpec(
            num_scalar_prefetch=1,
            grid=(t // tc,),
            in_specs=[pl.BlockSpec((tc, d), lambda i, dest: (i, 0)),
                      pl.BlockSpec((tc, TOP_K), lambda i, dest: (i, 0)),
                      pl.BlockSpec((1, d), lambda i, dest: (0, 0)),
                      pl.BlockSpec(memory_space=pl.ANY)],
            out_specs=pl.BlockSpec((tc, d), lambda i, dest: (i, 0)),
            scratch_shapes=[pltpu.VMEM((2, TOP_K, tc, d), F32), pltpu.SemaphoreType.DMA((2,))]),
        out_shape=jax.ShapeDtypeStruct((t, d), F32),
        compiler_params=_cparams("arbitrary"),
        name="moe_combine",
    )(dest, x1, gates_t, g_final, ys)


def _mixer(x, layer, g_mix, w_in, conv_w, conv_b, dt_bias, a_log, d_skip, g_ssm, w_out,
           g_ffn, w_router, b_router):
    w_bf_t, w_dt_t = _to_bf16_t(jnp.swapaxes(w_in, 1, 2), layer, ntail=SSM_HEADS)
    q, k, v, z, xbc, dt = _in_proj(x, g_mix.reshape(1, -1), w_bf_t, w_dt_t)
    attn = _attention(q, k, v)
    ssm = _ssd(z, xbc, dt, conv_w, conv_b, dt_bias, a_log, d_skip, g_ssm)
    x1, *routing = _out_proj_route(x, attn, ssm, _to_bf16(w_out, layer), g_ffn.reshape(1, -1),
                                   w_router, b_router)
    return x1, routing


def _moe(x1, routing, g_ffn, w_gate_up, b_gate_up, w_down, b_down, g_out, norm, tm=256):
    t, d = x1.shape
    g_ffn = g_ffn.reshape(1, -1)
    idx, gates, pos, cnt = routing
    counts = cnt[:, 0]
    padded = (counts + tm - 1) // tm * tm
    pend = jnp.cumsum(padded)
    pstart = pend - padded
    experts = jnp.arange(N_EXPERTS, dtype=jnp.int32)
    dest = pos + jnp.sum(jnp.where(idx[None] == experts[:, None, None],
                                   pstart[:, None, None], 0), axis=0)
    n_blocks = (t * TOP_K) // tm + N_EXPERTS
    n_used = (pend[-1] // tm).reshape(1)
    nblk = padded // tm

    xs = _dispatch(x1, g_ffn, dest, pstart + counts, pend, n_used, n_blocks, tm)
    act = _grouped_matmul(xs, w_gate_up, b_gate_up, pstart, nblk, n_used,
                          tm=tm, tn=2048, swiglu=True, out_dtype=BF16)
    ys = _grouped_matmul(act, w_down, b_down, pstart, nblk, n_used,
                         tm=tm, tn=2048, swiglu=False, out_dtype=F32)
    return _combine(x1, gates.T, dest, ys, g_out.reshape(1, -1), norm)


def kernel(x, g_mix, w_in, conv_w, conv_b, dt_bias, a_log, d_skip, g_ssm, w_out, g_ffn, w_router,
           b_router, w_gate_up, b_gate_up, w_down, b_down, g_final):
    b, s, d = x.shape
    depth = g_mix.shape[0]
    outs = []
    for bi in range(b):
        xb = x[bi]
        for l in range(depth):
            x1, routing = _mixer(xb, l, g_mix[l], w_in, conv_w[l], conv_b[l], dt_bias[l], a_log[l],
                                 d_skip[l], g_ssm[l], w_out, g_ffn[l], w_router[l], b_router[l])
            xb = _moe(x1, routing, g_ffn[l], w_gate_up[l], b_gate_up[l],
                      w_down[l], b_down[l], g_final, norm=(l == depth - 1))
        outs.append(xb)
    return outs[0][None] if b == 1 else jnp.stack(outs)
```
